```python
import jax
import jax.numpy as jnp
from jax import lax
import numpy as np


D_MODEL = 2048
BATCH = 4
SEQ = 4096
DEPTH = 4

HEAD_DIM = 128
ROPE_THETA = 500000.0
ROPE_DIM = HEAD_DIM // 4
N_MEM = 256
X_HEADS = 4
MOBA_HEADS = 8
MOBA_BLOCK = 256
MOBA_TOPK = 3
MOBA_QCHUNK = 32
NSA_HEADS = 8
NSA_GROUPS = 2
NSA_CMP_LEN = 32
NSA_CMP_STRIDE = 16
NSA_SEL_LEN = 64
NSA_TOPK = 16
NSA_WINDOW = 512
NSA_QCHUNK = 64
WIN_QBLOCK = 128
RET_HEADS = 8
RET_DK = 256
RET_DV = 512
RET_CHUNK = 128
RET_THETA = 10000.0
D_FF = 5632
CONV_WIDTH = 3
N_EVEN = (DEPTH + 1) // 2
N_ODD = DEPTH // 2
RMS_EPS = 1e-6
NEG = -1e30
FORCE = 1e9

AB_SIZES = (MOBA_HEADS * HEAD_DIM,) * 3 + (NSA_HEADS * HEAD_DIM,) + (NSA_GROUPS * HEAD_DIM,) * 6 + (NSA_HEADS * 3,)
AB_COLS = sum(AB_SIZES)
C_SIZES = (RET_HEADS * RET_DK,) * 2 + (RET_HEADS * RET_DV,) * 2
C_COLS = sum(C_SIZES)

kernel_name = 'hybrid_moba_nsa_retention_convffn'


def _split(t, sizes):
    out, start = [], 0
    for s in sizes:
        out.append(t[..., start:start + s])
        start += s
    return out


def _rmsnorm(x, g):
    xf = x.astype(jnp.float32)
    y = xf * lax.rsqrt(jnp.mean(jnp.square(xf), axis=-1, keepdims=True) + RMS_EPS)
    return (y * g.astype(jnp.float32)).astype(x.dtype)


def _heads(t, n):
    B, S, _ = t.shape
    return t.reshape(B, S, n, -1).transpose(0, 2, 1, 3)


def _merge(t):
    B, H, S, d = t.shape
    return t.transpose(0, 2, 1, 3).reshape(B, S, H * d)


def _rope(x, positions, rot_dim, theta):
    half = rot_dim // 2
    inv = jnp.float32(theta) ** (-jnp.arange(half, dtype=jnp.float32) / half)
    ang = positions.astype(jnp.float32)[:, None, :, None] * inv
    cos, sin = jnp.cos(ang), jnp.sin(ang)
    xr = x[..., :rot_dim].astype(jnp.float32)
    x1, x2 = xr[..., :half], xr[..., half:]
    rot = jnp.concatenate([x1 * cos - x2 * sin, x1 * sin + x2 * cos], axis=-1).astype(x.dtype)
    if rot_dim == x.shape[-1]:
        return rot
    return jnp.concatenate([rot, x[..., rot_dim:]], axis=-1)


def _moba(q, k, v):
    B, H, S, d = q.shape
    nb = -(-S // MOBA_BLOCK)
    sp = nb * MOBA_BLOCK
    pad = ((0, 0), (0, 0), (0, sp - S), (0, 0))
    q, k, v = jnp.pad(q, pad), jnp.pad(k, pad), jnp.pad(v, pad)
    kb = k.reshape(B, H, nb, MOBA_BLOCK, d)
    vb = v.reshape(B, H, nb, MOBA_BLOCK, d)
    kmean = jnp.mean(kb.astype(jnp.float32), axis=3)
    gate = jnp.einsum('bhsd,bhnd->bhsn', q.astype(jnp.float32), kmean)
    q_blk = jnp.arange(sp) // MOBA_BLOCK
    past = jnp.arange(nb)[None, :] < q_blk[:, None]
    gate = jnp.where(past, gate, NEG)
    kk = min(MOBA_TOPK, nb)
    g_val, g_idx = lax.top_k(gate, kk)
    g_ok = g_val > NEG / 2
    scale = d ** -0.5
    b_i = jnp.arange(B)[:, None, None, None]
    h_i = jnp.arange(H)[None, :, None, None]
    QC = MOBA_QCHUNK

    def chunk(c):
        s0 = c * QC
        qc = lax.dynamic_slice_in_dim(q, s0, QC, axis=2)
        idx = lax.dynamic_slice_in_dim(g_idx, s0, QC, axis=2)
        ok = lax.dynamic_slice_in_dim(g_ok, s0, QC, axis=2)
        k_sel = kb[b_i, h_i, idx]
        v_sel = vb[b_i, h_i, idx]
        own = s0 // MOBA_BLOCK
        k_own = lax.dynamic_index_in_dim(kb, own, axis=2, keepdims=False)
        v_own = lax.dynamic_index_in_dim(vb, own, axis=2, keepdims=False)
        s_sel = jnp.einsum('bhqd,bhqnkd->bhqnk', qc, k_sel).astype(jnp.float32) * scale
        s_sel = jnp.where(ok[..., None], s_sel, NEG).reshape(B, H, QC, kk * MOBA_BLOCK)
        s_own = jnp.einsum('bhqd,bhkd->bhqk', qc, k_own).astype(jnp.float32) * scale
        q_pos = s0 + jnp.arange(QC)
        k_pos = own * MOBA_BLOCK + jnp.arange(MOBA_BLOCK)
        s_own = jnp.where(k_pos[None, :] <= q_pos[:, None], s_own, NEG)
        p = jax.nn.softmax(jnp.concatenate([s_sel, s_own], axis=-1), axis=-1).astype(v.dtype)
        p_sel = p[..., :kk * MOBA_BLOCK].reshape(B, H, QC, kk, MOBA_BLOCK)
        p_own = p[..., kk * MOBA_BLOCK:]
        return (jnp.einsum('bhqnk,bhqnkd->bhqd', p_sel, v_sel)
                + jnp.einsum('bhqk,bhkd->bhqd', p_own, v_own))

    out = lax.map(chunk, jnp.arange(sp // QC))
    out = out.transpose(1, 2, 0, 3, 4).reshape(B, H, sp, d)
    return out[:, :, :S]


def _compress(t, pe, w1, w2):
    B, G, S, d = t.shape
    f = NSA_CMP_LEN // NSA_CMP_STRIDE
    n_sub = S // NSA_CMP_STRIDE
    n_cmp = n_sub - f + 1
    sub = t.reshape(B, G, n_sub, NSA_CMP_STRIDE, d)
    blocks = jnp.concatenate([sub[:, :, i:i + n_cmp] for i in range(f)], axis=3) + pe
    h = jax.nn.silu(blocks.reshape(B, G, n_cmp, NSA_CMP_LEN * d) @ w1)
    return h @ w2


def _nsa(q, k_c, v_c, k_s, v_s, k_w, v_w, gate_logits, positions, pe_k, w1_k, w2_k, pe_v, w1_v, w2_v):
    B, H, S, d = q.shape
    G = NSA_GROUPS
    R = H // G
    f32 = jnp.float32
    scale = d ** -0.5
    pos = jnp.arange(S)
    kc = _compress(k_c, pe_k, w1_k, w2_k)
    vc = _compress(v_c, pe_v, w1_v, w2_v)
    n_cmp = kc.shape[2]
    qg = q.reshape(B, G, R, S, d)
    s_c = jnp.einsum('bgrsd,bgnd->bgrsn', qg, kc).astype(f32) * scale
    c_ok = (jnp.arange(n_cmp) * NSA_CMP_STRIDE + NSA_CMP_LEN - 1)[None, :] <= pos[:, None]
    p_c = jnp.where(c_ok, jax.nn.softmax(jnp.where(c_ok, s_c, NEG), axis=-1), 0.0)
    o_c = jnp.einsum('bgrsn,bgnd->bgrsd', p_c.astype(vc.dtype), vc)
    n_sel = S // NSA_SEL_LEN
    c_start = np.arange(n_cmp) * NSA_CMP_STRIDE
    s_start = np.arange(n_sel) * NSA_SEL_LEN
    cover = (c_start[:, None] < s_start[None, :] + NSA_SEL_LEN) & (c_start[:, None] + NSA_CMP_LEN > s_start[None, :])
    imp = jnp.einsum('bgrsn,nj->bgsj', p_c, jnp.asarray(cover, f32))
    blk = pos // NSA_SEL_LEN
    j = jnp.arange(n_sel)
    forced = (j[None, :] == 0) | (j[None, :] == blk[:, None]) | (j[None, :] == blk[:, None] - 1)
    imp = jnp.where(forced, FORCE, jnp.where(j[None, :] <= blk[:, None], imp, NEG))
    kk = min(NSA_TOPK, n_sel)
    s_val, s_idx = lax.top_k(imp, kk)
    s_ok = s_val > NEG / 2
    q_rot = _rope(q, positions, ROPE_DIM, ROPE_THETA).reshape(B, G, R, S, d)
    k_s = _rope(k_s, positions, ROPE_DIM, ROPE_THETA)
    k_w = _rope(k_w, positions, ROPE_DIM, ROPE_THETA)
    ksb = k_s.reshape(B, G, n_sel, NSA_SEL_LEN, d)
    vsb = v_s.reshape(B, G, n_sel, NSA_SEL_LEN, d)
    b_i = jnp.arange(B)[:, None, None, None]
    g_i = jnp.arange(G)[None, :, None, None]
    QC = NSA_QCHUNK

    def chunk(c):
        s0 = c * QC
        qc = lax.dynamic_slice_in_dim(q_rot, s0, QC, axis=3)
        idx = lax.dynamic_slice_in_dim(s_idx, s0, QC, axis=2)
        ok = lax.dynamic_slice_in_dim(s_ok, s0, QC, axis=2)
        k_sel = ksb[b_i, g_i, idx]
        v_sel = vsb[b_i, g_i, idx]
        sc = jnp.einsum('bgrqd,bgqnkd->bgrqnk', qc, k_sel).astype(f32) * scale
        k_pos = idx[..., None] * NSA_SEL_LEN + jnp.arange(NSA_SEL_LEN)
        q_pos = s0 + jnp.arange(QC)
        valid = ok[..., None] & (k_pos <= q_pos[:, None, None])
        sc = jnp.where(valid[:, :, None], sc, NEG).reshape(B, G, R, QC, kk * NSA_SEL_LEN)
        p = jax.nn.softmax(sc, axis=-1).astype(v_sel.dtype).reshape(B, G, R, QC, kk, NSA_SEL_LEN)
        return jnp.einsum('bgrqnk,bgqnkd->bgrqd', p, v_sel)

    o_s = lax.map(chunk, jnp.arange(S // QC))
    o_s = o_s.transpose(1, 2, 3, 0, 4, 5).reshape(B, G, R, S, d)
    nqb = S // WIN_QBLOCK
    span = NSA_WINDOW + WIN_QBLOCK
    kv_idx = jnp.arange(nqb)[:, None] * WIN_QBLOCK + jnp.arange(span)[None, :]
    padw = ((0, 0), (0, 0), (NSA_WINDOW, 0), (0, 0))
    k_win = jnp.pad(k_w, padw)[:, :, kv_idx]
    v_win = jnp.pad(v_w, padw)[:, :, kv_idx]
    qb = q_rot.reshape(B, G, R, nqb, WIN_QBLOCK, d)
    sw = jnp.einsum('bgrcqd,bgckd->bgrcqk', qb, k_win).astype(f32) * scale
    q_pos = pos.reshape(nqb, WIN_QBLOCK)
    k_pos = kv_idx - NSA_WINDOW
    dist = q_pos[:, :, None] - k_pos[:, None, :]
    w_ok = (dist >= 0) & (dist < NSA_WINDOW) & (k_pos[:, None, :] >= 0)
    p_w = jax.nn.softmax(jnp.where(w_ok, sw, NEG), axis=-1).astype(v_win.dtype)
    o_w = jnp.einsum('bgrcqk,bgckd->bgrcqd', p_w, v_win).reshape(B, G, R, S, d)
    gates = jax.nn.sigmoid(gate_logits.astype(f32)).reshape(B, S, H, 3).transpose(0, 2, 1, 3).reshape(B, G, R, S, 3)
    o = gates[..., 0:1] * o_c + gates[..., 1:2] * o_s + gates[..., 2:3] * o_w
    return o.reshape(B, H, S, d).astype(q.dtype)


def _mixer_ab(h, positions, w_in, pe_k, w1_k, w2_k, pe_v, w1_v, w2_v, w_out):
    mq, mk, mv, nq, nkc, nvc, nks, nvs, nkw, nvw, ng = _split(h @ w_in, AB_SIZES)
    mq = _rope(_heads(mq, MOBA_HEADS), positions, ROPE_DIM, ROPE_THETA)
    mk = _rope(_heads(mk, MOBA_HEADS), positions, ROPE_DIM, ROPE_THETA)
    o_a = _moba(mq, mk, _heads(mv, MOBA_HEADS))
    o_b = _nsa(_heads(nq, NSA_HEADS), _heads(nkc, NSA_GROUPS), _heads(nvc, NSA_GROUPS),
               _heads(nks, NSA_GROUPS), _heads(nvs, NSA_GROUPS), _heads(nkw, NSA_GROUPS),
               _heads(nvw, NSA_GROUPS), ng, positions, pe_k, w1_k, w2_k, pe_v, w1_v, w2_v)
    o = jnp.concatenate([o_a, o_b.astype(o_a.dtype)], axis=1)
    return _merge(o) @ w_out


def _retention(q, k, v):
    B, H, S, dk = q.shape
    dv = v.shape[-1]
    C = RET_CHUNK
    nch = S // C
    log_g = jnp.log(1.0 - jnp.exp2(-5.0 - jnp.arange(H, dtype=jnp.float32)))
    n = jnp.arange(C, dtype=jnp.float32)
    diff = n[:, None] - n[None, :]
    decay = jnp.where(diff >= 0, jnp.exp(jnp.maximum(diff, 0.0) * log_g[:, None, None]), 0.0)
    q_dec = jnp.exp((n + 1.0) * log_g[:, None])[:, :, None]
    k_dec = jnp.exp((C - 1.0 - n) * log_g[:, None])[:, :, None]
    c_dec = jnp.exp(C * log_g)[:, None, None]

    def to_chunks(t):
        return t.reshape(B, H, nch, C, t.shape[-1]).transpose(2, 0, 1, 3, 4)

    def step(state, xs):
        qi, ki, vi = xs
        intra = jnp.einsum('bhnm,bhmv->bhnv', jnp.einsum('bhnd,bhmd->bhnm', qi, ki) * decay, vi)
        cross = jnp.einsum('bhnd,bhdv->bhnv', qi, state) * q_dec
        state = state * c_dec + jnp.einsum('bhmd,bhmv->bhdv', ki * k_dec, vi)
        return state, intra + cross

    state0 = jnp.zeros((B, H, dk, dv), jnp.float32)
    _, out = lax.scan(step, state0, (to_chunks(q), to_chunks(k), to_chunks(v)))
    return out.transpose(1, 2, 0, 3, 4).reshape(B, H, S, dv)


def _mixer_c(h, positions, w_in, gn_gain, w_out):
    f32 = jnp.float32
    q, k, v, g = _split(h @ w_in, C_SIZES)
    q = _rope(_heads(q, RET_HEADS), positions, RET_DK, RET_THETA).astype(f32)
    k = _rope(_heads(k, RET_HEADS), positions, RET_DK, RET_THETA).astype(f32) * RET_DK ** -0.5
    v = _heads(v, RET_HEADS).astype(f32)
    y = _retention(q, k, v)
    mu = jnp.mean(y, axis=-1, keepdims=True)
    var = jnp.mean(jnp.square(y - mu), axis=-1, keepdims=True)
    y = (y - mu) * lax.rsqrt(var + RMS_EPS) * gn_gain.astype(f32)[None, :, None, :]
    y = _merge(y).astype(h.dtype)
    return (jax.nn.silu(g) * y) @ w_out


def _cross_attn(h, mem_n, w_q, w_kv, w_o):
    q = _heads(h @ w_q, X_HEADS)
    k, v = _split(mem_n @ w_kv, (X_HEADS * HEAD_DIM, X_HEADS * HEAD_DIM))
    k, v = _heads(k, X_HEADS), _heads(v, X_HEADS)
    s = jnp.einsum('bhsd,bhmd->bhsm', q, k).astype(jnp.float32) * HEAD_DIM ** -0.5
    p = jax.nn.softmax(s, axis=-1).astype(v.dtype)
    return _merge(jnp.einsum('bhsm,bhmd->bhsd', p, v)) @ w_o


def _conv_ffn(h, w_up, conv_w, conv_b, w_down):
    S = h.shape[1]
    u = h @ w_up
    up = jnp.pad(u, ((0, 0), (CONV_WIDTH - 1, 0), (0, 0)))
    c = conv_b + sum(conv_w[i] * up[:, i:i + S] for i in range(CONV_WIDTH))
    gate, val = _split(c, (D_FF, D_FF))
    return (jax.nn.silu(gate) * val) @ w_down


def setup_inputs(seed: int = 0) -> dict:
    key = jax.random.key(seed)
    ks = jax.random.split(key, 26)
    f32 = jnp.float32

    def nrm(k, shape, scale):
        return jax.random.normal(k, shape, f32) * scale

    def gain(k, shape):
        return 1.0 + nrm(k, shape, 0.02)

    D = D_MODEL
    positions = (jax.random.randint(ks[2], (BATCH, 1), 0, 1024, dtype=jnp.int32)
                 + jnp.arange(SEQ, dtype=jnp.int32)[None, :])
    return {
        'x': nrm(ks[0], (BATCH, SEQ, D), 1.0),
        'mem': nrm(ks[1], (BATCH, N_MEM, D), 1.0),
        'positions': positions,
        'norm_mix': gain(ks[3], (DEPTH, D)),
        'norm_cross': gain(ks[4], (DEPTH, D)),
        'norm_ffn': gain(ks[5], (DEPTH, D)),
        'norm_mem': gain(ks[6], (D,)),
        'norm_final': gain(ks[7], (D,)),
        'w_in_ab': nrm(ks[8], (N_EVEN, D, AB_COLS), D ** -0.5),
        'cmp_pe_k': nrm(ks[9], (N_EVEN, NSA_CMP_LEN, HEAD_DIM), 0.1),
        'cmp_w1_k': nrm(ks[10], (N_EVEN, NSA_CMP_LEN * HEAD_DIM, HEAD_DIM), (NSA_CMP_LEN * HEAD_DIM) ** -0.5),
        'cmp_w2_k': nrm(ks[11], (N_EVEN, HEAD_DIM, HEAD_DIM), HEAD_DIM ** -0.5),
        'cmp_pe_v': nrm(ks[12], (N_EVEN, NSA_CMP_LEN, HEAD_DIM), 0.1),
        'cmp_w1_v': nrm(ks[13], (N_EVEN, NSA_CMP_LEN * HEAD_DIM, HEAD_DIM), (NSA_CMP_LEN * HEAD_DIM) ** -0.5),
        'cmp_w2_v': nrm(ks[14], (N_EVEN, HEAD_DIM, HEAD_DIM), HEAD_DIM ** -0.5),
        'w_out_ab': nrm(ks[15], (N_EVEN, D, D), D ** -0.5),
        'w_in_c': nrm(ks[16], (N_ODD, D, C_COLS), D ** -0.5),
        'ret_gn': gain(ks[17], (N_ODD, RET_HEADS, RET_DV)),
        'w_out_c': nrm(ks[18], (N_ODD, RET_HEADS * RET_DV, D), (RET_HEADS * RET_DV) ** -0.5),
        'w_q_x': nrm(ks[19], (DEPTH, D, X_HEADS * HEAD_DIM), D ** -0.5),
        'w_kv_x': nrm(ks[20], (DEPTH, D, 2 * X_HEADS * HEAD_DIM), D ** -0.5),
        'w_o_x': nrm(ks[21], (DEPTH, X_HEADS * HEAD_DIM, D), (X_HEADS * HEAD_DIM) ** -0.5),
        'w_up': nrm(ks[22], (DEPTH, D, 2 * D_FF), D ** -0.5),
        'conv_w': nrm(ks[23], (DEPTH, CONV_WIDTH, 2 * D_FF), CONV_WIDTH ** -0.5),
        'conv_b': nrm(ks[24], (DEPTH, 2 * D_FF), 0.02),
        'w_down': nrm(ks[25], (DEPTH, D_FF, D), D_FF ** -0.5),
    }


def reference(x, mem, positions, norm_mix, norm_cross, norm_ffn, norm_mem, norm_final,
              w_in_ab, cmp_pe_k, cmp_w1_k, cmp_w2_k, cmp_pe_v, cmp_w1_v, cmp_w2_v, w_out_ab,
              w_in_c, ret_gn, w_out_c, w_q_x, w_kv_x, w_o_x, w_up, conv_w, conv_b, w_down):
    mem_n = _rmsnorm(mem, norm_mem)
    h = x
    for l in range(DEPTH):
        hn = _rmsnorm(h, norm_mix[l])
        if l % 2 == 0:
            e = l // 2
            h = h + _mixer_ab(hn, positions, w_in_ab[e], cmp_pe_k[e], cmp_w1_k[e], cmp_w2_k[e],
                              cmp_pe_v[e], cmp_w1_v[e], cmp_w2_v[e], w_out_ab[e])
        else:
            o = l // 2
            h = h + _mixer_c(hn, positions, w_in_c[o], ret_gn[o], w_out_c[o])
        h = h + _cross_attn(_rmsnorm(h, norm_cross[l]), mem_n, w_q_x[l], w_kv_x[l], w_o_x[l])
        h = h + _conv_ffn(_rmsnorm(h, norm_ffn[l]), w_up[l], conv_w[l], conv_b[l], w_down[l])
    return _rmsnorm(h, norm_final)
```

```python
import functools

import numpy as np
import jax
import jax.numpy as jnp
from jax import lax
from jax.experimental import pallas as pl
from jax.experimental.pallas import tpu as pltpu

F32 = jnp.float32
BF16 = jnp.bfloat16

D_MODEL = 2048
DEPTH = 4
HEAD_DIM = 128
ROPE_THETA = 500000.0
ROPE_DIM = HEAD_DIM // 4
X_HEADS = 4
MOBA_HEADS = 8
MOBA_BLOCK = 256
MOBA_TOPK = 3
NSA_HEADS = 8
NSA_GROUPS = 2
NSA_REP = NSA_HEADS // NSA_GROUPS
NSA_CMP_LEN = 32
NSA_CMP_STRIDE = 16
NSA_SEL_LEN = 64
NSA_TOPK = 16
NSA_WINDOW = 512
RET_HEADS = 8
RET_DK = 256
RET_DV = 512
RET_THETA = 10000.0
D_FF = 5632
CONV_WIDTH = 3
RMS_EPS = 1e-6
NEG = -1e30
FORCE = 1e9

LANE = 128
SUBLANE = 8
VMEM_LIMIT = 56 * 2 ** 20

AB_SIZES = (MOBA_HEADS * HEAD_DIM,) * 3 + (NSA_HEADS * HEAD_DIM,) + (NSA_GROUPS * HEAD_DIM,) * 6 + (NSA_HEADS * 3,)
AB_COLS = sum(AB_SIZES)
AB_BLOCKS = 48
BLK_MQ, BLK_MK, BLK_MV, BLK_NQ = 0, 8, 16, 24
BLK_NKC, BLK_NVC, BLK_NKS, BLK_NVS, BLK_NKW, BLK_NVW, BLK_NG = 32, 34, 36, 38, 40, 42, 44

ATT_TQ = 256
ATT_TK = 256
RET_CHUNK = 128

_NT = (((1,), (1,)), ((), ()))
_TN = (((0,), (0,)), ((), ()))


def _params(*sem):
    return pltpu.CompilerParams(dimension_semantics=sem, vmem_limit_bytes=VMEM_LIMIT)


def _dot(a, b):
    return jnp.dot(a, b, preferred_element_type=F32)


def _dot_nt(a, b):
    return lax.dot_general(a, b, _NT, preferred_element_type=F32)


def _sigmoid(x):
    return 1.0 / (1.0 + jnp.exp(-x))


def _rms_rows(x, g):
    ms = jnp.mean(x * x, axis=-1, keepdims=True)
    return x * lax.rsqrt(ms + RMS_EPS) * g


def _tables_kernel(pos_ref, inv_rope_ref, inv_ret_ref, c_ref, s_ref, cr_ref, sr_ref):
    pos = pos_ref[...]
    lane = lax.broadcasted_iota(jnp.int32, pos.shape, 1)
    rot = lane < ROPE_DIM
    ang = pos * inv_rope_ref[...]
    c_ref[...] = jnp.where(rot, jnp.cos(ang), 1.0)
    s_ref[...] = jnp.where(rot, jnp.sin(ang), 0.0)
    ang_r = pos * inv_ret_ref[...]
    cr_ref[...] = jnp.cos(ang_r)
    sr_ref[...] = jnp.sin(ang_r)


def _rope_tables(positions):
    m = positions.size
    posb = jnp.broadcast_to(positions.reshape(m, 1).astype(F32), (m, LANE))
    half = ROPE_DIM // 2
    inv = jnp.float32(ROPE_THETA) ** (-jnp.arange(half, dtype=F32) / half)
    inv_rope = jnp.concatenate([inv, inv, jnp.zeros((LANE - ROPE_DIM,), F32)]).reshape(1, LANE)
    half_r = RET_DK // 2
    inv_ret = (jnp.float32(RET_THETA) ** (-jnp.arange(half_r, dtype=F32) / half_r)).reshape(1, LANE)
    tm = 1024 if m % 1024 == 0 else m
    row = pl.BlockSpec((tm, LANE), lambda i: (i, 0))
    vec = pl.BlockSpec((1, LANE), lambda i: (0, 0))
    sds = jax.ShapeDtypeStruct((m, LANE), F32)
    return pl.pallas_call(
        _tables_kernel, grid=(m // tm,), in_specs=[row, vec, vec], out_specs=[row] * 4,
        out_shape=[sds] * 4, compiler_params=_params("parallel"), name="rope_tables",
    )(posb, inv_rope, inv_ret)


def _rot_matrix():
    half = ROPE_DIM // 2
    p = np.zeros((HEAD_DIM, HEAD_DIM), np.float32)
    for l in range(half):
        p[l + half, l] = -1.0
        p[l, l + half] = 1.0
    return jnp.asarray(p, BF16)


def _rope_partial(x_bf16, c, s, p_ref):
    return x_bf16.astype(F32) * c + _dot(x_bf16, p_ref[...]) * s


def _norm_matmul_kernel(x_ref, g_ref, w_ref, o_ref, xn_ref, *, head_major):
    @pl.when(pl.program_id(1) == 0)
    def _():
        xn_ref[...] = _rms_rows(x_ref[...], g_ref[...]).astype(BF16)

    acc = _dot(xn_ref[...], w_ref[...])
    if head_major:
        for c in range(o_ref.shape[0]):
            o_ref[c] = acc[:, c * LANE:(c + 1) * LANE].astype(o_ref.dtype)
    else:
        o_ref[...] = acc.astype(o_ref.dtype)


def _norm_matmul(x, g, w, *, tm, tn, head_major=False):
    m, k = x.shape
    n = w.shape[1]
    assert m % tm == 0 and n % tn == 0 and tn % LANE == 0
    if head_major:
        out_shape = jax.ShapeDtypeStruct((n // LANE, m, LANE), BF16)
        out_spec = pl.BlockSpec((tn // LANE, tm, LANE), lambda i, j: (j, i, 0))
    else:
        out_shape = jax.ShapeDtypeStruct((m, n), BF16)
        out_spec = pl.BlockSpec((tm, tn), lambda i, j: (i, j))
    return pl.pallas_call(
        functools.partial(_norm_matmul_kernel, head_major=head_major),
        grid=(m // tm, n // tn),
        in_specs=[pl.BlockSpec((tm, k), lambda i, j: (i, 0)),
                  pl.BlockSpec((1, k), lambda i, j: (0, 0)),
                  pl.BlockSpec((k, tn), lambda i, j: (0, j))],
        out_specs=out_spec, out_shape=out_shape,
        scratch_shapes=[pltpu.VMEM((tm, k), BF16)],
        compiler_params=_params("parallel", "arbitrary"), name="norm_matmul",
    )(x, g.reshape(1, k), w)


def _matmul_res_kernel(*refs):
    r_ref, o_ref = refs[-2], refs[-1]
    acc = r_ref[...]
    for x_ref, w_ref in zip(refs[:-2:2], refs[1:-2:2]):
        acc = acc + _dot(x_ref[...], w_ref[...])
    o_ref[...] = acc


def _matmul_res(xs, ws, res, *, tm, tn):
    m, n = res.shape
    assert m % tm == 0 and n % tn == 0
    in_specs, args = [], []
    for x, w in zip(xs, ws):
        k = x.shape[1]
        in_specs += [pl.BlockSpec((tm, k), lambda i, j: (i, 0)), pl.BlockSpec((k, tn), lambda i, j: (0, j))]
        args += [x, w]
    return pl.pallas_call(
        _matmul_res_kernel, grid=(m // tm, n // tn),
        in_specs=in_specs + [pl.BlockSpec((tm, tn), lambda i, j: (i, j))],
        out_specs=pl.BlockSpec((tm, tn), lambda i, j: (i, j)),
        out_shape=jax.ShapeDtypeStruct((m, n), F32),
        compiler_params=_params("parallel", "arbitrary"), name="matmul_res",
    )(*args, res)


def _moba_kernel(q_ref, k_ref, v_ref, c_ref, s_ref, p_ref, o_ref, krot_ref, kmean_ref, *, nb):
    qi = pl.program_id(2)
    blk = MOBA_BLOCK
    scale = HEAD_DIM ** -0.5

    @pl.when(qi == 0)
    def _():
        kmean_ref[...] = jnp.zeros_like(kmean_ref)
        for j in range(nb):
            rows = slice(j * blk, (j + 1) * blk)
            kr = _rope_partial(k_ref[rows, :], c_ref[rows, :], s_ref[rows, :], p_ref)
            krot_ref[rows, :] = kr.astype(BF16)
            kmean_ref[j:j + 1, :] = jnp.mean(kr, axis=0, keepdims=True)

    r0 = pl.multiple_of(qi * blk, blk)
    q = _rope_partial(q_ref[...], c_ref[pl.ds(r0, blk), :], s_ref[pl.ds(r0, blk), :], p_ref)
    qb = q.astype(BF16)

    gate = lax.dot_general(q, kmean_ref[...], _NT, precision=lax.Precision.HIGHEST,
                           preferred_element_type=F32)
    lane = lax.broadcasted_iota(jnp.int32, (blk, LANE), 1)
    g = jnp.where(lane < qi, gate, NEG)
    sel = jnp.zeros((blk, LANE), F32)
    for _ in range(min(MOBA_TOPK, nb)):
        mx = jnp.max(g, axis=-1, keepdims=True)
        idx = jnp.min(jnp.where(g == mx, lane, LANE), axis=-1, keepdims=True)
        hit = lane == idx
        sel = jnp.where(hit, jnp.where(mx > NEG / 2, 1.0, 0.0), sel)
        g = jnp.where(hit, -jnp.inf, g)

    s = _dot_nt(qb, krot_ref[pl.ds(r0, blk), :]) * scale
    row = lax.broadcasted_iota(jnp.int32, (blk, blk), 0)
    col = lax.broadcasted_iota(jnp.int32, (blk, blk), 1)
    s = jnp.where(col <= row, s, NEG)
    m = jnp.max(s, axis=-1, keepdims=True)
    p = jnp.exp(s - m)
    l = jnp.sum(p, axis=-1, keepdims=True)
    acc = _dot(p.astype(BF16), v_ref[pl.ds(r0, blk), :])

    def body(j, carry):
        m, l, acc = carry
        off = pl.multiple_of(j * blk, blk)
        s = _dot_nt(qb, krot_ref[pl.ds(off, blk), :]) * scale
        chosen = jnp.sum(jnp.where(lane == j, sel, 0.0), axis=-1, keepdims=True)
        s = jnp.where(chosen > 0.5, s, NEG)
        m_new = jnp.maximum(m, jnp.max(s, axis=-1, keepdims=True))
        a = jnp.exp(m - m_new)
        p = jnp.exp(s - m_new)
        l = a * l + jnp.sum(p, axis=-1, keepdims=True)
        acc = a * acc + _dot(p.astype(BF16), v_ref[pl.ds(off, blk), :])
        return m_new, l, acc

    m, l, acc = lax.fori_loop(0, qi, body, (m, l, acc))
    o_ref[...] = (acc / l).astype(o_ref.dtype)


def _moba(proj, cos, sin, rot, batch, seq):
    nb = seq // MOBA_BLOCK
    assert seq % MOBA_BLOCK == 0 and nb <= LANE
    m = batch * seq
    head = lambda off: pl.BlockSpec((None, seq, HEAD_DIM), lambda b, h, i: (off + h, b, 0))
    table = pl.BlockSpec((seq, LANE), lambda b, h, i: (b, 0))
    return pl.pallas_call(
        functools.partial(_moba_kernel, nb=nb),
        grid=(batch, MOBA_HEADS, nb),
        in_specs=[pl.BlockSpec((None, MOBA_BLOCK, HEAD_DIM), lambda b, h, i: (BLK_MQ + h, b * nb + i, 0)),
                  head(BLK_MK), head(BLK_MV), table, table,
                  pl.BlockSpec((HEAD_DIM, HEAD_DIM), lambda b, h, i: (0, 0))],
        out_specs=pl.BlockSpec((MOBA_BLOCK, HEAD_DIM), lambda b, h, i: (b * nb + i, h)),
        out_shape=jax.ShapeDtypeStruct((m, MOBA_HEADS * HEAD_DIM), BF16),
        scratch_shapes=[pltpu.VMEM((seq, HEAD_DIM), BF16), pltpu.VMEM((LANE, HEAD_DIM), F32)],
        compiler_params=_params("parallel", "parallel", "arbitrary"), name="moba",
    )(proj, proj, proj, cos, sin, rot)


def _compress_kernel(x_ref, pe_ref, w1_ref, w2_ref, o_ref):
    half = NSA_CMP_STRIDE * HEAD_DIM
    x = x_ref[...]
    n_sub = x.shape[0]
    first = _dot(x, w1_ref[:half, :])
    second = _dot(x, w1_ref[half:, :])
    bias = _dot(pe_ref[...], w1_ref[...])[0:1, :]
    nxt = pltpu.roll(second, n_sub - 1, 0)
    hid = first + nxt + bias
    hid = hid * _sigmoid(hid)
    o_ref[...] = _dot(hid.astype(BF16), w2_ref[...]).astype(o_ref.dtype)


def _compress(proj, pe, w1, w2, batch, seq):
    n_sub = seq // NSA_CMP_STRIDE
    flat = NSA_CMP_STRIDE * HEAD_DIM
    x = proj.reshape(AB_BLOCKS, batch * n_sub, flat)
    pe_flat = jnp.zeros((2, SUBLANE * 2, NSA_CMP_LEN * HEAD_DIM), BF16).at[:, 0].set(
        pe.reshape(2, NSA_CMP_LEN * HEAD_DIM).astype(BF16))
    return pl.pallas_call(
        _compress_kernel, grid=(2, batch, NSA_GROUPS),
        in_specs=[pl.BlockSpec((None, n_sub, flat), lambda t, b, g: (BLK_NKC + 2 * t + g, b, 0)),
                  pl.BlockSpec((None, SUBLANE * 2, 2 * flat), lambda t, b, g: (t, 0, 0)),
                  pl.BlockSpec((None, 2 * flat, HEAD_DIM), lambda t, b, g: (t, 0, 0)),
                  pl.BlockSpec((None, HEAD_DIM, HEAD_DIM), lambda t, b, g: (t, 0, 0))],
        out_specs=pl.BlockSpec((None, None, None, n_sub, HEAD_DIM), lambda t, b, g: (t, b, g, 0, 0)),
        out_shape=jax.ShapeDtypeStruct((2, batch, NSA_GROUPS, n_sub, HEAD_DIM), BF16),
        compiler_params=_params("parallel", "parallel", "parallel"), name="nsa_compress",
    )(x, pe_flat, w1.astype(BF16), w2.astype(BF16))


def _nsa_cmp_kernel(q_ref, kc_ref, vc_ref, cov_ref, oc_ref, sel_ref, *, n_sel):
    qi = pl.program_id(2)
    tq = ATT_TQ
    n_cmp = kc_ref.shape[0]
    scale = HEAD_DIM ** -0.5
    q_pos = qi * tq + lax.broadcasted_iota(jnp.int32, (tq, n_cmp), 0)
    n_idx = lax.broadcasted_iota(jnp.int32, (tq, n_cmp), 1)
    ok = n_idx * NSA_CMP_STRIDE + (NSA_CMP_LEN - 1) <= q_pos
    kc = kc_ref[...]
    vc = vc_ref[...]
    p_sum = jnp.zeros((tq, n_cmp), F32)
    for r in range(NSA_REP):
        s = jnp.where(ok, _dot_nt(q_ref[r], kc) * scale, NEG)
        mx = jnp.max(s, axis=-1, keepdims=True)
        e = jnp.where(ok, jnp.exp(s - mx), 0.0)
        l = jnp.sum(e, axis=-1, keepdims=True)
        p = e * jnp.where(l > 0.0, 1.0 / l, 0.0)
        oc_ref[r] = _dot(p.astype(BF16), vc).astype(oc_ref.dtype)
        p_sum = p_sum + p

    imp = jnp.dot(p_sum, cov_ref[...], precision=lax.Precision.HIGHEST, preferred_element_type=F32)
    lane = lax.broadcasted_iota(jnp.int32, (tq, LANE), 1)
    blk = (qi * tq + lax.broadcasted_iota(jnp.int32, (tq, LANE), 0)) // NSA_SEL_LEN
    forced = (lane == 0) | (lane == blk) | (lane == blk - 1)
    x = jnp.where(forced, FORCE, jnp.where(lane <= blk, imp, NEG))
    x = jnp.where(lane < n_sel, x, -jnp.inf)
    rank = jnp.zeros((tq, LANE), F32)
    for i in range(n_sel):
        xi = x[:, i:i + 1]
        rank = rank + jnp.where(lane > i, jnp.where(xi >= x, 1.0, 0.0), jnp.where(xi > x, 1.0, 0.0))
    chosen = jnp.where(rank < float(min(NSA_TOPK, n_sel)), jnp.where(x > NEG / 2, 1.0, 0.0), 0.0)
    sel_ref[...] = chosen.astype(sel_ref.dtype)


def _cover_matrix(n_cmp_pad, n_cmp, n_sel):
    c_start = np.arange(n_cmp_pad) * NSA_CMP_STRIDE
    s_start = np.arange(LANE) * NSA_SEL_LEN
    cover = (c_start[:, None] < s_start[None, :] + NSA_SEL_LEN) & (c_start[:, None] + NSA_CMP_LEN > s_start[None, :])
    cover &= (np.arange(n_cmp_pad) < n_cmp)[:, None] & (np.arange(LANE) < n_sel)[None, :]
    return jnp.asarray(cover, F32)


def _nsa_cmp(proj, kvc, batch, seq):
    nq = seq // ATT_TQ
    n_sub = seq // NSA_CMP_STRIDE
    n_sel = seq // NSA_SEL_LEN
    assert n_sel <= LANE
    m = batch * seq
    cover = _cover_matrix(n_sub, n_sub - NSA_CMP_LEN // NSA_CMP_STRIDE + 1, n_sel)
    cmp_spec = lambda t: pl.BlockSpec((None, None, None, n_sub, HEAD_DIM), lambda b, g, i: (t, b, g, 0, 0))
    return pl.pallas_call(
        functools.partial(_nsa_cmp_kernel, n_sel=n_sel),
        grid=(batch, NSA_GROUPS, nq),
        in_specs=[pl.BlockSpec((NSA_REP, ATT_TQ, HEAD_DIM), lambda b, g, i: (BLK_NQ // NSA_REP + g, b * nq + i, 0)),
                  cmp_spec(0), cmp_spec(1),
                  pl.BlockSpec((n_sub, LANE), lambda b, g, i: (0, 0))],
        out_specs=[pl.BlockSpec((NSA_REP, ATT_TQ, HEAD_DIM), lambda b, g, i: (g, b * nq + i, 0)),
                   pl.BlockSpec((ATT_TQ, LANE), lambda b, g, i: (b * nq + i, g))],
        out_shape=[jax.ShapeDtypeStruct((NSA_HEADS, m, HEAD_DIM), BF16),
                   jax.ShapeDtypeStruct((m, NSA_GROUPS * LANE), BF16)],
        compiler_params=_params("parallel", "parallel", "parallel"), name="nsa_cmp",
    )(proj, kvc, kvc, cover)


def _nsa_sw_kernel(q_ref, ks_ref, vs_ref, kw_ref, vw_ref, sel_ref, oc_ref, gl_ref, c_ref, s_ref, p_ref,
                   o_ref, ksr_ref, kwr_ref, qr_ref, m_ref, l_ref, acc_ref, *, seq):
    grp = pl.program_id(1)
    qi = pl.program_id(2)
    tq, tk = ATT_TQ, ATT_TK
    rep = NSA_REP
    scale = HEAD_DIM ** -0.5
    sel_per_tile = tk // NSA_SEL_LEN

    @pl.when(qi == 0)
    def _():
        for j in range(seq // tk):
            rows = slice(j * tk, (j + 1) * tk)
            c, s = c_ref[rows, :], s_ref[rows, :]
            ksr_ref[rows, :] = _rope_partial(ks_ref[rows, :], c, s, p_ref).astype(BF16)
            kwr_ref[rows, :] = _rope_partial(kw_ref[rows, :], c, s, p_ref).astype(BF16)

    r0 = pl.multiple_of(qi * tq, tq)
    cq = c_ref[pl.ds(r0, tq), :]
    sq = s_ref[pl.ds(r0, tq), :]
    for r in range(rep):
        qr_ref[r] = _rope_partial(q_ref[r], cq, sq, p_ref).astype(BF16)

    row = lax.broadcasted_iota(jnp.int32, (tq, tk), 0)
    col = lax.broadcasted_iota(jnp.int32, (tq, tk), 1)
    sel = sel_ref[...]

    def expand_sel(jt):
        blk_row = lax.broadcasted_iota(jnp.int32, (LANE, tk), 0)
        blk_col = jt * sel_per_tile + lax.broadcasted_iota(jnp.int32, (LANE, tk), 1) // NSA_SEL_LEN
        e = jnp.where(blk_row == blk_col, 1.0, 0.0).astype(BF16)
        return _dot(sel, e) > 0.5

    def attend(branch, k_tile, v_tile, valid, first):
        for r in range(rep):
            idx = branch * rep + r
            s = _dot_nt(qr_ref[r], k_tile) * scale
            if valid is not None:
                s = jnp.where(valid, s, NEG)
            if first:
                m_new = jnp.max(s, axis=-1, keepdims=True)
                p = jnp.exp(s - m_new)
                l_ref[idx] = jnp.sum(p, axis=-1, keepdims=True)
                acc_ref[idx] = _dot(p.astype(BF16), v_tile)
            else:
                m_old = m_ref[idx]
                m_new = jnp.maximum(m_old, jnp.max(s, axis=-1, keepdims=True))
                a = jnp.exp(m_old - m_new)
                p = jnp.exp(s - m_new)
                l_ref[idx] = a * l_ref[idx] + jnp.sum(p, axis=-1, keepdims=True)
                acc_ref[idx] = a * acc_ref[idx] + _dot(p.astype(BF16), v_tile)
            m_ref[idx] = m_new

    causal = col <= row
    attend(0, ksr_ref[pl.ds(r0, tk), :], vs_ref[pl.ds(r0, tk), :], causal & expand_sel(qi), True)
    attend(1, kwr_ref[pl.ds(r0, tk), :], vw_ref[pl.ds(r0, tk), :], causal, True)

    def sel_body(jt, carry):
        off = pl.multiple_of(jt * tk, tk)
        attend(0, ksr_ref[pl.ds(off, tk), :], vs_ref[pl.ds(off, tk), :], expand_sel(jt), False)
        return carry

    lax.fori_loop(0, qi, sel_body, 0)

    for d in range(1, NSA_WINDOW // tk + 1):
        @pl.when(qi >= d)
        def _(d=d):
            off = pl.multiple_of((qi - d) * tk, tk)
            in_win = (row - col) + d * tk < NSA_WINDOW
            attend(1, kwr_ref[pl.ds(off, tk), :], vw_ref[pl.ds(off, tk), :], in_win, False)

    gates = _sigmoid(gl_ref[...].astype(F32))
    lane = lax.broadcasted_iota(jnp.int32, (tq, LANE), 1)

    def gate_col(c):
        return jnp.sum(jnp.where(lane == c, gates, 0.0), axis=-1, keepdims=True)

    for r in range(rep):
        base = (grp * rep + r) * 3
        o_s = acc_ref[r] / l_ref[r]
        o_w = acc_ref[rep + r] / l_ref[rep + r]
        o = gate_col(base) * oc_ref[r].astype(F32) + gate_col(base + 1) * o_s + gate_col(base + 2) * o_w
        o_ref[:, r * HEAD_DIM:(r + 1) * HEAD_DIM] = o.astype(o_ref.dtype)


def _nsa_sw(proj, sel, o_cmp, cos, sin, rot, batch, seq):
    nq = seq // ATT_TQ
    assert seq % ATT_TQ == 0 and NSA_WINDOW % ATT_TK == 0 and ATT_TQ == ATT_TK
    m = batch * seq
    rep = NSA_REP
    head = lambda off: pl.BlockSpec((None, seq, HEAD_DIM), lambda b, g, i: (off + g, b, 0))
    table = pl.BlockSpec((seq, LANE), lambda b, g, i: (b, 0))
    q_like = lambda off: pl.BlockSpec((rep, ATT_TQ, HEAD_DIM), lambda b, g, i: (off + g, b * nq + i, 0))
    return pl.pallas_call(
        functools.partial(_nsa_sw_kernel, seq=seq),
        grid=(batch, NSA_GROUPS, nq),
        in_specs=[q_like(BLK_NQ // rep), head(BLK_NKS), head(BLK_NVS), head(BLK_NKW), head(BLK_NVW),
                  pl.BlockSpec((ATT_TQ, LANE), lambda b, g, i: (b * nq + i, g)),
                  q_like(0),
                  pl.BlockSpec((None, ATT_TQ, LANE), lambda b, g, i: (BLK_NG, b * nq + i, 0)),
                  table, table,
                  pl.BlockSpec((HEAD_DIM, HEAD_DIM), lambda b, g, i: (0, 0))],
        out_specs=pl.BlockSpec((ATT_TQ, rep * HEAD_DIM), lambda b, g, i: (b * nq + i, g)),
        out_shape=jax.ShapeDtypeStruct((m, NSA_HEADS * HEAD_DIM), BF16),
        scratch_shapes=[pltpu.VMEM((seq, HEAD_DIM), BF16), pltpu.VMEM((seq, HEAD_DIM), BF16),
                        pltpu.VMEM((rep, ATT_TQ, HEAD_DIM), BF16),
                        pltpu.VMEM((2 * rep, ATT_TQ, 1), F32), pltpu.VMEM((2 * rep, ATT_TQ, 1), F32),
                        pltpu.VMEM((2 * rep, ATT_TQ, HEAD_DIM), F32)],
        compiler_params=_params("parallel", "parallel", "arbitrary"), name="nsa_sel_win",
    )(proj, proj, proj, proj, proj, sel, o_cmp, proj, cos, sin, rot)


def _rope_full(x_bf16, c, s):
    half = x_bf16.shape[-1] // 2
    x = x_bf16.astype(F32)
    x1, x2 = x[:, :half], x[:, half:]
    return jnp.concatenate([x1 * c - x2 * s, x1 * s + x2 * c], axis=-1)


def _retention_kernel(q_ref, k_ref, v_ref, g_ref, c_ref, s_ref, lg_ref, gn_ref, o_ref, state_ref):
    ci = pl.program_id(2)
    ch = q_ref.shape[0]

    @pl.when(ci == 0)
    def _():
        state_ref[...] = jnp.zeros_like(state_ref)

    log_g = lg_ref[...][:, 0:1]
    c, s = c_ref[...], s_ref[...]
    q = _rope_full(q_ref[...], c, s)
    k = _rope_full(k_ref[...], c, s) * (RET_DK ** -0.5)
    v = v_ref[...]

    n_row = lax.broadcasted_iota(jnp.int32, (ch, ch), 0)
    n_col = lax.broadcasted_iota(jnp.int32, (ch, ch), 1)
    diff = (n_row - n_col).astype(F32)
    decay = jnp.where(diff >= 0.0, jnp.exp(jnp.maximum(diff, 0.0) * log_g), 0.0)
    n_vec = lax.broadcasted_iota(jnp.int32, (ch, 1), 0).astype(F32)
    q_dec = jnp.exp((n_vec + 1.0) * log_g)
    k_dec = jnp.exp((ch - 1.0 - n_vec) * log_g)
    c_dec = jnp.exp(ch * log_g)

    qb = q.astype(BF16)
    state = state_ref[...]
    scores = _dot_nt(qb, k.astype(BF16)) * decay
    y = _dot(scores.astype(BF16), v) + _dot(qb, state.astype(BF16)) * q_dec
    state_ref[...] = state * c_dec + lax.dot_general((k * k_dec).astype(BF16), v, _TN,
                                                     preferred_element_type=F32)

    mu = jnp.mean(y, axis=-1, keepdims=True)
    yc = y - mu
    var = jnp.mean(yc * yc, axis=-1, keepdims=True)
    yn = yc * lax.rsqrt(var + RMS_EPS) * gn_ref[...]
    gate = g_ref[...].astype(F32)
    o_ref[...] = (gate * _sigmoid(gate) * yn).astype(o_ref.dtype)


def _retention(proj, cos_r, sin_r, gn, batch, seq):
    ch = RET_CHUNK
    nch = seq // ch
    assert seq % ch == 0
    m = batch * seq
    log_g = jnp.log(1.0 - jnp.exp2(-5.0 - jnp.arange(RET_HEADS, dtype=F32)))
    log_g = jnp.broadcast_to(log_g[:, None, None], (RET_HEADS, 1, LANE))
    kq = RET_HEADS * RET_DK // RET_DK
    v_off = 2 * RET_HEADS * RET_DK // RET_DV
    g_off = v_off + RET_HEADS
    rows = lambda b, h, c: b * nch + c
    return pl.pallas_call(
        _retention_kernel, grid=(batch, RET_HEADS, nch),
        in_specs=[pl.BlockSpec((ch, RET_DK), lambda b, h, c: (rows(b, h, c), h)),
                  pl.BlockSpec((ch, RET_DK), lambda b, h, c: (rows(b, h, c), kq + h)),
                  pl.BlockSpec((ch, RET_DV), lambda b, h, c: (rows(b, h, c), v_off + h)),
                  pl.BlockSpec((ch, RET_DV), lambda b, h, c: (rows(b, h, c), g_off + h)),
                  pl.BlockSpec((ch, LANE), lambda b, h, c: (rows(b, h, c), 0)),
                  pl.BlockSpec((ch, LANE), lambda b, h, c: (rows(b, h, c), 0)),
                  pl.BlockSpec((None, 1, LANE), lambda b, h, c: (h, 0, 0)),
                  pl.BlockSpec((None, 1, RET_DV), lambda b, h, c: (h, 0, 0))],
        out_specs=pl.BlockSpec((ch, RET_DV), lambda b, h, c: (rows(b, h, c), h)),
        out_shape=jax.ShapeDtypeStruct((m, RET_HEADS * RET_DV), BF16),
        scratch_shapes=[pltpu.VMEM((RET_DK, RET_DV), F32)],
        compiler_params=_params("parallel", "parallel", "arbitrary"), name="retention",
    )(proj, proj, proj, proj, cos_r, sin_r, log_g, gn.reshape(RET_HEADS, 1, RET_DV).astype(F32))


def _cross_kernel(h_ref, g_ref, wq_ref, kv_ref, wo_ref, o_ref):
    h = h_ref[...]
    hn = _rms_rows(h, g_ref[...]).astype(BF16)
    q = _dot(hn, wq_ref[...]).astype(BF16)
    scale = HEAD_DIM ** -0.5
    kv_cols = X_HEADS * HEAD_DIM
    outs = []
    for hd in range(X_HEADS):
        cols = slice(hd * HEAD_DIM, (hd + 1) * HEAD_DIM)
        s = _dot_nt(q[:, cols], kv_ref[:, cols]) * scale
        mx = jnp.max(s, axis=-1, keepdims=True)
        p = jnp.exp(s - mx)
        l = jnp.sum(p, axis=-1, keepdims=True)
        p = p / l
        outs.append(_dot(p.astype(BF16), kv_ref[:, kv_cols + hd * HEAD_DIM:kv_cols + (hd + 1) * HEAD_DIM]))
    o = jnp.concatenate(outs, axis=-1).astype(BF16)
    o_ref[...] = h + _dot(o, wo_ref[...])


def _cross(h, g, wq, kv, wo, layer, batch, seq, n_mem, *, tm):
    m, d = h.shape
    per_b = seq // tm
    kv_cols = 2 * X_HEADS * HEAD_DIM
    return pl.pallas_call(
        _cross_kernel, grid=(m // tm,),
        in_specs=[pl.BlockSpec((tm, d), lambda i: (i, 0)),
                  pl.BlockSpec((1, d), lambda i: (0, 0)),
                  pl.BlockSpec((d, X_HEADS * HEAD_DIM), lambda i: (0, 0)),
                  pl.BlockSpec((n_mem, kv_cols), lambda i: (i // per_b, layer)),
                  pl.BlockSpec((X_HEADS * HEAD_DIM, d), lambda i: (0, 0))],
        out_specs=pl.BlockSpec((tm, d), lambda i: (i, 0)),
        out_shape=jax.ShapeDtypeStruct((m, d), F32),
        compiler_params=_params("parallel"), name="cross_attn",
    )(h, g.reshape(1, d), wq, kv, wo)


def _ffn_kernel(h_ref, halo_ref, g_ref, wg_ref, wv_ref, cwg_ref, cwv_ref, cbg_ref, cbv_ref, wd_ref,
                o_ref, hn_ref, acc_ref, *, per_b):
    i = pl.program_id(0)
    f = pl.program_id(1)
    tm = h_ref.shape[0]
    pad = SUBLANE * 2

    @pl.when(f == 0)
    def _():
        g = g_ref[...]
        hn_ref[pad:, :] = _rms_rows(h_ref[...], g).astype(BF16)
        halo = _rms_rows(halo_ref[...], g)
        halo = jnp.where(i % per_b == 0, 0.0, halo)
        hn_ref[:pad, :] = halo.astype(BF16)
        acc_ref[...] = jnp.zeros_like(acc_ref)

    hn = hn_ref[...]

    def conv(u, w_ref, b_ref):
        w = w_ref[...]
        return (b_ref[...] + w[0:1, :] * u[pad - 2:pad - 2 + tm] + w[1:2, :] * u[pad - 1:pad - 1 + tm]
                + w[2:3, :] * u[pad:pad + tm])

    gate = conv(_dot(hn, wg_ref[...]), cwg_ref, cbg_ref)
    val = conv(_dot(hn, wv_ref[...]), cwv_ref, cbv_ref)
    act = (gate * _sigmoid(gate) * val).astype(BF16)
    acc_ref[...] += _dot(act, wd_ref[...])

    @pl.when(f == pl.num_programs(1) - 1)
    def _():
        o_ref[...] = h_ref[...] + acc_ref[...]


def _ffn(h, g, w_up, conv_w, conv_b, w_down, batch, seq, *, tm, tf):
    m, d = h.shape
    dff = w_down.shape[0]
    assert seq % tm == 0 and dff % tf == 0
    per_b = seq // tm
    nf = dff // tf
    pad = SUBLANE * 2
    halo_blocks = tm // pad
    cw = jnp.zeros((SUBLANE, 2 * dff), F32).at[:CONV_WIDTH].set(conv_w)
    cb = conv_b.reshape(1, 2 * dff)
    return pl.pallas_call(
        functools.partial(_ffn_kernel, per_b=per_b),
        grid=(m // tm, nf),
        in_specs=[pl.BlockSpec((tm, d), lambda i, f: (i, 0)),
                  pl.BlockSpec((pad, d), lambda i, f: (jnp.maximum(i * halo_blocks - 1, 0), 0)),
                  pl.BlockSpec((1, d), lambda i, f: (0, 0)),
                  pl.BlockSpec((d, tf), lambda i, f: (0, f)),
                  pl.BlockSpec((d, tf), lambda i, f: (0, nf + f)),
                  pl.BlockSpec((SUBLANE, tf), lambda i, f: (0, f)),
                  pl.BlockSpec((SUBLANE, tf), lambda i, f: (0, nf + f)),
                  pl.BlockSpec((1, tf), lambda i, f: (0, f)),
                  pl.BlockSpec((1, tf), lambda i, f: (0, nf + f)),
                  pl.BlockSpec((tf, d), lambda i, f: (f, 0))],
        out_specs=pl.BlockSpec((tm, d), lambda i, f: (i, 0)),
        out_shape=jax.ShapeDtypeStruct((m, d), F32),
        scratch_shapes=[pltpu.VMEM((tm + pad, d), BF16), pltpu.VMEM((tm, d), F32)],
        compiler_params=_params("parallel", "arbitrary"), name="conv_ffn",
    )(h, h, g.reshape(1, d), w_up, w_up, cw, cw, cb, cb, w_down)


def _final_norm_kernel(x_ref, g_ref, o_ref):
    o_ref[...] = _rms_rows(x_ref[...], g_ref[...])


def _final_norm(x, g, *, tm):
    m, d = x.shape
    return pl.pallas_call(
        _final_norm_kernel, grid=(m // tm,),
        in_specs=[pl.BlockSpec((tm, d), lambda i: (i, 0)), pl.BlockSpec((1, d), lambda i: (0, 0))],
        out_specs=pl.BlockSpec((tm, d), lambda i: (i, 0)),
        out_shape=jax.ShapeDtypeStruct((m, d), F32),
        compiler_params=_params("parallel"), name="final_norm",
    )(x, g.reshape(1, d))


def _row_tile(m, want):
    return want if m % want == 0 else m


def _mixer_ab(h, g, w_in, pe, w1, w2, w_out, tabs, rot, batch, seq):
    m, d = h.shape
    cos, sin = tabs[0], tabs[1]
    w_pad = jnp.zeros((d, AB_BLOCKS * LANE), BF16).at[:, :AB_COLS].set(w_in.astype(BF16))
    proj = _norm_matmul(h, g, w_pad, tm=_row_tile(m, 512), tn=512, head_major=True)
    o_moba = _moba(proj, cos, sin, rot, batch, seq)
    kvc = _compress(proj, pe, w1, w2, batch, seq)
    o_cmp, sel = _nsa_cmp(proj, kvc, batch, seq)
    o_nsa = _nsa_sw(proj, sel, o_cmp, cos, sin, rot, batch, seq)
    w_o = w_out.astype(BF16)
    n_moba = MOBA_HEADS * HEAD_DIM
    return _matmul_res([o_moba, o_nsa], [w_o[:n_moba], w_o[n_moba:]], h, tm=_row_tile(m, 512), tn=512)


def _mixer_c(h, g, w_in, gn, w_out, tabs, batch, seq):
    m, d = h.shape
    proj = _norm_matmul(h, g, w_in.astype(BF16), tm=_row_tile(m, 512), tn=512)
    y = _retention(proj, tabs[2], tabs[3], gn, batch, seq)
    return _matmul_res([y], [w_out.astype(BF16)], h, tm=_row_tile(m, 512), tn=512)


def kernel(x, mem, positions, norm_mix, norm_cross, norm_ffn, norm_mem, norm_final, w_in_ab, cmp_pe_k, cmp_w1_k,
           cmp_w2_k, cmp_pe_v, cmp_w1_v, cmp_w2_v, w_out_ab, w_in_c, ret_gn, w_out_c, w_q_x, w_kv_x, w_o_x, w_up,
           conv_w, conv_b, w_down):
    batch, seq, d = x.shape
    n_mem = mem.shape[1]
    depth = norm_mix.shape[0]
    m = batch * seq
    tabs = _rope_tables(positions)
    rot = _rot_matrix()

    kv_cols = 2 * X_HEADS * HEAD_DIM
    w_kv = jnp.transpose(w_kv_x, (1, 0, 2)).reshape(d, depth * kv_cols).astype(BF16)
    kv = _norm_matmul(mem.reshape(batch * n_mem, d), norm_mem, w_kv, tm=_row_tile(batch * n_mem, 512), tn=512)

    h = x.reshape(m, d)
    for l in range(depth):
        if l % 2 == 0:
            e = l // 2
            h = _mixer_ab(h, norm_mix[l], w_in_ab[e],
                          jnp.stack([cmp_pe_k[e], cmp_pe_v[e]]), jnp.stack([cmp_w1_k[e], cmp_w1_v[e]]),
                          jnp.stack([cmp_w2_k[e], cmp_w2_v[e]]), w_out_ab[e], tabs, rot, batch, seq)
        else:
            o = l // 2
            h = _mixer_c(h, norm_mix[l], w_in_c[o], ret_gn[o], w_out_c[o], tabs, batch, seq)
        h = _cross(h, norm_cross[l], w_q_x[l].astype(BF16), kv, w_o_x[l].astype(BF16), l, batch, seq, n_mem,
                   tm=_row_tile(seq, 256))
        h = _ffn(h, norm_ffn[l], w_up[l].astype(BF16), conv_w[l], conv_b[l], w_down[l].astype(BF16), batch, seq,
                 tm=_row_tile(seq, 512), tf=512)
    return _final_norm(h, norm_final, tm=_row_tile(m, 512)).reshape(batch, seq, d)
```

```python
import functools

import numpy as np
import jax
import jax.numpy as jnp
from jax import lax
from jax.experimental import pallas as pl
from jax.experimental.pallas import tpu as pltpu

F32 = jnp.float32
BF16 = jnp.bfloat16

D_MODEL = 2048
DEPTH = 4
HEAD_DIM = 128
ROPE_THETA = 500000.0
ROPE_DIM = HEAD_DIM // 4
X_HEADS = 4
MOBA_HEADS = 8
MOBA_BLOCK = 256
MOBA_TOPK = 3
NSA_HEADS = 8
NSA_GROUPS = 2
NSA_REP = NSA_HEADS // NSA_GROUPS
NSA_CMP_LEN = 32
NSA_CMP_STRIDE = 16
NSA_SEL_LEN = 64
NSA_TOPK = 16
NSA_WINDOW = 512
RET_HEADS = 8
RET_DK = 256
RET_DV = 512
RET_THETA = 10000.0
D_FF = 5632
CONV_WIDTH = 3
RMS_EPS = 1e-6
NEG = -1e30
FORCE = 1e9

LANE = 128
SUBLANE = 8
VMEM_LIMIT = 56 * 2 ** 20

AB_SIZES = (MOBA_HEADS * HEAD_DIM,) * 3 + (NSA_HEADS * HEAD_DIM,) + (NSA_GROUPS * HEAD_DIM,) * 6 + (NSA_HEADS * 3,)
AB_COLS = sum(AB_SIZES)
AB_BLOCKS = 48
BLK_MQ, BLK_MK, BLK_MV, BLK_NQ = 0, 8, 16, 24
BLK_NKC, BLK_NVC, BLK_NKS, BLK_NVS, BLK_NKW, BLK_NVW, BLK_NG = 32, 34, 36, 38, 40, 42, 44

ATT_TQ = 256
ATT_TK = 256
RET_CHUNK = 128
MOBA_GROUP = 4
SEL_GROUP = 4
FFN_SPLIT = 1

_NT = (((1,), (1,)), ((), ()))
_TN = (((0,), (0,)), ((), ()))


def _params(*sem):
    return pltpu.CompilerParams(dimension_semantics=sem, vmem_limit_bytes=VMEM_LIMIT)


def _dot(a, b):
    return jnp.dot(a, b, preferred_element_type=F32)


def _dot_nt(a, b):
    return lax.dot_general(a, b, _NT, preferred_element_type=F32)


def _sigmoid(x):
    return 1.0 / (1.0 + jnp.exp(-x))


def _rms_rows(x, g):
    ms = jnp.mean(x * x, axis=-1, keepdims=True)
    return x * lax.rsqrt(ms + RMS_EPS) * g


def _tables_kernel(pos_ref, inv_rope_ref, inv_ret_ref, c_ref, s_ref, cr_ref, sr_ref):
    pos = pos_ref[...]
    lane = lax.broadcasted_iota(jnp.int32, pos.shape, 1)
    rot = lane < ROPE_DIM
    ang = pos * inv_rope_ref[...]
    c_ref[...] = jnp.where(rot, jnp.cos(ang), 1.0)
    s_ref[...] = jnp.where(rot, jnp.sin(ang), 0.0)
    ang_r = pos * inv_ret_ref[...]
    cr_ref[...] = jnp.cos(ang_r)
    sr_ref[...] = jnp.sin(ang_r)


def _rope_tables(positions):
    m = positions.size
    posb = jnp.broadcast_to(positions.reshape(m, 1).astype(F32), (m, LANE))
    half = ROPE_DIM // 2
    inv = jnp.float32(ROPE_THETA) ** (-jnp.arange(half, dtype=F32) / half)
    inv_rope = jnp.concatenate([inv, inv, jnp.zeros((LANE - ROPE_DIM,), F32)]).reshape(1, LANE)
    half_r = RET_DK // 2
    inv_ret = (jnp.float32(RET_THETA) ** (-jnp.arange(half_r, dtype=F32) / half_r)).reshape(1, LANE)
    tm = 1024 if m % 1024 == 0 else m
    row = pl.BlockSpec((tm, LANE), lambda i: (i, 0))
    vec = pl.BlockSpec((1, LANE), lambda i: (0, 0))
    sds = jax.ShapeDtypeStruct((m, LANE), F32)
    return pl.pallas_call(
        _tables_kernel, grid=(m // tm,), in_specs=[row, vec, vec], out_specs=[row] * 4,
        out_shape=[sds] * 4, compiler_params=_params("parallel"), name="rope_tables",
    )(posb, inv_rope, inv_ret)


def _rot_matrix():
    half = ROPE_DIM // 2
    p = np.zeros((HEAD_DIM, HEAD_DIM), np.float32)
    for l in range(half):
        p[l + half, l] = -1.0
        p[l, l + half] = 1.0
    return jnp.asarray(p, BF16)


def _rope_partial(x_bf16, c, s, p_ref):
    return x_bf16.astype(F32) * c + _dot(x_bf16, p_ref[...]) * s


def _norm_matmul_kernel(x_ref, g_ref, w_ref, o_ref, xn_ref, *, head_major):
    @pl.when(pl.program_id(1) == 0)
    def _():
        xn_ref[...] = _rms_rows(x_ref[...], g_ref[...]).astype(BF16)

    acc = _dot(xn_ref[...], w_ref[...])
    if head_major:
        for c in range(o_ref.shape[0]):
            o_ref[c] = acc[:, c * LANE:(c + 1) * LANE].astype(o_ref.dtype)
    else:
        o_ref[...] = acc.astype(o_ref.dtype)


def _norm_matmul(x, g, w, *, tm, tn, head_major=False):
    m, k = x.shape
    n = w.shape[1]
    assert m % tm == 0 and n % tn == 0 and tn % LANE == 0
    if head_major:
        out_shape = jax.ShapeDtypeStruct((n // LANE, m, LANE), BF16)
        out_spec = pl.BlockSpec((tn // LANE, tm, LANE), lambda i, j: (j, i, 0))
    else:
        out_shape = jax.ShapeDtypeStruct((m, n), BF16)
        out_spec = pl.BlockSpec((tm, tn), lambda i, j: (i, j))
    return pl.pallas_call(
        functools.partial(_norm_matmul_kernel, head_major=head_major),
        grid=(m // tm, n // tn),
        in_specs=[pl.BlockSpec((tm, k), lambda i, j: (i, 0)),
                  pl.BlockSpec((1, k), lambda i, j: (0, 0)),
                  pl.BlockSpec((k, tn), lambda i, j: (0, j))],
        out_specs=out_spec, out_shape=out_shape,
        scratch_shapes=[pltpu.VMEM((tm, k), BF16)],
        compiler_params=_params("parallel", "arbitrary"), name="norm_matmul",
    )(x, g.reshape(1, k), w)


def _matmul_res_kernel(*refs):
    r_ref, o_ref = refs[-2], refs[-1]
    acc = r_ref[...]
    for x_ref, w_ref in zip(refs[:-2:2], refs[1:-2:2]):
        acc = acc + _dot(x_ref[...], w_ref[...])
    o_ref[...] = acc


def _matmul_res(xs, ws, res, *, tm, tn):
    m, n = res.shape
    assert m % tm == 0 and n % tn == 0
    in_specs, args = [], []
    for x, w in zip(xs, ws):
        k = x.shape[1]
        in_specs += [pl.BlockSpec((tm, k), lambda i, j: (i, 0)), pl.BlockSpec((k, tn), lambda i, j: (0, j))]
        args += [x, w]
    return pl.pallas_call(
        _matmul_res_kernel, grid=(m // tm, n // tn),
        in_specs=in_specs + [pl.BlockSpec((tm, tn), lambda i, j: (i, j))],
        out_specs=pl.BlockSpec((tm, tn), lambda i, j: (i, j)),
        out_shape=jax.ShapeDtypeStruct((m, n), F32),
        compiler_params=_params("parallel", "arbitrary"), name="matmul_res",
    )(*args, res)


def _topk_rows(x, n_valid, k, n_rows):
    rows, nq = x.shape
    ridx = lax.broadcasted_iota(jnp.int32, (rows, nq), 0)
    x = jnp.where(ridx < n_valid, x, NEG)
    x = jnp.where(ridx < n_rows, x, -jnp.inf)
    rank = jnp.zeros((rows, nq), F32)
    for i in range(n_rows):
        xi = x[i:i + 1, :]
        rank = rank + jnp.where(ridx > i, jnp.where(xi >= x, 1.0, 0.0), jnp.where(xi > x, 1.0, 0.0))
    marks = jnp.where(rank < float(k), jnp.where(x > NEG / 2, 1.0, 0.0), 0.0)
    if rows < LANE:
        marks = jnp.concatenate([marks, jnp.zeros((LANE - rows, nq), F32)], axis=0)
    return marks.T


def _moba_kernel(q_ref, k_ref, v_ref, c_ref, s_ref, p_ref, o_ref, krot_ref, kmean_ref, *, nb):
    qi = pl.program_id(2)
    blk = MOBA_BLOCK
    nb_pad = -(-nb // SUBLANE) * SUBLANE
    grp = MOBA_GROUP
    scale = HEAD_DIM ** -0.5

    @pl.when(qi == 0)
    def _():
        kmean_ref[...] = jnp.zeros_like(kmean_ref)
        for j in range(nb):
            rows = slice(j * blk, (j + 1) * blk)
            kr = _rope_partial(k_ref[rows, :], c_ref[rows, :], s_ref[rows, :], p_ref)
            krot_ref[rows, :] = kr.astype(BF16)
            kmean_ref[j:j + 1, :] = jnp.mean(kr, axis=0, keepdims=True)

    r0 = pl.multiple_of(qi * blk, blk)
    q = _rope_partial(q_ref[...], c_ref[pl.ds(r0, blk), :], s_ref[pl.ds(r0, blk), :], p_ref)
    qb = q.astype(BF16)

    gate_t = lax.dot_general(kmean_ref[...], q, _NT, precision=lax.Precision.HIGHEST,
                             preferred_element_type=F32)[:nb_pad, :]
    sel = _topk_rows(gate_t, qi, min(MOBA_TOPK, nb), nb)
    lane = lax.broadcasted_iota(jnp.int32, (blk, LANE), 1)

    s = _dot_nt(qb, krot_ref[pl.ds(r0, blk), :]) * scale
    row = lax.broadcasted_iota(jnp.int32, (blk, blk), 0)
    col = lax.broadcasted_iota(jnp.int32, (blk, blk), 1)
    s = jnp.where(col <= row, s, NEG)
    m = jnp.max(s, axis=-1, keepdims=True)
    p = jnp.exp(s - m)
    l = jnp.sum(p, axis=-1, keepdims=True)
    acc = _dot(p.astype(BF16), v_ref[pl.ds(r0, blk), :])

    def body(jg, carry):
        m, l, acc = carry
        off = pl.multiple_of(jg * (grp * blk), grp * blk)
        s = _dot_nt(qb, krot_ref[pl.ds(off, grp * blk), :]) * scale
        parts = []
        for t in range(grp):
            chosen = jnp.sum(jnp.where(lane == jg * grp + t, sel, 0.0), axis=-1, keepdims=True)
            parts.append(jnp.where(chosen > 0.5, s[:, t * blk:(t + 1) * blk], NEG))
        s = jnp.concatenate(parts, axis=-1)
        m_new = jnp.maximum(m, jnp.max(s, axis=-1, keepdims=True))
        a = jnp.exp(m - m_new)
        p = jnp.exp(s - m_new)
        l = a * l + jnp.sum(p, axis=-1, keepdims=True)
        acc = a * acc + _dot(p.astype(BF16), v_ref[pl.ds(off, grp * blk), :])
        return m_new, l, acc

    m, l, acc = lax.fori_loop(0, (qi + grp - 1) // grp, body, (m, l, acc))
    o_ref[...] = (acc / l).astype(o_ref.dtype)


def _moba(proj, cos, sin, rot, batch, seq):
    nb = seq // MOBA_BLOCK
    assert seq % (MOBA_BLOCK * MOBA_GROUP) == 0 and nb <= LANE
    m = batch * seq
    head = lambda off: pl.BlockSpec((None, seq, HEAD_DIM), lambda b, h, i: (off + h, b, 0))
    table = pl.BlockSpec((seq, LANE), lambda b, h, i: (b, 0))
    return pl.pallas_call(
        functools.partial(_moba_kernel, nb=nb),
        grid=(batch, MOBA_HEADS, nb),
        in_specs=[pl.BlockSpec((None, MOBA_BLOCK, HEAD_DIM), lambda b, h, i: (BLK_MQ + h, b * nb + i, 0)),
                  head(BLK_MK), head(BLK_MV), table, table,
                  pl.BlockSpec((HEAD_DIM, HEAD_DIM), lambda b, h, i: (0, 0))],
        out_specs=pl.BlockSpec((MOBA_BLOCK, HEAD_DIM), lambda b, h, i: (b * nb + i, h)),
        out_shape=jax.ShapeDtypeStruct((m, MOBA_HEADS * HEAD_DIM), BF16),
        scratch_shapes=[pltpu.VMEM((seq, HEAD_DIM), BF16), pltpu.VMEM((LANE, HEAD_DIM), F32)],
        compiler_params=_params("parallel", "parallel", "arbitrary"), name="moba",
    )(proj, proj, proj, cos, sin, rot)


def _compress_kernel(x_ref, pe_ref, w1_ref, w2_ref, o_ref, xf_ref):
    seq = x_ref.shape[0]
    n_sub = seq // NSA_CMP_STRIDE
    xf_ref[:seq, :] = x_ref[...].astype(F32)
    xf_ref[seq:, :] = jnp.zeros((NSA_CMP_STRIDE, HEAD_DIM), F32)
    hid = _dot(pe_ref[...], w1_ref[...])[0:1, :]
    for r in range(NSA_CMP_LEN):
        rows = xf_ref[pl.ds(r, n_sub, stride=NSA_CMP_STRIDE), :]
        hid = hid + _dot(rows.astype(BF16), w1_ref[r * HEAD_DIM:(r + 1) * HEAD_DIM, :])
    hid = hid * _sigmoid(hid)
    o_ref[...] = _dot(hid.astype(BF16), w2_ref[...]).astype(o_ref.dtype)


def _compress(proj, pe, w1, w2, batch, seq):
    n_sub = seq // NSA_CMP_STRIDE
    flat = NSA_CMP_LEN * HEAD_DIM
    pe_flat = jnp.zeros((2, SUBLANE * 2, flat), BF16).at[:, 0].set(pe.reshape(2, flat).astype(BF16))
    return pl.pallas_call(
        _compress_kernel, grid=(2, batch, NSA_GROUPS),
        in_specs=[pl.BlockSpec((None, seq, HEAD_DIM), lambda t, b, g: (BLK_NKC + 2 * t + g, b, 0)),
                  pl.BlockSpec((None, SUBLANE * 2, flat), lambda t, b, g: (t, 0, 0)),
                  pl.BlockSpec((None, flat, HEAD_DIM), lambda t, b, g: (t, 0, 0)),
                  pl.BlockSpec((None, HEAD_DIM, HEAD_DIM), lambda t, b, g: (t, 0, 0))],
        out_specs=pl.BlockSpec((None, None, None, n_sub, HEAD_DIM), lambda t, b, g: (t, b, g, 0, 0)),
        out_shape=jax.ShapeDtypeStruct((2, batch, NSA_GROUPS, n_sub, HEAD_DIM), BF16),
        scratch_shapes=[pltpu.VMEM((seq + NSA_CMP_STRIDE, HEAD_DIM), F32)],
        compiler_params=_params("parallel", "parallel", "parallel"), name="nsa_compress",
    )(proj, pe_flat, w1.astype(BF16), w2.astype(BF16))


def _nsa_cmp_kernel(q_ref, kc_ref, vc_ref, cov_ref, oc_ref, sel_ref, *, n_sel):
    qi = pl.program_id(2)
    tq = ATT_TQ
    n_cmp = kc_ref.shape[0]
    scale = HEAD_DIM ** -0.5
    q_pos = qi * tq + lax.broadcasted_iota(jnp.int32, (tq, n_cmp), 0)
    n_idx = lax.broadcasted_iota(jnp.int32, (tq, n_cmp), 1)
    ok = n_idx * NSA_CMP_STRIDE + (NSA_CMP_LEN - 1) <= q_pos
    kc = kc_ref[...]
    vc = vc_ref[...]
    p_sum = jnp.zeros((tq, n_cmp), F32)
    for r in range(NSA_REP):
        s = jnp.where(ok, _dot_nt(q_ref[r], kc) * scale, NEG)
        mx = jnp.max(s, axis=-1, keepdims=True)
        e = jnp.where(ok, jnp.exp(s - mx), 0.0)
        l = jnp.sum(e, axis=-1, keepdims=True)
        p = e * jnp.where(l > 0.0, 1.0 / l, 0.0)
        oc_ref[r] = _dot(p.astype(BF16), vc).astype(oc_ref.dtype)
        p_sum = p_sum + p

    n_rows = -(-n_sel // SUBLANE) * SUBLANE
    imp = lax.dot_general(cov_ref[...], p_sum, _NT, precision=lax.Precision.HIGHEST,
                          preferred_element_type=F32)[:n_rows, :]
    ridx = lax.broadcasted_iota(jnp.int32, (n_rows, tq), 0)
    blk = (qi * tq + lax.broadcasted_iota(jnp.int32, (n_rows, tq), 1)) // NSA_SEL_LEN
    forced = (ridx == 0) | (ridx == blk) | (ridx == blk - 1)
    x = jnp.where(forced, FORCE, imp)
    sel_ref[...] = _topk_rows(x, blk[0:1, :] + 1, min(NSA_TOPK, n_sel), n_sel).astype(sel_ref.dtype)


def _cover_matrix(n_cmp_pad, n_cmp, n_sel):
    c_start = np.arange(n_cmp_pad) * NSA_CMP_STRIDE
    s_start = np.arange(LANE) * NSA_SEL_LEN
    cover = (c_start[:, None] < s_start[None, :] + NSA_SEL_LEN) & (c_start[:, None] + NSA_CMP_LEN > s_start[None, :])
    cover &= (np.arange(n_cmp_pad) < n_cmp)[:, None] & (np.arange(LANE) < n_sel)[None, :]
    return jnp.asarray(cover.T, F32)


def _nsa_cmp(proj, kvc, batch, seq):
    nq = seq // ATT_TQ
    n_sub = seq // NSA_CMP_STRIDE
    n_sel = seq // NSA_SEL_LEN
    assert n_sel <= LANE
    m = batch * seq
    cover = _cover_matrix(n_sub, n_sub - NSA_CMP_LEN // NSA_CMP_STRIDE + 1, n_sel)
    cmp_spec = lambda t: pl.BlockSpec((None, None, None, n_sub, HEAD_DIM), lambda b, g, i: (t, b, g, 0, 0))
    return pl.pallas_call(
        functools.partial(_nsa_cmp_kernel, n_sel=n_sel),
        grid=(batch, NSA_GROUPS, nq),
        in_specs=[pl.BlockSpec((NSA_REP, ATT_TQ, HEAD_DIM), lambda b, g, i: (BLK_NQ // NSA_REP + g, b * nq + i, 0)),
                  cmp_spec(0), cmp_spec(1),
                  pl.BlockSpec((LANE, n_sub), lambda b, g, i: (0, 0))],
        out_specs=[pl.BlockSpec((NSA_REP, ATT_TQ, HEAD_DIM), lambda b, g, i: (g, b * nq + i, 0)),
                   pl.BlockSpec((ATT_TQ, LANE), lambda b, g, i: (b * nq + i, g))],
        out_shape=[jax.ShapeDtypeStruct((NSA_HEADS, m, HEAD_DIM), BF16),
                   jax.ShapeDtypeStruct((m, NSA_GROUPS * LANE), BF16)],
        compiler_params=_params("parallel", "parallel", "parallel"), name="nsa_cmp",
    )(proj, kvc, kvc, cover)


def _nsa_sw_kernel(q_ref, ks_ref, vs_ref, kw_ref, vw_ref, sel_ref, oc_ref, gl_ref, c_ref, s_ref, p_ref,
                   o_ref, ksr_ref, kwr_ref, vwp_ref, qr_ref, m_ref, l_ref, acc_ref, ow_ref, *, seq):
    grp = pl.program_id(1)
    qi = pl.program_id(2)
    tq, tk = ATT_TQ, ATT_TK
    rep = NSA_REP
    win = NSA_WINDOW
    gk = SEL_GROUP * tk
    scale = HEAD_DIM ** -0.5

    @pl.when(qi == 0)
    def _():
        kwr_ref[:win, :] = jnp.zeros((win, HEAD_DIM), BF16)
        vwp_ref[:win, :] = jnp.zeros((win, HEAD_DIM), BF16)
        for j in range(seq // tk):
            rows = slice(j * tk, (j + 1) * tk)
            shifted = slice(win + j * tk, win + (j + 1) * tk)
            c, s = c_ref[rows, :], s_ref[rows, :]
            ksr_ref[rows, :] = _rope_partial(ks_ref[rows, :], c, s, p_ref).astype(BF16)
            kwr_ref[shifted, :] = _rope_partial(kw_ref[rows, :], c, s, p_ref).astype(BF16)
            vwp_ref[shifted, :] = vw_ref[rows, :]

    r0 = pl.multiple_of(qi * tq, tq)
    cq = c_ref[pl.ds(r0, tq), :]
    sq = s_ref[pl.ds(r0, tq), :]
    for r in range(rep):
        qr_ref[r] = _rope_partial(q_ref[r], cq, sq, p_ref).astype(BF16)

    row_w = lax.broadcasted_iota(jnp.int32, (tq, win + tq), 0)
    col_w = lax.broadcasted_iota(jnp.int32, (tq, win + tq), 1)
    in_win = (col_w > row_w) & (col_w <= row_w + win) & (col_w >= win - r0)
    kw_slab = kwr_ref[pl.ds(r0, win + tq), :]
    vw_slab = vwp_ref[pl.ds(r0, win + tq), :]
    for r in range(rep):
        s = jnp.where(in_win, _dot_nt(qr_ref[r], kw_slab) * scale, NEG)
        p = jnp.exp(s - jnp.max(s, axis=-1, keepdims=True))
        ow_ref[r] = _dot(p.astype(BF16), vw_slab) / jnp.sum(p, axis=-1, keepdims=True)

    sel = sel_ref[...]
    row = lax.broadcasted_iota(jnp.int32, (tq, gk), 0)
    col = lax.broadcasted_iota(jnp.int32, (tq, gk), 1)

    def chosen_keys(jg):
        blk_row = lax.broadcasted_iota(jnp.int32, (LANE, gk), 0)
        blk_col = jg * (gk // NSA_SEL_LEN) + lax.broadcasted_iota(jnp.int32, (LANE, gk), 1) // NSA_SEL_LEN
        return _dot(sel, jnp.where(blk_row == blk_col, 1.0, 0.0).astype(BF16))

    def attend(off, valid, first):
        k_grp = ksr_ref[pl.ds(off, gk), :]
        v_grp = vs_ref[pl.ds(off, gk), :]
        for r in range(rep):
            s = jnp.where(valid, _dot_nt(qr_ref[r], k_grp) * scale, NEG)
            if first:
                m_new = jnp.max(s, axis=-1, keepdims=True)
                p = jnp.exp(s - m_new)
                l_ref[r] = jnp.sum(p, axis=-1, keepdims=True)
                acc_ref[r] = _dot(p.astype(BF16), v_grp)
            else:
                m_old = m_ref[r]
                m_new = jnp.maximum(m_old, jnp.max(s, axis=-1, keepdims=True))
                a = jnp.exp(m_old - m_new)
                p = jnp.exp(s - m_new)
                l_ref[r] = a * l_ref[r] + jnp.sum(p, axis=-1, keepdims=True)
                acc_ref[r] = a * acc_ref[r] + _dot(p.astype(BF16), v_grp)
            m_ref[r] = m_new

    gd = qi // SEL_GROUP
    off_d = pl.multiple_of(gd * gk, gk)
    causal = col + off_d <= row + r0
    attend(off_d, jnp.where(causal, chosen_keys(gd), 0.0) > 0.5, True)

    def sel_body(jg, carry):
        attend(pl.multiple_of(jg * gk, gk), chosen_keys(jg) > 0.5, False)
        return carry

    lax.fori_loop(0, gd, sel_body, 0)

    gates = _sigmoid(gl_ref[...].astype(F32))
    lane = lax.broadcasted_iota(jnp.int32, (tq, LANE), 1)

    def gate_col(c):
        return jnp.sum(jnp.where(lane == c, gates, 0.0), axis=-1, keepdims=True)

    for r in range(rep):
        base = (grp * rep + r) * 3
        o_s = acc_ref[r] / l_ref[r]
        o = gate_col(base) * oc_ref[r].astype(F32) + gate_col(base + 1) * o_s + gate_col(base + 2) * ow_ref[r]
        o_ref[:, r * HEAD_DIM:(r + 1) * HEAD_DIM] = o.astype(o_ref.dtype)


def _nsa_sw(proj, sel, o_cmp, cos, sin, rot, batch, seq):
    nq = seq // ATT_TQ
    assert seq % (SEL_GROUP * ATT_TK) == 0 and NSA_WINDOW % ATT_TK == 0 and ATT_TQ == ATT_TK
    m = batch * seq
    rep = NSA_REP
    head = lambda off: pl.BlockSpec((None, seq, HEAD_DIM), lambda b, g, i: (off + g, b, 0))
    table = pl.BlockSpec((seq, LANE), lambda b, g, i: (b, 0))
    q_like = lambda off: pl.BlockSpec((rep, ATT_TQ, HEAD_DIM), lambda b, g, i: (off + g, b * nq + i, 0))
    return pl.pallas_call(
        functools.partial(_nsa_sw_kernel, seq=seq),
        grid=(batch, NSA_GROUPS, nq),
        in_specs=[q_like(BLK_NQ // rep), head(BLK_NKS), head(BLK_NVS), head(BLK_NKW), head(BLK_NVW),
                  pl.BlockSpec((ATT_TQ, LANE), lambda b, g, i: (b * nq + i, g)),
                  q_like(0),
                  pl.BlockSpec((None, ATT_TQ, LANE), lambda b, g, i: (BLK_NG, b * nq + i, 0)),
                  table, table,
                  pl.BlockSpec((HEAD_DIM, HEAD_DIM), lambda b, g, i: (0, 0))],
        out_specs=pl.BlockSpec((ATT_TQ, rep * HEAD_DIM), lambda b, g, i: (b * nq + i, g)),
        out_shape=jax.ShapeDtypeStruct((m, NSA_HEADS * HEAD_DIM), BF16),
        scratch_shapes=[pltpu.VMEM((seq, HEAD_DIM), BF16),
                        pltpu.VMEM((seq + NSA_WINDOW, HEAD_DIM), BF16),
                        pltpu.VMEM((seq + NSA_WINDOW, HEAD_DIM), BF16),
                        pltpu.VMEM((rep, ATT_TQ, HEAD_DIM), BF16),
                        pltpu.VMEM((rep, ATT_TQ, 1), F32), pltpu.VMEM((rep, ATT_TQ, 1), F32),
                        pltpu.VMEM((rep, ATT_TQ, HEAD_DIM), F32), pltpu.VMEM((rep, ATT_TQ, HEAD_DIM), F32)],
        compiler_params=_params("parallel", "parallel", "arbitrary"), name="nsa_sel_win",
    )(proj, proj, proj, proj, proj, sel, o_cmp, proj, cos, sin, rot)


def _rope_full(x_bf16, c, s):
    half = x_bf16.shape[-1] // 2
    x = x_bf16.astype(F32)
    x1, x2 = x[:, :half], x[:, half:]
    return jnp.concatenate([x1 * c - x2 * s, x1 * s + x2 * c], axis=-1)


def _retention_kernel(q_ref, k_ref, v_ref, g_ref, c_ref, s_ref, lg_ref, gn_ref, o_ref, state_ref):
    ci = pl.program_id(1)
    ch = q_ref.shape[0]

    @pl.when(ci == 0)
    def _():
        state_ref[...] = jnp.zeros_like(state_ref)

    c, s = c_ref[...], s_ref[...]
    n_row = lax.broadcasted_iota(jnp.int32, (ch, ch), 0)
    n_col = lax.broadcasted_iota(jnp.int32, (ch, ch), 1)
    diff = (n_row - n_col).astype(F32)
    n_vec = lax.broadcasted_iota(jnp.int32, (ch, 1), 0).astype(F32)

    for h in range(RET_HEADS):
        log_g = lg_ref[h][:, 0:1]
        qk_cols = slice(h * RET_DK, (h + 1) * RET_DK)
        v_cols = slice(h * RET_DV, (h + 1) * RET_DV)
        q = _rope_full(q_ref[:, qk_cols], c, s)
        k = _rope_full(k_ref[:, qk_cols], c, s) * (RET_DK ** -0.5)
        v = v_ref[:, v_cols]

        decay = jnp.where(diff >= 0.0, jnp.exp(jnp.maximum(diff, 0.0) * log_g), 0.0)
        q_dec = jnp.exp((n_vec + 1.0) * log_g)
        k_dec = jnp.exp((ch - 1.0 - n_vec) * log_g)
        c_dec = jnp.exp(ch * log_g)

        qb = q.astype(BF16)
        state = state_ref[h]
        scores = _dot_nt(qb, k.astype(BF16)) * decay
        y = _dot(scores.astype(BF16), v) + _dot(qb, state.astype(BF16)) * q_dec
        state_ref[h] = state * c_dec + lax.dot_general((k * k_dec).astype(BF16), v, _TN,
                                                       preferred_element_type=F32)

        mu = jnp.mean(y, axis=-1, keepdims=True)
        yc = y - mu
        var = jnp.mean(yc * yc, axis=-1, keepdims=True)
        yn = yc * lax.rsqrt(var + RMS_EPS) * gn_ref[h]
        gate = g_ref[:, v_cols].astype(F32)
        o_ref[:, v_cols] = (gate * _sigmoid(gate) * yn).astype(o_ref.dtype)


def _retention(proj, cos_r, sin_r, gn, batch, seq):
    ch = RET_CHUNK
    nch = seq // ch
    assert seq % ch == 0
    m = batch * seq
    log_g = jnp.log(1.0 - jnp.exp2(-5.0 - jnp.arange(RET_HEADS, dtype=F32)))
    log_g = jnp.broadcast_to(log_g[:, None, None], (RET_HEADS, 1, LANE))
    qk_w = RET_HEADS * RET_DK
    v_w = RET_HEADS * RET_DV
    assert v_w == 2 * qk_w
    rows = lambda b, c: b * nch + c
    return pl.pallas_call(
        _retention_kernel, grid=(batch, nch),
        in_specs=[pl.BlockSpec((ch, qk_w), lambda b, c: (rows(b, c), 0)),
                  pl.BlockSpec((ch, qk_w), lambda b, c: (rows(b, c), 1)),
                  pl.BlockSpec((ch, v_w), lambda b, c: (rows(b, c), 1)),
                  pl.BlockSpec((ch, v_w), lambda b, c: (rows(b, c), 2)),
                  pl.BlockSpec((ch, LANE), lambda b, c: (rows(b, c), 0)),
                  pl.BlockSpec((ch, LANE), lambda b, c: (rows(b, c), 0)),
                  pl.BlockSpec((RET_HEADS, 1, LANE), lambda b, c: (0, 0, 0)),
                  pl.BlockSpec((RET_HEADS, 1, RET_DV), lambda b, c: (0, 0, 0))],
        out_specs=pl.BlockSpec((ch, v_w), lambda b, c: (rows(b, c), 0)),
        out_shape=jax.ShapeDtypeStruct((m, v_w), BF16),
        scratch_shapes=[pltpu.VMEM((RET_HEADS, RET_DK, RET_DV), F32)],
        compiler_params=_params("parallel", "arbitrary"), name="retention",
    )(proj, proj, proj, proj, cos_r, sin_r, log_g, gn.reshape(RET_HEADS, 1, RET_DV).astype(F32))


def _cross_kernel(h_ref, g_ref, wq_ref, kv_ref, wo_ref, o_ref):
    h = h_ref[...]
    hn = _rms_rows(h, g_ref[...]).astype(BF16)
    q = _dot(hn, wq_ref[...]).astype(BF16)
    scale = HEAD_DIM ** -0.5
    kv_cols = X_HEADS * HEAD_DIM
    outs = []
    for hd in range(X_HEADS):
        cols = slice(hd * HEAD_DIM, (hd + 1) * HEAD_DIM)
        s = _dot_nt(q[:, cols], kv_ref[:, cols]) * scale
        mx = jnp.max(s, axis=-1, keepdims=True)
        p = jnp.exp(s - mx)
        l = jnp.sum(p, axis=-1, keepdims=True)
        p = p / l
        outs.append(_dot(p.astype(BF16), kv_ref[:, kv_cols + hd * HEAD_DIM:kv_cols + (hd + 1) * HEAD_DIM]))
    o = jnp.concatenate(outs, axis=-1).astype(BF16)
    o_ref[...] = h + _dot(o, wo_ref[...])


def _cross(h, g, wq, kv, wo, layer, batch, seq, n_mem, *, tm):
    m, d = h.shape
    per_b = seq // tm
    kv_cols = 2 * X_HEADS * HEAD_DIM
    return pl.pallas_call(
        _cross_kernel, grid=(m // tm,),
        in_specs=[pl.BlockSpec((tm, d), lambda i: (i, 0)),
                  pl.BlockSpec((1, d), lambda i: (0, 0)),
                  pl.BlockSpec((d, X_HEADS * HEAD_DIM), lambda i: (0, 0)),
                  pl.BlockSpec((n_mem, kv_cols), lambda i: (i // per_b, layer)),
                  pl.BlockSpec((X_HEADS * HEAD_DIM, d), lambda i: (0, 0))],
        out_specs=pl.BlockSpec((tm, d), lambda i: (i, 0)),
        out_shape=jax.ShapeDtypeStruct((m, d), F32),
        compiler_params=_params("parallel"), name="cross_attn",
    )(h, g.reshape(1, d), wq, kv, wo)


def _ffn_kernel(h_ref, halo_ref, g_ref, wg_ref, wv_ref, cwg_ref, cwv_ref, cbg_ref, cbv_ref, wd_ref,
                o_ref, hn_ref, acc_ref, *, per_b):
    i = pl.program_id(0)
    f = pl.program_id(1)
    tm = h_ref.shape[0]
    pad = SUBLANE * 2

    @pl.when(f == 0)
    def _():
        g = g_ref[...]
        hn_ref[pad:, :] = _rms_rows(h_ref[...], g).astype(BF16)
        halo = _rms_rows(halo_ref[...], g)
        halo = jnp.where(i % per_b == 0, 0.0, halo)
        hn_ref[:pad, :] = halo.astype(BF16)
        acc_ref[...] = jnp.zeros_like(acc_ref)

    hn = hn_ref[...]

    def conv(u, w, b):
        return (b + w[0:1, :] * u[pad - 2:pad - 2 + tm] + w[1:2, :] * u[pad - 1:pad - 1 + tm]
                + w[2:3, :] * u[pad:pad + tm])

    tf = wd_ref.shape[0]
    width = tf // FFN_SPLIT
    slices = [slice(c * width, (c + 1) * width) for c in range(FFN_SPLIT)]
    ups = [(_dot(hn, wg_ref[:, cols]), _dot(hn, wv_ref[:, cols])) for cols in slices]
    total = None
    for cols, (u_gate, u_val) in zip(slices, ups):
        gate = conv(u_gate, cwg_ref[:, cols], cbg_ref[:, cols])
        val = conv(u_val, cwv_ref[:, cols], cbv_ref[:, cols])
        act = (gate * _sigmoid(gate) * val).astype(BF16)
        part = _dot(act, wd_ref[cols, :])
        total = part if total is None else total + part
    acc_ref[...] += total

    @pl.when(f == pl.num_programs(1) - 1)
    def _():
        o_ref[...] = h_ref[...] + acc_ref[...]


def _ffn(h, g, w_up, conv_w, conv_b, w_down, batch, seq, *, tm, tf):
    m, d = h.shape
    dff = w_down.shape[0]
    assert seq % tm == 0 and dff % tf == 0
    per_b = seq // tm
    nf = dff // tf
    pad = SUBLANE * 2
    halo_blocks = tm // pad
    cw = jnp.zeros((SUBLANE, 2 * dff), F32).at[:CONV_WIDTH].set(conv_w)
    cb = conv_b.reshape(1, 2 * dff)
    return pl.pallas_call(
        functools.partial(_ffn_kernel, per_b=per_b),
        grid=(m // tm, nf),
        in_specs=[pl.BlockSpec((tm, d), lambda i, f: (i, 0)),
                  pl.BlockSpec((pad, d), lambda i, f: (jnp.maximum(i * halo_blocks - 1, 0), 0)),
                  pl.BlockSpec((1, d), lambda i, f: (0, 0)),
                  pl.BlockSpec((d, tf), lambda i, f: (0, f)),
                  pl.BlockSpec((d, tf), lambda i, f: (0, nf + f)),
                  pl.BlockSpec((SUBLANE, tf), lambda i, f: (0, f)),
                  pl.BlockSpec((SUBLANE, tf), lambda i, f: (0, nf + f)),
                  pl.BlockSpec((1, tf), lambda i, f: (0, f)),
                  pl.BlockSpec((1, tf), lambda i, f: (0, nf + f)),
                  pl.BlockSpec((tf, d), lambda i, f: (f, 0))],
        out_specs=pl.BlockSpec((tm, d), lambda i, f: (i, 0)),
        out_shape=jax.ShapeDtypeStruct((m, d), F32),
        scratch_shapes=[pltpu.VMEM((tm + pad, d), BF16), pltpu.VMEM((tm, d), F32)],
        compiler_params=_params("parallel", "arbitrary"), name="conv_ffn",
    )(h, h, g.reshape(1, d), w_up, w_up, cw, cw, cb, cb, w_down)


def _final_norm_kernel(x_ref, g_ref, o_ref):
    o_ref[...] = _rms_rows(x_ref[...], g_ref[...])


def _final_norm(x, g, *, tm):
    m, d = x.shape
    return pl.pallas_call(
        _final_norm_kernel, grid=(m // tm,),
        in_specs=[pl.BlockSpec((tm, d), lambda i: (i, 0)), pl.BlockSpec((1, d), lambda i: (0, 0))],
        out_specs=pl.BlockSpec((tm, d), lambda i: (i, 0)),
        out_shape=jax.ShapeDtypeStruct((m, d), F32),
        compiler_params=_params("parallel"), name="final_norm",
    )(x, g.reshape(1, d))


def _row_tile(m, want):
    return want if m % want == 0 else m


def _mixer_ab(h, g, w_in, pe, w1, w2, w_out, tabs, rot, batch, seq):
    m, d = h.shape
    cos, sin = tabs[0], tabs[1]
    w_pad = jnp.zeros((d, AB_BLOCKS * LANE), BF16).at[:, :AB_COLS].set(w_in.astype(BF16))
    proj = _norm_matmul(h, g, w_pad, tm=_row_tile(m, 1024), tn=512, head_major=True)
    o_moba = _moba(proj, cos, sin, rot, batch, seq)
    kvc = _compress(proj, pe, w1, w2, batch, seq)
    o_cmp, sel = _nsa_cmp(proj, kvc, batch, seq)
    o_nsa = _nsa_sw(proj, sel, o_cmp, cos, sin, rot, batch, seq)
    w_o = w_out.astype(BF16)
    n_moba = MOBA_HEADS * HEAD_DIM
    return _matmul_res([o_moba, o_nsa], [w_o[:n_moba], w_o[n_moba:]], h, tm=_row_tile(m, 1024), tn=512)


def _mixer_c(h, g, w_in, gn, w_out, tabs, batch, seq):
    m, d = h.shape
    proj = _norm_matmul(h, g, w_in.astype(BF16), tm=_row_tile(m, 1024), tn=512)
    y = _retention(proj, tabs[2], tabs[3], gn, batch, seq)
    return _matmul_res([y], [w_out.astype(BF16)], h, tm=_row_tile(m, 1024), tn=512)


def kernel(x, mem, positions, norm_mix, norm_cross, norm_ffn, norm_mem, norm_final, w_in_ab, cmp_pe_k, cmp_w1_k,
           cmp_w2_k, cmp_pe_v, cmp_w1_v, cmp_w2_v, w_out_ab, w_in_c, ret_gn, w_out_c, w_q_x, w_kv_x, w_o_x, w_up,
           conv_w, conv_b, w_down):
    batch, seq, d = x.shape
    n_mem = mem.shape[1]
    depth = norm_mix.shape[0]
    m = batch * seq
    tabs = _rope_tables(positions)
    rot = _rot_matrix()

    kv_cols = 2 * X_HEADS * HEAD_DIM
    w_kv = jnp.transpose(w_kv_x, (1, 0, 2)).reshape(d, depth * kv_cols).astype(BF16)
    kv = _norm_matmul(mem.reshape(batch * n_mem, d), norm_mem, w_kv, tm=_row_tile(batch * n_mem, 512), tn=512)

    h = x.reshape(m, d)
    for l in range(depth):
        if l % 2 == 0:
            e = l // 2
            h = _mixer_ab(h, norm_mix[l], w_in_ab[e],
                          jnp.stack([cmp_pe_k[e], cmp_pe_v[e]]), jnp.stack([cmp_w1_k[e], cmp_w1_v[e]]),
                          jnp.stack([cmp_w2_k[e], cmp_w2_v[e]]), w_out_ab[e], tabs, rot, batch, seq)
        else:
            o = l // 2
            h = _mixer_c(h, norm_mix[l], w_in_c[o], ret_gn[o], w_out_c[o], tabs, batch, seq)
        h = _cross(h, norm_cross[l], w_q_x[l].astype(BF16), kv, w_o_x[l].astype(BF16), l, batch, seq, n_mem,
                   tm=_row_tile(seq, 256))
        h = _ffn(h, norm_ffn[l], w_up[l].astype(BF16), conv_w[l], conv_b[l], w_down[l].astype(BF16), batch, seq,
                 tm=_row_tile(seq, 512), tf=512)
    return _final_norm(h, norm_final, tm=_row_tile(m, 512)).reshape(batch, seq, d)
```

```python
import functools

import numpy as np
import jax
import jax.numpy as jnp
from jax import lax
from jax.experimental import pallas as pl
from jax.experimental.pallas import tpu as pltpu

F32 = jnp.float32
BF16 = jnp.bfloat16

D_MODEL = 2048
DEPTH = 4
HEAD_DIM = 128
ROPE_THETA = 500000.0
ROPE_DIM = HEAD_DIM // 4
X_HEADS = 4
MOBA_HEADS = 8
MOBA_BLOCK = 256
MOBA_TOPK = 3
NSA_HEADS = 8
NSA_GROUPS = 2
NSA_REP = NSA_HEADS // NSA_GROUPS
NSA_CMP_LEN = 32
NSA_CMP_STRIDE = 16
NSA_SEL_LEN = 64
NSA_TOPK = 16
NSA_WINDOW = 512
RET_HEADS = 8
RET_DK = 256
RET_DV = 512
RET_THETA = 10000.0
D_FF = 5632
CONV_WIDTH = 3
RMS_EPS = 1e-6
NEG = -1e30
FORCE = 1e9
LOG2E = 1.4426950408889634
Q_SCALE = HEAD_DIM ** -0.5 * LOG2E

LANE = 128
SUBLANE = 8
VMEM_LIMIT = 56 * 2 ** 20

AB_SIZES = (MOBA_HEADS * HEAD_DIM,) * 3 + (NSA_HEADS * HEAD_DIM,) + (NSA_GROUPS * HEAD_DIM,) * 6 + (NSA_HEADS * 3,)
AB_COLS = sum(AB_SIZES)
AB_BLOCKS = 48
BLK_MQ, BLK_MK, BLK_MV, BLK_NQ = 0, 8, 16, 24
BLK_NKC, BLK_NVC, BLK_NKS, BLK_NVS, BLK_NKW, BLK_NVW, BLK_NG = 32, 34, 36, 38, 40, 42, 44

ATT_TQ = 256
ATT_TK = 256
RET_CHUNK = 128
MOBA_GROUP = 4
SEL_GROUP = 4

_NT = (((1,), (1,)), ((), ()))
_TN = (((0,), (0,)), ((), ()))


def _params(*sem):
    return pltpu.CompilerParams(dimension_semantics=sem, vmem_limit_bytes=VMEM_LIMIT)


def _dot(a, b):
    return jnp.dot(a, b, preferred_element_type=F32)


def _dot_nt(a, b):
    return lax.dot_general(a, b, _NT, preferred_element_type=F32)


def _sigmoid(x):
    return 1.0 / (1.0 + jnp.exp(-x))


def _rms_rows(x, g):
    ms = jnp.mean(x * x, axis=-1, keepdims=True)
    return x * lax.rsqrt(ms + RMS_EPS) * g


def _tables_kernel(pos_ref, inv_rope_ref, inv_ret_ref, c_ref, s_ref, cr_ref, sr_ref):
    pos = pos_ref[...]
    lane = lax.broadcasted_iota(jnp.int32, pos.shape, 1)
    rot = lane < ROPE_DIM
    ang = pos * inv_rope_ref[...]
    c_ref[...] = jnp.where(rot, jnp.cos(ang), 1.0)
    s_ref[...] = jnp.where(rot, jnp.sin(ang), 0.0)
    ang_r = pos * inv_ret_ref[...]
    cr_ref[...] = jnp.cos(ang_r)
    sr_ref[...] = jnp.sin(ang_r)


def _rope_tables(positions):
    m = positions.size
    posb = jnp.broadcast_to(positions.reshape(m, 1).astype(F32), (m, LANE))
    half = ROPE_DIM // 2
    inv = jnp.float32(ROPE_THETA) ** (-jnp.arange(half, dtype=F32) / half)
    inv_rope = jnp.concatenate([inv, inv, jnp.zeros((LANE - ROPE_DIM,), F32)]).reshape(1, LANE)
    half_r = RET_DK // 2
    inv_ret = (jnp.float32(RET_THETA) ** (-jnp.arange(half_r, dtype=F32) / half_r)).reshape(1, LANE)
    tm = 1024 if m % 1024 == 0 else m
    row = pl.BlockSpec((tm, LANE), lambda i: (i, 0))
    vec = pl.BlockSpec((1, LANE), lambda i: (0, 0))
    sds = jax.ShapeDtypeStruct((m, LANE), F32)
    return pl.pallas_call(
        _tables_kernel, grid=(m // tm,), in_specs=[row, vec, vec], out_specs=[row] * 4,
        out_shape=[sds] * 4, compiler_params=_params("parallel"), name="rope_tables",
    )(posb, inv_rope, inv_ret)


def _rot_matrix():
    half = ROPE_DIM // 2
    p = np.zeros((HEAD_DIM, HEAD_DIM), np.float32)
    for l in range(half):
        p[l + half, l] = -1.0
        p[l, l + half] = 1.0
    return jnp.asarray(p, BF16)


def _rope_partial(x_bf16, c, s, p_ref):
    return x_bf16.astype(F32) * c + _dot(x_bf16, p_ref[...]) * s


def _norm_matmul_kernel(x_ref, g_ref, w_ref, o_ref, xn_ref, *, head_major):
    @pl.when(pl.program_id(1) == 0)
    def _():
        xn_ref[...] = _rms_rows(x_ref[...], g_ref[...]).astype(BF16)

    acc = _dot(xn_ref[...], w_ref[...])
    if head_major:
        for c in range(o_ref.shape[0]):
            o_ref[c] = acc[:, c * LANE:(c + 1) * LANE].astype(o_ref.dtype)
    else:
        o_ref[...] = acc.astype(o_ref.dtype)


def _norm_matmul(x, g, w, *, tm, tn, head_major=False):
    m, k = x.shape
    n = w.shape[1]
    assert m % tm == 0 and n % tn == 0 and tn % LANE == 0
    if head_major:
        out_shape = jax.ShapeDtypeStruct((n // LANE, m, LANE), BF16)
        out_spec = pl.BlockSpec((tn // LANE, tm, LANE), lambda i, j: (j, i, 0))
    else:
        out_shape = jax.ShapeDtypeStruct((m, n), BF16)
        out_spec = pl.BlockSpec((tm, tn), lambda i, j: (i, j))
    return pl.pallas_call(
        functools.partial(_norm_matmul_kernel, head_major=head_major),
        grid=(m // tm, n // tn),
        in_specs=[pl.BlockSpec((tm, k), lambda i, j: (i, 0)),
                  pl.BlockSpec((1, k), lambda i, j: (0, 0)),
                  pl.BlockSpec((k, tn), lambda i, j: (0, j))],
        out_specs=out_spec, out_shape=out_shape,
        scratch_shapes=[pltpu.VMEM((tm, k), BF16)],
        compiler_params=_params("parallel", "arbitrary"), name="norm_matmul",
    )(x, g.reshape(1, k), w)


def _matmul_res_kernel(*refs):
    r_ref, o_ref = refs[-2], refs[-1]
    acc = r_ref[...]
    for x_ref, w_ref in zip(refs[:-2:2], refs[1:-2:2]):
        acc = acc + _dot(x_ref[...], w_ref[...])
    o_ref[...] = acc


def _matmul_res(xs, w, res, *, tm, tn):
    m, n = res.shape
    k = xs[0].shape[1]
    assert m % tm == 0 and n % tn == 0 and all(x.shape[1] == k for x in xs) and w.shape[0] == k * len(xs)
    in_specs, args = [], []
    for idx, x in enumerate(xs):
        in_specs += [pl.BlockSpec((tm, k), lambda i, j: (i, 0)),
                     pl.BlockSpec((k, tn), lambda i, j, idx=idx: (idx, j))]
        args += [x, w]
    return pl.pallas_call(
        _matmul_res_kernel, grid=(m // tm, n // tn),
        in_specs=in_specs + [pl.BlockSpec((tm, tn), lambda i, j: (i, j))],
        out_specs=pl.BlockSpec((tm, tn), lambda i, j: (i, j)),
        out_shape=jax.ShapeDtypeStruct((m, n), F32),
        compiler_params=_params("parallel", "arbitrary"), name="matmul_res",
    )(*args, res)


def _topk_rows(x, n_valid, k, n_rows):
    rows, nq = x.shape
    ridx = lax.broadcasted_iota(jnp.int32, (rows, nq), 0)
    x = jnp.where(ridx < n_valid, x, NEG)
    x = jnp.where(ridx < n_rows, x, -jnp.inf)
    rank = jnp.zeros((rows, nq), F32)
    for i in range(n_rows):
        xi = x[i:i + 1, :]
        rank = rank + jnp.where(ridx > i, jnp.where(xi >= x, 1.0, 0.0), jnp.where(xi > x, 1.0, 0.0))
    marks = jnp.where(rank < float(k), jnp.where(x > NEG / 2, 1.0, 0.0), 0.0)
    if rows < LANE:
        marks = jnp.concatenate([marks, jnp.zeros((LANE - rows, nq), F32)], axis=0)
    return marks.T


def _moba_kernel(q_ref, k_ref, v_ref, c_ref, s_ref, p_ref, o_ref, krot_ref, kmean_ref, *, nb):
    qi = pl.program_id(2)
    blk = MOBA_BLOCK
    nb_pad = -(-nb // SUBLANE) * SUBLANE
    grp = MOBA_GROUP

    @pl.when(qi == 0)
    def _():
        kmean_ref[...] = jnp.zeros_like(kmean_ref)
        for j in range(nb):
            rows = slice(j * blk, (j + 1) * blk)
            kr = _rope_partial(k_ref[rows, :], c_ref[rows, :], s_ref[rows, :], p_ref)
            krot_ref[rows, :] = kr.astype(BF16)
            kmean_ref[j:j + 1, :] = jnp.mean(kr, axis=0, keepdims=True)

    r0 = pl.multiple_of(qi * blk, blk)
    q = _rope_partial(q_ref[...], c_ref[pl.ds(r0, blk), :], s_ref[pl.ds(r0, blk), :], p_ref)
    qb = (q * Q_SCALE).astype(BF16)

    gate_t = lax.dot_general(kmean_ref[...], q, _NT, precision=lax.Precision.HIGHEST,
                             preferred_element_type=F32)[:nb_pad, :]
    sel = _topk_rows(gate_t, qi, min(MOBA_TOPK, nb), nb)
    lane = lax.broadcasted_iota(jnp.int32, (blk, LANE), 1)

    s = _dot_nt(qb, krot_ref[pl.ds(r0, blk), :])
    row = lax.broadcasted_iota(jnp.int32, (blk, blk), 0)
    col = lax.broadcasted_iota(jnp.int32, (blk, blk), 1)
    s = jnp.where(col <= row, s, NEG)
    m = jnp.max(s, axis=-1, keepdims=True)
    p = jnp.exp2(s - m)
    l = jnp.sum(p, axis=-1, keepdims=True)
    acc = _dot(p.astype(BF16), v_ref[pl.ds(r0, blk), :])

    def body(jg, carry):
        m, l, acc = carry
        off = pl.multiple_of(jg * (grp * blk), grp * blk)
        s = _dot_nt(qb, krot_ref[pl.ds(off, grp * blk), :])
        parts = []
        for t in range(grp):
            chosen = jnp.sum(jnp.where(lane == jg * grp + t, sel, 0.0), axis=-1, keepdims=True)
            parts.append(jnp.where(chosen > 0.5, s[:, t * blk:(t + 1) * blk], NEG))
        s = jnp.concatenate(parts, axis=-1)
        m_new = jnp.maximum(m, jnp.max(s, axis=-1, keepdims=True))
        a = jnp.exp2(m - m_new)
        p = jnp.exp2(s - m_new)
        l = a * l + jnp.sum(p, axis=-1, keepdims=True)
        acc = a * acc + _dot(p.astype(BF16), v_ref[pl.ds(off, grp * blk), :])
        return m_new, l, acc

    m, l, acc = lax.fori_loop(0, (qi + grp - 1) // grp, body, (m, l, acc))
    o_ref[...] = (acc / l).astype(o_ref.dtype)


def _moba(proj, cos, sin, rot, batch, seq):
    nb = seq // MOBA_BLOCK
    assert seq % (MOBA_BLOCK * MOBA_GROUP) == 0 and nb <= LANE
    m = batch * seq
    head = lambda off: pl.BlockSpec((None, seq, HEAD_DIM), lambda b, h, i: (off + h, b, 0))
    table = pl.BlockSpec((seq, LANE), lambda b, h, i: (b, 0))
    return pl.pallas_call(
        functools.partial(_moba_kernel, nb=nb),
        grid=(batch, MOBA_HEADS, nb),
        in_specs=[pl.BlockSpec((None, MOBA_BLOCK, HEAD_DIM), lambda b, h, i: (BLK_MQ + h, b * nb + i, 0)),
                  head(BLK_MK), head(BLK_MV), table, table,
                  pl.BlockSpec((HEAD_DIM, HEAD_DIM), lambda b, h, i: (0, 0))],
        out_specs=pl.BlockSpec((MOBA_BLOCK, HEAD_DIM), lambda b, h, i: (b * nb + i, h)),
        out_shape=jax.ShapeDtypeStruct((m, MOBA_HEADS * HEAD_DIM), BF16),
        scratch_shapes=[pltpu.VMEM((seq, HEAD_DIM), BF16), pltpu.VMEM((LANE, HEAD_DIM), F32)],
        compiler_params=_params("parallel", "parallel", "arbitrary"), name="moba",
    )(proj, proj, proj, cos, sin, rot)


def _compress_kernel(x_ref, pe_ref, w1_ref, w2_ref, o_ref, xf_ref):
    seq = x_ref.shape[0]
    n_sub = seq // NSA_CMP_STRIDE
    xf_ref[:seq, :] = x_ref[...].astype(F32)
    xf_ref[seq:, :] = jnp.zeros((NSA_CMP_STRIDE, HEAD_DIM), F32)
    hid = _dot(pe_ref[...], w1_ref[...])[0:1, :]
    for r in range(NSA_CMP_LEN):
        rows = xf_ref[pl.ds(r, n_sub, stride=NSA_CMP_STRIDE), :]
        hid = hid + _dot(rows.astype(BF16), w1_ref[r * HEAD_DIM:(r + 1) * HEAD_DIM, :])
    hid = hid * _sigmoid(hid)
    o_ref[...] = _dot(hid.astype(BF16), w2_ref[...]).astype(o_ref.dtype)


def _compress(proj, pe, w1, w2, batch, seq):
    n_sub = seq // NSA_CMP_STRIDE
    flat = NSA_CMP_LEN * HEAD_DIM
    pe_flat = jnp.zeros((2, SUBLANE * 2, flat), BF16).at[:, 0].set(pe.reshape(2, flat).astype(BF16))
    return pl.pallas_call(
        _compress_kernel, grid=(2, batch, NSA_GROUPS),
        in_specs=[pl.BlockSpec((None, seq, HEAD_DIM), lambda t, b, g: (BLK_NKC + 2 * t + g, b, 0)),
                  pl.BlockSpec((None, SUBLANE * 2, flat), lambda t, b, g: (t, 0, 0)),
                  pl.BlockSpec((None, flat, HEAD_DIM), lambda t, b, g: (t, 0, 0)),
                  pl.BlockSpec((None, HEAD_DIM, HEAD_DIM), lambda t, b, g: (t, 0, 0))],
        out_specs=pl.BlockSpec((None, None, None, n_sub, HEAD_DIM), lambda t, b, g: (t, b, g, 0, 0)),
        out_shape=jax.ShapeDtypeStruct((2, batch, NSA_GROUPS, n_sub, HEAD_DIM), BF16),
        scratch_shapes=[pltpu.VMEM((seq + NSA_CMP_STRIDE, HEAD_DIM), F32)],
        compiler_params=_params("parallel", "parallel", "parallel"), name="nsa_compress",
    )(proj, pe_flat, w1.astype(BF16), w2.astype(BF16))


def _nsa_cmp_kernel(q_ref, kc_ref, vc_ref, cov_ref, oc_ref, sel_ref, *, n_sel):
    qi = pl.program_id(2)
    tq = ATT_TQ
    n_cmp = kc_ref.shape[0]
    scale = HEAD_DIM ** -0.5
    q_pos = qi * tq + lax.broadcasted_iota(jnp.int32, (tq, n_cmp), 0)
    n_idx = lax.broadcasted_iota(jnp.int32, (tq, n_cmp), 1)
    ok = n_idx * NSA_CMP_STRIDE + (NSA_CMP_LEN - 1) <= q_pos
    kc = kc_ref[...]
    vc = vc_ref[...]
    p_sum = jnp.zeros((tq, n_cmp), F32)
    for r in range(NSA_REP):
        s = jnp.where(ok, _dot_nt(q_ref[r], kc) * scale, NEG)
        mx = jnp.max(s, axis=-1, keepdims=True)
        e = jnp.where(ok, jnp.exp(s - mx), 0.0)
        l = jnp.sum(e, axis=-1, keepdims=True)
        p = e * jnp.where(l > 0.0, 1.0 / l, 0.0)
        oc_ref[r] = _dot(p.astype(BF16), vc).astype(oc_ref.dtype)
        p_sum = p_sum + p

    n_rows = -(-n_sel // SUBLANE) * SUBLANE
    imp = lax.dot_general(cov_ref[...], p_sum, _NT, precision=lax.Precision.HIGHEST,
                          preferred_element_type=F32)[:n_rows, :]
    ridx = lax.broadcasted_iota(jnp.int32, (n_rows, tq), 0)
    blk = (qi * tq + lax.broadcasted_iota(jnp.int32, (n_rows, tq), 1)) // NSA_SEL_LEN
    forced = (ridx == 0) | (ridx == blk) | (ridx == blk - 1)
    x = jnp.where(forced, FORCE, imp)
    sel_ref[...] = _topk_rows(x, blk[0:1, :] + 1, min(NSA_TOPK, n_sel), n_sel).astype(sel_ref.dtype)


def _cover_matrix(n_cmp_pad, n_cmp, n_sel):
    c_start = np.arange(n_cmp_pad) * NSA_CMP_STRIDE
    s_start = np.arange(LANE) * NSA_SEL_LEN
    cover = (c_start[:, None] < s_start[None, :] + NSA_SEL_LEN) & (c_start[:, None] + NSA_CMP_LEN > s_start[None, :])
    cover &= (np.arange(n_cmp_pad) < n_cmp)[:, None] & (np.arange(LANE) < n_sel)[None, :]
    return jnp.asarray(cover.T, F32)


def _nsa_cmp(proj, kvc, batch, seq):
    nq = seq // ATT_TQ
    n_sub = seq // NSA_CMP_STRIDE
    n_sel = seq // NSA_SEL_LEN
    assert n_sel <= LANE
    m = batch * seq
    cover = _cover_matrix(n_sub, n_sub - NSA_CMP_LEN // NSA_CMP_STRIDE + 1, n_sel)
    cmp_spec = lambda t: pl.BlockSpec((None, None, None, n_sub, HEAD_DIM), lambda b, g, i: (t, b, g, 0, 0))
    return pl.pallas_call(
        functools.partial(_nsa_cmp_kernel, n_sel=n_sel),
        grid=(batch, NSA_GROUPS, nq),
        in_specs=[pl.BlockSpec((NSA_REP, ATT_TQ, HEAD_DIM), lambda b, g, i: (BLK_NQ // NSA_REP + g, b * nq + i, 0)),
                  cmp_spec(0), cmp_spec(1),
                  pl.BlockSpec((LANE, n_sub), lambda b, g, i: (0, 0))],
        out_specs=[pl.BlockSpec((NSA_REP, ATT_TQ, HEAD_DIM), lambda b, g, i: (g, b * nq + i, 0)),
                   pl.BlockSpec((ATT_TQ, LANE), lambda b, g, i: (b * nq + i, g))],
        out_shape=[jax.ShapeDtypeStruct((NSA_HEADS, m, HEAD_DIM), BF16),
                   jax.ShapeDtypeStruct((m, NSA_GROUPS * LANE), BF16)],
        compiler_params=_params("parallel", "parallel", "parallel"), name="nsa_cmp",
    )(proj, kvc, kvc, cover)


def _nsa_sw_kernel(q_ref, ks_ref, vs_ref, kw_ref, vw_ref, sel_ref, oc_ref, gl_ref, c_ref, s_ref, p_ref,
                   o_ref, ksr_ref, kwr_ref, vwp_ref, qr_ref, m_ref, l_ref, acc_ref, ow_ref, *, seq):
    grp = pl.program_id(1)
    qi = pl.program_id(2)
    tq, tk = ATT_TQ, ATT_TK
    rep = NSA_REP
    win = NSA_WINDOW
    gk = SEL_GROUP * tk

    @pl.when(qi == 0)
    def _():
        kwr_ref[:win, :] = jnp.zeros((win, HEAD_DIM), BF16)
        vwp_ref[:win, :] = jnp.zeros((win, HEAD_DIM), BF16)
        for j in range(seq // tk):
            rows = slice(j * tk, (j + 1) * tk)
            shifted = slice(win + j * tk, win + (j + 1) * tk)
            c, s = c_ref[rows, :], s_ref[rows, :]
            ksr_ref[rows, :] = _rope_partial(ks_ref[rows, :], c, s, p_ref).astype(BF16)
            kwr_ref[shifted, :] = _rope_partial(kw_ref[rows, :], c, s, p_ref).astype(BF16)
            vwp_ref[shifted, :] = vw_ref[rows, :]

    r0 = pl.multiple_of(qi * tq, tq)
    cq = c_ref[pl.ds(r0, tq), :]
    sq = s_ref[pl.ds(r0, tq), :]
    for r in range(rep):
        qr_ref[r] = (_rope_partial(q_ref[r], cq, sq, p_ref) * Q_SCALE).astype(BF16)

    q_all = qr_ref[...].reshape(rep * tq, HEAD_DIM)

    row_w = lax.broadcasted_iota(jnp.int32, (tq, win + tq), 0)
    col_w = lax.broadcasted_iota(jnp.int32, (tq, win + tq), 1)
    in_win = (col_w > row_w) & (col_w <= row_w + win) & (col_w >= win - r0)
    s = _dot_nt(q_all, kwr_ref[pl.ds(r0, win + tq), :]).reshape(rep, tq, win + tq)
    s = jnp.where(in_win[None], s, NEG)
    p = jnp.exp2(s - jnp.max(s, axis=-1, keepdims=True))
    l = jnp.sum(p, axis=-1, keepdims=True)
    o_w = _dot(p.astype(BF16).reshape(rep * tq, win + tq), vwp_ref[pl.ds(r0, win + tq), :])
    ow_ref[...] = o_w.reshape(rep, tq, HEAD_DIM) / l

    sel = sel_ref[...]
    row = lax.broadcasted_iota(jnp.int32, (tq, gk), 0)
    col = lax.broadcasted_iota(jnp.int32, (tq, gk), 1)

    def chosen_keys(jg):
        blk_row = lax.broadcasted_iota(jnp.int32, (LANE, gk), 0)
        blk_col = jg * (gk // NSA_SEL_LEN) + lax.broadcasted_iota(jnp.int32, (LANE, gk), 1) // NSA_SEL_LEN
        return _dot(sel, jnp.where(blk_row == blk_col, 1.0, 0.0).astype(BF16))

    def attend(off, valid, first):
        s = _dot_nt(q_all, ksr_ref[pl.ds(off, gk), :]).reshape(rep, tq, gk)
        s = jnp.where(valid[None], s, NEG)
        m_new = jnp.max(s, axis=-1, keepdims=True)
        if not first:
            m_old = m_ref[...]
            m_new = jnp.maximum(m_old, m_new)
        p = jnp.exp2(s - m_new)
        l = jnp.sum(p, axis=-1, keepdims=True)
        pv = _dot(p.astype(BF16).reshape(rep * tq, gk), vs_ref[pl.ds(off, gk), :]).reshape(rep, tq, HEAD_DIM)
        if first:
            l_ref[...] = l
            acc_ref[...] = pv
        else:
            a = jnp.exp2(m_old - m_new)
            l_ref[...] = a * l_ref[...] + l
            acc_ref[...] = a * acc_ref[...] + pv
        m_ref[...] = m_new

    gd = qi // SEL_GROUP
    off_d = pl.multiple_of(gd * gk, gk)
    causal = col + off_d <= row + r0
    attend(off_d, jnp.where(causal, chosen_keys(gd), 0.0) > 0.5, True)

    def sel_body(jg, carry):
        attend(pl.multiple_of(jg * gk, gk), chosen_keys(jg) > 0.5, False)
        return carry

    lax.fori_loop(0, gd, sel_body, 0)

    gates = _sigmoid(gl_ref[...].astype(F32))
    lane = lax.broadcasted_iota(jnp.int32, (tq, LANE), 1)

    def gate_col(c):
        return jnp.sum(jnp.where(lane == c, gates, 0.0), axis=-1, keepdims=True)

    for r in range(rep):
        base = (grp * rep + r) * 3
        o_s = acc_ref[r] / l_ref[r]
        o = gate_col(base) * oc_ref[r].astype(F32) + gate_col(base + 1) * o_s + gate_col(base + 2) * ow_ref[r]
        o_ref[:, r * HEAD_DIM:(r + 1) * HEAD_DIM] = o.astype(o_ref.dtype)


def _nsa_sw(proj, sel, o_cmp, cos, sin, rot, batch, seq):
    nq = seq // ATT_TQ
    assert seq % (SEL_GROUP * ATT_TK) == 0 and NSA_WINDOW % ATT_TK == 0 and ATT_TQ == ATT_TK
    m = batch * seq
    rep = NSA_REP
    head = lambda off: pl.BlockSpec((None, seq, HEAD_DIM), lambda b, g, i: (off + g, b, 0))
    table = pl.BlockSpec((seq, LANE), lambda b, g, i: (b, 0))
    q_like = lambda off: pl.BlockSpec((rep, ATT_TQ, HEAD_DIM), lambda b, g, i: (off + g, b * nq + i, 0))
    return pl.pallas_call(
        functools.partial(_nsa_sw_kernel, seq=seq),
        grid=(batch, NSA_GROUPS, nq),
        in_specs=[q_like(BLK_NQ // rep), head(BLK_NKS), head(BLK_NVS), head(BLK_NKW), head(BLK_NVW),
                  pl.BlockSpec((ATT_TQ, LANE), lambda b, g, i: (b * nq + i, g)),
                  q_like(0),
                  pl.BlockSpec((None, ATT_TQ, LANE), lambda b, g, i: (BLK_NG, b * nq + i, 0)),
                  table, table,
                  pl.BlockSpec((HEAD_DIM, HEAD_DIM), lambda b, g, i: (0, 0))],
        out_specs=pl.BlockSpec((ATT_TQ, rep * HEAD_DIM), lambda b, g, i: (b * nq + i, g)),
        out_shape=jax.ShapeDtypeStruct((m, NSA_HEADS * HEAD_DIM), BF16),
        scratch_shapes=[pltpu.VMEM((seq, HEAD_DIM), BF16),
                        pltpu.VMEM((seq + NSA_WINDOW, HEAD_DIM), BF16),
                        pltpu.VMEM((seq + NSA_WINDOW, HEAD_DIM), BF16),
                        pltpu.VMEM((rep, ATT_TQ, HEAD_DIM), BF16),
                        pltpu.VMEM((rep, ATT_TQ, 1), F32), pltpu.VMEM((rep, ATT_TQ, 1), F32),
                        pltpu.VMEM((rep, ATT_TQ, HEAD_DIM), F32), pltpu.VMEM((rep, ATT_TQ, HEAD_DIM), F32)],
        compiler_params=_params("parallel", "parallel", "arbitrary"), name="nsa_sel_win",
    )(proj, proj, proj, proj, proj, sel, o_cmp, proj, cos, sin, rot)


def _rope_full(x_bf16, c, s):
    half = x_bf16.shape[-1] // 2
    x = x_bf16.astype(F32)
    x1, x2 = x[:, :half], x[:, half:]
    return jnp.concatenate([x1 * c - x2 * s, x1 * s + x2 * c], axis=-1)


def _retention_kernel(q_ref, k_ref, v_ref, g_ref, c_ref, s_ref, lg_ref, gn_ref, o_ref, state_ref):
    ci = pl.program_id(1)
    ch = q_ref.shape[0]

    @pl.when(ci == 0)
    def _():
        state_ref[...] = jnp.zeros_like(state_ref)

    c, s = c_ref[...], s_ref[...]
    n_row = lax.broadcasted_iota(jnp.int32, (ch, ch), 0)
    n_col = lax.broadcasted_iota(jnp.int32, (ch, ch), 1)
    diff = (n_row - n_col).astype(F32)
    n_vec = lax.broadcasted_iota(jnp.int32, (ch, 1), 0).astype(F32)

    for h in range(RET_HEADS):
        log_g = lg_ref[h][:, 0:1]
        qk_cols = slice(h * RET_DK, (h + 1) * RET_DK)
        v_cols = slice(h * RET_DV, (h + 1) * RET_DV)
        q = _rope_full(q_ref[:, qk_cols], c, s)
        k = _rope_full(k_ref[:, qk_cols], c, s) * (RET_DK ** -0.5)
        v = v_ref[:, v_cols]

        decay = jnp.where(diff >= 0.0, jnp.exp(jnp.maximum(diff, 0.0) * log_g), 0.0)
        q_dec = jnp.exp((n_vec + 1.0) * log_g)
        k_dec = jnp.exp((ch - 1.0 - n_vec) * log_g)
        c_dec = jnp.exp(ch * log_g)

        qb = q.astype(BF16)
        state = state_ref[h]
        scores = _dot_nt(qb, k.astype(BF16)) * decay
        y = _dot(scores.astype(BF16), v) + _dot(qb, state.astype(BF16)) * q_dec
        state_ref[h] = state * c_dec + lax.dot_general((k * k_dec).astype(BF16), v, _TN,
                                                       preferred_element_type=F32)

        mu = jnp.mean(y, axis=-1, keepdims=True)
        yc = y - mu
        var = jnp.mean(yc * yc, axis=-1, keepdims=True)
        yn = yc * lax.rsqrt(var + RMS_EPS) * gn_ref[h]
        gate = g_ref[:, v_cols].astype(F32)
        o_ref[:, v_cols] = (gate * _sigmoid(gate) * yn).astype(o_ref.dtype)


def _retention(proj, cos_r, sin_r, gn, batch, seq):
    ch = RET_CHUNK
    nch = seq // ch
    assert seq % ch == 0
    m = batch * seq
    log_g = jnp.log(1.0 - jnp.exp2(-5.0 - jnp.arange(RET_HEADS, dtype=F32)))
    log_g = jnp.broadcast_to(log_g[:, None, None], (RET_HEADS, 1, LANE))
    qk_w = RET_HEADS * RET_DK
    v_w = RET_HEADS * RET_DV
    assert v_w == 2 * qk_w
    rows = lambda b, c: b * nch + c
    return pl.pallas_call(
        _retention_kernel, grid=(batch, nch),
        in_specs=[pl.BlockSpec((ch, qk_w), lambda b, c: (rows(b, c), 0)),
                  pl.BlockSpec((ch, qk_w), lambda b, c: (rows(b, c), 1)),
                  pl.BlockSpec((ch, v_w), lambda b, c: (rows(b, c), 1)),
                  pl.BlockSpec((ch, v_w), lambda b, c: (rows(b, c), 2)),
                  pl.BlockSpec((ch, LANE), lambda b, c: (rows(b, c), 0)),
                  pl.BlockSpec((ch, LANE), lambda b, c: (rows(b, c), 0)),
                  pl.BlockSpec((RET_HEADS, 1, LANE), lambda b, c: (0, 0, 0)),
                  pl.BlockSpec((RET_HEADS, 1, RET_DV), lambda b, c: (0, 0, 0))],
        out_specs=pl.BlockSpec((ch, v_w), lambda b, c: (rows(b, c), 0)),
        out_shape=jax.ShapeDtypeStruct((m, v_w), BF16),
        scratch_shapes=[pltpu.VMEM((RET_HEADS, RET_DK, RET_DV), F32)],
        compiler_params=_params("parallel", "arbitrary"), name="retention",
    )(proj, proj, proj, proj, cos_r, sin_r, log_g, gn.reshape(RET_HEADS, 1, RET_DV).astype(F32))


def _cross_kernel(h_ref, g_ref, wq_ref, kv_ref, wo_ref, o_ref):
    h = h_ref[...]
    hn = _rms_rows(h, g_ref[...]).astype(BF16)
    q = (_dot(hn, wq_ref[...]) * Q_SCALE).astype(BF16)
    kv_cols = X_HEADS * HEAD_DIM
    outs = []
    for hd in range(X_HEADS):
        cols = slice(hd * HEAD_DIM, (hd + 1) * HEAD_DIM)
        s = _dot_nt(q[:, cols], kv_ref[:, cols])
        mx = jnp.max(s, axis=-1, keepdims=True)
        p = jnp.exp2(s - mx)
        l = jnp.sum(p, axis=-1, keepdims=True)
        p = p / l
        outs.append(_dot(p.astype(BF16), kv_ref[:, kv_cols + hd * HEAD_DIM:kv_cols + (hd + 1) * HEAD_DIM]))
    o = jnp.concatenate(outs, axis=-1).astype(BF16)
    o_ref[...] = h + _dot(o, wo_ref[...])


def _cross(h, g, wq, kv, wo, layer, batch, seq, n_mem, *, tm):
    m, d = h.shape
    per_b = seq // tm
    kv_cols = 2 * X_HEADS * HEAD_DIM
    return pl.pallas_call(
        _cross_kernel, grid=(m // tm,),
        in_specs=[pl.BlockSpec((tm, d), lambda i: (i, 0)),
                  pl.BlockSpec((1, d), lambda i: (0, 0)),
                  pl.BlockSpec((d, X_HEADS * HEAD_DIM), lambda i: (0, 0)),
                  pl.BlockSpec((n_mem, kv_cols), lambda i: (i // per_b, layer)),
                  pl.BlockSpec((X_HEADS * HEAD_DIM, d), lambda i: (0, 0))],
        out_specs=pl.BlockSpec((tm, d), lambda i: (i, 0)),
        out_shape=jax.ShapeDtypeStruct((m, d), F32),
        compiler_params=_params("parallel"), name="cross_attn",
    )(h, g.reshape(1, d), wq, kv, wo)


def _ffn_kernel(h_ref, halo_ref, g_ref, wg_ref, wv_ref, cwg_ref, cwv_ref, cbg_ref, cbv_ref, wd_ref,
                o_ref, hn_ref, acc_ref, act_a_ref, act_b_ref, *, per_b, nf):
    i = pl.program_id(0)
    f = pl.program_id(1)
    tm = h_ref.shape[0]
    pad = SUBLANE * 2

    def conv(u, w, b):
        return (b + w[0:1, :] * u[pad - 2:pad - 2 + tm] + w[1:2, :] * u[pad - 1:pad - 1 + tm]
                + w[2:3, :] * u[pad:pad + tm])

    def activation():
        hn = hn_ref[...]
        gate = conv(_dot(hn, wg_ref[...]), cwg_ref[...], cbg_ref[...])
        val = conv(_dot(hn, wv_ref[...]), cwv_ref[...], cbv_ref[...])
        return (gate * _sigmoid(gate) * val).astype(BF16)

    @pl.when(f == 0)
    def _():
        g = g_ref[...]
        hn_ref[pad:, :] = _rms_rows(h_ref[...], g).astype(BF16)
        halo = _rms_rows(halo_ref[...], g)
        halo = jnp.where(i % per_b == 0, 0.0, halo)
        hn_ref[:pad, :] = halo.astype(BF16)
        acc_ref[...] = jnp.zeros_like(acc_ref)
        act_a_ref[...] = activation()

    bufs = (act_a_ref, act_b_ref)
    for parity in range(2):
        @pl.when((f > 0) & (f < nf) & (f % 2 == parity))
        def _(parity=parity):
            part = _dot(bufs[1 - parity][...], wd_ref[...])
            bufs[parity][...] = activation()
            acc_ref[...] += part

    @pl.when(f == nf)
    def _():
        o_ref[...] = h_ref[...] + acc_ref[...] + _dot(bufs[(nf - 1) % 2][...], wd_ref[...])


def _ffn(h, g, w_up, conv_w, conv_b, w_down, batch, seq, *, tm, tf):
    m, d = h.shape
    dff = w_down.shape[0]
    assert seq % tm == 0 and dff % tf == 0
    per_b = seq // tm
    nf = dff // tf
    pad = SUBLANE * 2
    halo_blocks = tm // pad
    cw = jnp.zeros((SUBLANE, 2 * dff), F32).at[:CONV_WIDTH].set(conv_w)
    cb = conv_b.reshape(1, 2 * dff)
    up = lambda f: jnp.minimum(f, nf - 1)
    down = lambda f: jnp.maximum(f - 1, 0)
    return pl.pallas_call(
        functools.partial(_ffn_kernel, per_b=per_b, nf=nf),
        grid=(m // tm, nf + 1),
        in_specs=[pl.BlockSpec((tm, d), lambda i, f: (i, 0)),
                  pl.BlockSpec((pad, d), lambda i, f: (jnp.maximum(i * halo_blocks - 1, 0), 0)),
                  pl.BlockSpec((1, d), lambda i, f: (0, 0)),
                  pl.BlockSpec((d, tf), lambda i, f: (0, up(f))),
                  pl.BlockSpec((d, tf), lambda i, f: (0, nf + up(f))),
                  pl.BlockSpec((SUBLANE, tf), lambda i, f: (0, up(f))),
                  pl.BlockSpec((SUBLANE, tf), lambda i, f: (0, nf + up(f))),
                  pl.BlockSpec((1, tf), lambda i, f: (0, up(f))),
                  pl.BlockSpec((1, tf), lambda i, f: (0, nf + up(f))),
                  pl.BlockSpec((tf, d), lambda i, f: (down(f), 0))],
        out_specs=pl.BlockSpec((tm, d), lambda i, f: (i, 0)),
        out_shape=jax.ShapeDtypeStruct((m, d), F32),
        scratch_shapes=[pltpu.VMEM((tm + pad, d), BF16), pltpu.VMEM((tm, d), F32),
                        pltpu.VMEM((tm, tf), BF16), pltpu.VMEM((tm, tf), BF16)],
        compiler_params=_params("parallel", "arbitrary"), name="conv_ffn",
    )(h, h, g.reshape(1, d), w_up, w_up, cw, cw, cb, cb, w_down)


def _final_norm_kernel(x_ref, g_ref, o_ref):
    o_ref[...] = _rms_rows(x_ref[...], g_ref[...])


def _final_norm(x, g, *, tm):
    m, d = x.shape
    return pl.pallas_call(
        _final_norm_kernel, grid=(m // tm,),
        in_specs=[pl.BlockSpec((tm, d), lambda i: (i, 0)), pl.BlockSpec((1, d), lambda i: (0, 0))],
        out_specs=pl.BlockSpec((tm, d), lambda i: (i, 0)),
        out_shape=jax.ShapeDtypeStruct((m, d), F32),
        compiler_params=_params("parallel"), name="final_norm",
    )(x, g.reshape(1, d))


def _row_tile(m, want):
    return want if m % want == 0 else m


def _mixer_ab(h, g, w_in, pe, w1, w2, w_out, tabs, rot, batch, seq):
    m, d = h.shape
    cos, sin = tabs[0], tabs[1]
    w_pad = jnp.zeros((d, AB_BLOCKS * LANE), BF16).at[:, :AB_COLS].set(w_in.astype(BF16))
    proj = _norm_matmul(h, g, w_pad, tm=_row_tile(m, 1024), tn=1024, head_major=True)
    o_moba = _moba(proj, cos, sin, rot, batch, seq)
    kvc = _compress(proj, pe, w1, w2, batch, seq)
    o_cmp, sel = _nsa_cmp(proj, kvc, batch, seq)
    o_nsa = _nsa_sw(proj, sel, o_cmp, cos, sin, rot, batch, seq)
    return _matmul_res([o_moba, o_nsa], w_out.astype(BF16), h, tm=_row_tile(m, 1024), tn=512)


def _mixer_c(h, g, w_in, gn, w_out, tabs, batch, seq):
    m, d = h.shape
    proj = _norm_matmul(h, g, w_in.astype(BF16), tm=_row_tile(m, 1024), tn=1024)
    y = _retention(proj, tabs[2], tabs[3], gn, batch, seq)
    return _matmul_res([y], w_out.astype(BF16), h, tm=_row_tile(m, 1024), tn=512)


def kernel(x, mem, positions, norm_mix, norm_cross, norm_ffn, norm_mem, norm_final, w_in_ab, cmp_pe_k, cmp_w1_k,
           cmp_w2_k, cmp_pe_v, cmp_w1_v, cmp_w2_v, w_out_ab, w_in_c, ret_gn, w_out_c, w_q_x, w_kv_x, w_o_x, w_up,
           conv_w, conv_b, w_down):
    batch, seq, d = x.shape
    n_mem = mem.shape[1]
    depth = norm_mix.shape[0]
    m = batch * seq
    tabs = _rope_tables(positions)
    rot = _rot_matrix()

    kv_cols = 2 * X_HEADS * HEAD_DIM
    w_kv = jnp.transpose(w_kv_x, (1, 0, 2)).reshape(d, depth * kv_cols).astype(BF16)
    kv = _norm_matmul(mem.reshape(batch * n_mem, d), norm_mem, w_kv, tm=_row_tile(batch * n_mem, 512), tn=512)

    h = x.reshape(m, d)
    for l in range(depth):
        if l % 2 == 0:
            e = l // 2
            h = _mixer_ab(h, norm_mix[l], w_in_ab[e],
                          jnp.stack([cmp_pe_k[e], cmp_pe_v[e]]), jnp.stack([cmp_w1_k[e], cmp_w1_v[e]]),
                          jnp.stack([cmp_w2_k[e], cmp_w2_v[e]]), w_out_ab[e], tabs, rot, batch, seq)
        else:
            o = l // 2
            h = _mixer_c(h, norm_mix[l], w_in_c[o], ret_gn[o], w_out_c[o], tabs, batch, seq)
        h = _cross(h, norm_cross[l], w_q_x[l].astype(BF16), kv, w_o_x[l].astype(BF16), l, batch, seq, n_mem,
                   tm=_row_tile(seq, 256))
        h = _ffn(h, norm_ffn[l], w_up[l].astype(BF16), conv_w[l], conv_b[l], w_down[l].astype(BF16), batch, seq,
                 tm=_row_tile(seq, 512), tf=512)
    return _final_norm(h, norm_final, tm=_row_tile(m, 512)).reshape(batch, seq, d)
```

```python
import functools

import numpy as np
import jax
import jax.numpy as jnp
from jax import lax
from jax.experimental import pallas as pl
from jax.experimental.pallas import tpu as pltpu

F32 = jnp.float32
BF16 = jnp.bfloat16

D_MODEL = 2048
DEPTH = 4
HEAD_DIM = 128
ROPE_THETA = 500000.0
ROPE_DIM = HEAD_DIM // 4
X_HEADS = 4
MOBA_HEADS = 8
MOBA_BLOCK = 256
MOBA_TOPK = 3
NSA_HEADS = 8
NSA_GROUPS = 2
NSA_REP = NSA_HEADS // NSA_GROUPS
NSA_CMP_LEN = 32
NSA_CMP_STRIDE = 16
NSA_SEL_LEN = 64
NSA_TOPK = 16
NSA_WINDOW = 512
RET_HEADS = 8
RET_DK = 256
RET_DV = 512
RET_THETA = 10000.0
D_FF = 5632
CONV_WIDTH = 3
RMS_EPS = 1e-6
NEG = -1e30
FORCE = 1e9
LOG2E = 1.4426950408889634
Q_SCALE = HEAD_DIM ** -0.5 * LOG2E

LANE = 128
SUBLANE = 8
VMEM_LIMIT = 56 * 2 ** 20

AB_SIZES = (MOBA_HEADS * HEAD_DIM,) * 3 + (NSA_HEADS * HEAD_DIM,) + (NSA_GROUPS * HEAD_DIM,) * 6 + (NSA_HEADS * 3,)
AB_COLS = sum(AB_SIZES)
AB_BLOCKS = 48
BLK_MQ, BLK_MK, BLK_MV, BLK_NQ = 0, 8, 16, 24
BLK_NKC, BLK_NVC, BLK_NKS, BLK_NVS, BLK_NKW, BLK_NVW, BLK_NG = 32, 34, 36, 38, 40, 42, 44

ATT_TQ = 256
ATT_TK = 256
RET_CHUNK = 128
MOBA_GROUP = 4
SEL_GROUP = 4

_NT = (((1,), (1,)), ((), ()))
_TN = (((0,), (0,)), ((), ()))


def _params(*sem):
    return pltpu.CompilerParams(dimension_semantics=sem, vmem_limit_bytes=VMEM_LIMIT)


def _dot(a, b):
    return jnp.dot(a, b, preferred_element_type=F32)


def _dot_nt(a, b):
    return lax.dot_general(a, b, _NT, preferred_element_type=F32)


def _sigmoid(x):
    return 1.0 / (1.0 + jnp.exp(-x))


def _rms_rows(x, g):
    ms = jnp.mean(x * x, axis=-1, keepdims=True)
    return x * lax.rsqrt(ms + RMS_EPS) * g


def _tables_kernel(pos_ref, inv_rope_ref, inv_ret_ref, c_ref, s_ref, cr_ref, sr_ref):
    pos = pos_ref[...]
    lane = lax.broadcasted_iota(jnp.int32, pos.shape, 1)
    rot = lane < ROPE_DIM
    ang = pos * inv_rope_ref[...]
    c_ref[...] = jnp.where(rot, jnp.cos(ang), 1.0)
    s_ref[...] = jnp.where(rot, jnp.sin(ang), 0.0)
    ang_r = pos * inv_ret_ref[...]
    cr_ref[...] = jnp.cos(ang_r)
    sr_ref[...] = jnp.sin(ang_r)


def _rope_tables(positions):
    m = positions.size
    posb = jnp.broadcast_to(positions.reshape(m, 1).astype(F32), (m, LANE))
    half = ROPE_DIM // 2
    inv = jnp.float32(ROPE_THETA) ** (-jnp.arange(half, dtype=F32) / half)
    inv_rope = jnp.concatenate([inv, inv, jnp.zeros((LANE - ROPE_DIM,), F32)]).reshape(1, LANE)
    half_r = RET_DK // 2
    inv_ret = (jnp.float32(RET_THETA) ** (-jnp.arange(half_r, dtype=F32) / half_r)).reshape(1, LANE)
    tm = 1024 if m % 1024 == 0 else m
    row = pl.BlockSpec((tm, LANE), lambda i: (i, 0))
    vec = pl.BlockSpec((1, LANE), lambda i: (0, 0))
    sds = jax.ShapeDtypeStruct((m, LANE), F32)
    return pl.pallas_call(
        _tables_kernel, grid=(m // tm,), in_specs=[row, vec, vec], out_specs=[row] * 4,
        out_shape=[sds] * 4, compiler_params=_params("parallel"), name="rope_tables",
    )(posb, inv_rope, inv_ret)


def _rot_matrix():
    half = ROPE_DIM // 2
    p = np.zeros((HEAD_DIM, HEAD_DIM), np.float32)
    for l in range(half):
        p[l + half, l] = -1.0
        p[l, l + half] = 1.0
    return jnp.asarray(p, BF16)


def _rope_partial(x_bf16, c, s, p_ref):
    return x_bf16.astype(F32) * c + _dot(x_bf16, p_ref[...]) * s


def _norm_matmul_kernel(x_ref, g_ref, w_ref, o_ref, xn_ref, *, head_major):
    @pl.when(pl.program_id(1) == 0)
    def _():
        xn_ref[...] = _rms_rows(x_ref[...], g_ref[...]).astype(BF16)

    acc = _dot(xn_ref[...], w_ref[...])
    if head_major:
        for c in range(o_ref.shape[0]):
            o_ref[c] = acc[:, c * LANE:(c + 1) * LANE].astype(o_ref.dtype)
    else:
        o_ref[...] = acc.astype(o_ref.dtype)


def _col_tiles(w, tn):
    k, n = w.shape
    return w.reshape(k, n // tn, tn).transpose(1, 0, 2)


def _norm_matmul(x, g, w, *, tm, tn, head_major=False):
    m, k = x.shape
    n = w.shape[1]
    assert m % tm == 0 and n % tn == 0 and tn % LANE == 0
    if head_major:
        out_shape = jax.ShapeDtypeStruct((n // LANE, m, LANE), BF16)
        out_spec = pl.BlockSpec((tn // LANE, tm, LANE), lambda i, j: (j, i, 0))
    else:
        out_shape = jax.ShapeDtypeStruct((m, n), BF16)
        out_spec = pl.BlockSpec((tm, tn), lambda i, j: (i, j))
    return pl.pallas_call(
        functools.partial(_norm_matmul_kernel, head_major=head_major),
        grid=(m // tm, n // tn),
        in_specs=[pl.BlockSpec((tm, k), lambda i, j: (i, 0)),
                  pl.BlockSpec((1, k), lambda i, j: (0, 0)),
                  pl.BlockSpec((None, k, tn), lambda i, j: (j, 0, 0))],
        out_specs=out_spec, out_shape=out_shape,
        scratch_shapes=[pltpu.VMEM((tm, k), BF16)],
        compiler_params=_params("parallel", "arbitrary"), name="norm_matmul",
    )(x, g.reshape(1, k), _col_tiles(w, tn))


def _matmul_res_kernel(*refs):
    r_ref, o_ref = refs[-2], refs[-1]
    acc = r_ref[...]
    for x_ref, w_ref in zip(refs[:-2:2], refs[1:-2:2]):
        acc = acc + _dot(x_ref[...], w_ref[...])
    o_ref[...] = acc


def _matmul_res(xs, w, res, *, tm, tn):
    m, n = res.shape
    k = xs[0].shape[1]
    assert m % tm == 0 and n % tn == 0 and all(x.shape[1] == k for x in xs) and w.shape[0] == k * len(xs)
    w_t = _col_tiles(w, tn)
    in_specs, args = [], []
    for idx, x in enumerate(xs):
        in_specs += [pl.BlockSpec((tm, k), lambda i, j: (i, 0)),
                     pl.BlockSpec((None, k, tn), lambda i, j, idx=idx: (j, idx, 0))]
        args += [x, w_t]
    return pl.pallas_call(
        _matmul_res_kernel, grid=(m // tm, n // tn),
        in_specs=in_specs + [pl.BlockSpec((tm, tn), lambda i, j: (i, j))],
        out_specs=pl.BlockSpec((tm, tn), lambda i, j: (i, j)),
        out_shape=jax.ShapeDtypeStruct((m, n), F32),
        compiler_params=_params("parallel", "arbitrary"), name="matmul_res",
    )(*args, res)


def _topk_rows(x, n_valid, k, n_rows):
    rows, nq = x.shape
    ridx = lax.broadcasted_iota(jnp.int32, (rows, nq), 0)
    x = jnp.where(ridx < n_valid, x, NEG)
    x = jnp.where(ridx < n_rows, x, -jnp.inf)
    rank = jnp.zeros((rows, nq), F32)
    for i in range(n_rows):
        xi = x[i:i + 1, :]
        rank = rank + jnp.where(ridx > i, jnp.where(xi >= x, 1.0, 0.0), jnp.where(xi > x, 1.0, 0.0))
    marks = jnp.where(rank < float(k), jnp.where(x > NEG / 2, 1.0, 0.0), 0.0)
    if rows < LANE:
        marks = jnp.concatenate([marks, jnp.zeros((LANE - rows, nq), F32)], axis=0)
    return marks.T


def _moba_kernel(q_ref, k_ref, v_ref, c_ref, s_ref, p_ref, o_ref, krot_ref, kmean_ref, *, nb):
    qi = pl.program_id(2)
    blk = MOBA_BLOCK
    nb_pad = -(-nb // SUBLANE) * SUBLANE
    grp = MOBA_GROUP

    @pl.when(qi == 0)
    def _():
        kmean_ref[...] = jnp.zeros_like(kmean_ref)
        for j in range(nb):
            rows = slice(j * blk, (j + 1) * blk)
            kr = _rope_partial(k_ref[rows, :], c_ref[rows, :], s_ref[rows, :], p_ref)
            krot_ref[rows, :] = kr.astype(BF16)
            kmean_ref[j:j + 1, :] = jnp.mean(kr, axis=0, keepdims=True)

    r0 = pl.multiple_of(qi * blk, blk)
    q = _rope_partial(q_ref[...], c_ref[pl.ds(r0, blk), :], s_ref[pl.ds(r0, blk), :], p_ref)
    qb = (q * Q_SCALE).astype(BF16)

    gate_t = lax.dot_general(kmean_ref[...], q, _NT, precision=lax.Precision.HIGHEST,
                             preferred_element_type=F32)[:nb_pad, :]
    sel = _topk_rows(gate_t, qi, min(MOBA_TOPK, nb), nb)
    lane = lax.broadcasted_iota(jnp.int32, (blk, LANE), 1)

    s = _dot_nt(qb, krot_ref[pl.ds(r0, blk), :])
    row = lax.broadcasted_iota(jnp.int32, (blk, blk), 0)
    col = lax.broadcasted_iota(jnp.int32, (blk, blk), 1)
    s = jnp.where(col <= row, s, NEG)
    m = jnp.max(s, axis=-1, keepdims=True)
    p = jnp.exp2(s - m)
    l = jnp.sum(p, axis=-1, keepdims=True)
    acc = _dot(p.astype(BF16), v_ref[pl.ds(r0, blk), :])

    def body(jg, carry):
        m, l, acc = carry
        off = pl.multiple_of(jg * (grp * blk), grp * blk)
        s = _dot_nt(qb, krot_ref[pl.ds(off, grp * blk), :])
        parts = []
        for t in range(grp):
            chosen = jnp.sum(jnp.where(lane == jg * grp + t, sel, 0.0), axis=-1, keepdims=True)
            parts.append(jnp.where(chosen > 0.5, s[:, t * blk:(t + 1) * blk], NEG))
        s = jnp.concatenate(parts, axis=-1)
        m_new = jnp.maximum(m, jnp.max(s, axis=-1, keepdims=True))
        a = jnp.exp2(m - m_new)
        p = jnp.exp2(s - m_new)
        l = a * l + jnp.sum(p, axis=-1, keepdims=True)
        acc = a * acc + _dot(p.astype(BF16), v_ref[pl.ds(off, grp * blk), :])
        return m_new, l, acc

    m, l, acc = lax.fori_loop(0, (qi + grp - 1) // grp, body, (m, l, acc))
    o_ref[...] = (acc / l).astype(o_ref.dtype)


def _moba(proj, cos, sin, rot, batch, seq):
    nb = seq // MOBA_BLOCK
    assert seq % (MOBA_BLOCK * MOBA_GROUP) == 0 and nb <= LANE
    m = batch * seq
    head = lambda off: pl.BlockSpec((None, seq, HEAD_DIM), lambda b, h, i: (off + h, b, 0))
    table = pl.BlockSpec((seq, LANE), lambda b, h, i: (b, 0))
    return pl.pallas_call(
        functools.partial(_moba_kernel, nb=nb),
        grid=(batch, MOBA_HEADS, nb),
        in_specs=[pl.BlockSpec((None, MOBA_BLOCK, HEAD_DIM), lambda b, h, i: (BLK_MQ + h, b * nb + i, 0)),
                  head(BLK_MK), head(BLK_MV), table, table,
                  pl.BlockSpec((HEAD_DIM, HEAD_DIM), lambda b, h, i: (0, 0))],
        out_specs=pl.BlockSpec((MOBA_BLOCK, HEAD_DIM), lambda b, h, i: (b * nb + i, h)),
        out_shape=jax.ShapeDtypeStruct((m, MOBA_HEADS * HEAD_DIM), BF16),
        scratch_shapes=[pltpu.VMEM((seq, HEAD_DIM), BF16), pltpu.VMEM((LANE, HEAD_DIM), F32)],
        compiler_params=_params("parallel", "parallel", "arbitrary"), name="moba",
    )(proj, proj, proj, cos, sin, rot)


def _compress_kernel(x_ref, pe_ref, w1_ref, w2_ref, o_ref, xf_ref):
    seq = x_ref.shape[0]
    n_sub = seq // NSA_CMP_STRIDE
    xf_ref[:seq, :] = x_ref[...].astype(F32)
    xf_ref[seq:, :] = jnp.zeros((NSA_CMP_STRIDE, HEAD_DIM), F32)
    hid = _dot(pe_ref[...], w1_ref[...])[0:1, :]
    for r in range(NSA_CMP_LEN):
        rows = xf_ref[pl.ds(r, n_sub, stride=NSA_CMP_STRIDE), :]
        hid = hid + _dot(rows.astype(BF16), w1_ref[r * HEAD_DIM:(r + 1) * HEAD_DIM, :])
    hid = hid * _sigmoid(hid)
    o_ref[...] = _dot(hid.astype(BF16), w2_ref[...]).astype(o_ref.dtype)


def _compress(proj, pe, w1, w2, batch, seq):
    n_sub = seq // NSA_CMP_STRIDE
    flat = NSA_CMP_LEN * HEAD_DIM
    pe_flat = jnp.zeros((2, SUBLANE * 2, flat), BF16).at[:, 0].set(pe.reshape(2, flat).astype(BF16))
    return pl.pallas_call(
        _compress_kernel, grid=(2, batch, NSA_GROUPS),
        in_specs=[pl.BlockSpec((None, seq, HEAD_DIM), lambda t, b, g: (BLK_NKC + 2 * t + g, b, 0)),
                  pl.BlockSpec((None, SUBLANE * 2, flat), lambda t, b, g: (t, 0, 0)),
                  pl.BlockSpec((None, flat, HEAD_DIM), lambda t, b, g: (t, 0, 0)),
                  pl.BlockSpec((None, HEAD_DIM, HEAD_DIM), lambda t, b, g: (t, 0, 0))],
        out_specs=pl.BlockSpec((None, None, None, n_sub, HEAD_DIM), lambda t, b, g: (t, b, g, 0, 0)),
        out_shape=jax.ShapeDtypeStruct((2, batch, NSA_GROUPS, n_sub, HEAD_DIM), BF16),
        scratch_shapes=[pltpu.VMEM((seq + NSA_CMP_STRIDE, HEAD_DIM), F32)],
        compiler_params=_params("parallel", "parallel", "parallel"), name="nsa_compress",
    )(proj, pe_flat, w1.astype(BF16), w2.astype(BF16))


def _nsa_cmp_kernel(q_ref, kc_ref, vc_ref, cov_ref, oc_ref, sel_ref, *, n_sel):
    qi = pl.program_id(2)
    tq = ATT_TQ
    n_cmp = kc_ref.shape[0]
    scale = HEAD_DIM ** -0.5
    q_pos = qi * tq + lax.broadcasted_iota(jnp.int32, (tq, n_cmp), 0)
    n_idx = lax.broadcasted_iota(jnp.int32, (tq, n_cmp), 1)
    ok = n_idx * NSA_CMP_STRIDE + (NSA_CMP_LEN - 1) <= q_pos
    kc = kc_ref[...]
    vc = vc_ref[...]
    p_sum = jnp.zeros((tq, n_cmp), F32)
    for r in range(NSA_REP):
        s = jnp.where(ok, _dot_nt(q_ref[r], kc) * scale, NEG)
        mx = jnp.max(s, axis=-1, keepdims=True)
        e = jnp.where(ok, jnp.exp(s - mx), 0.0)
        l = jnp.sum(e, axis=-1, keepdims=True)
        p = e * jnp.where(l > 0.0, 1.0 / l, 0.0)
        oc_ref[r] = _dot(p.astype(BF16), vc).astype(oc_ref.dtype)
        p_sum = p_sum + p

    n_rows = -(-n_sel // SUBLANE) * SUBLANE
    imp = lax.dot_general(cov_ref[...], p_sum, _NT, precision=lax.Precision.HIGHEST,
                          preferred_element_type=F32)[:n_rows, :]
    ridx = lax.broadcasted_iota(jnp.int32, (n_rows, tq), 0)
    blk = (qi * tq + lax.broadcasted_iota(jnp.int32, (n_rows, tq), 1)) // NSA_SEL_LEN
    forced = (ridx == 0) | (ridx == blk) | (ridx == blk - 1)
    x = jnp.where(forced, FORCE, imp)
    sel_ref[...] = _topk_rows(x, blk[0:1, :] + 1, min(NSA_TOPK, n_sel), n_sel).astype(sel_ref.dtype)


def _cover_matrix(n_cmp_pad, n_cmp, n_sel):
    c_start = np.arange(n_cmp_pad) * NSA_CMP_STRIDE
    s_start = np.arange(LANE) * NSA_SEL_LEN
    cover = (c_start[:, None] < s_start[None, :] + NSA_SEL_LEN) & (c_start[:, None] + NSA_CMP_LEN > s_start[None, :])
    cover &= (np.arange(n_cmp_pad) < n_cmp)[:, None] & (np.arange(LANE) < n_sel)[None, :]
    return jnp.asarray(cover.T, F32)


def _nsa_cmp(proj, kvc, batch, seq):
    nq = seq // ATT_TQ
    n_sub = seq // NSA_CMP_STRIDE
    n_sel = seq // NSA_SEL_LEN
    assert n_sel <= LANE
    m = batch * seq
    cover = _cover_matrix(n_sub, n_sub - NSA_CMP_LEN // NSA_CMP_STRIDE + 1, n_sel)
    cmp_spec = lambda t: pl.BlockSpec((None, None, None, n_sub, HEAD_DIM), lambda b, g, i: (t, b, g, 0, 0))
    return pl.pallas_call(
        functools.partial(_nsa_cmp_kernel, n_sel=n_sel),
        grid=(batch, NSA_GROUPS, nq),
        in_specs=[pl.BlockSpec((NSA_REP, ATT_TQ, HEAD_DIM), lambda b, g, i: (BLK_NQ // NSA_REP + g, b * nq + i, 0)),
                  cmp_spec(0), cmp_spec(1),
                  pl.BlockSpec((LANE, n_sub), lambda b, g, i: (0, 0))],
        out_specs=[pl.BlockSpec((NSA_REP, ATT_TQ, HEAD_DIM), lambda b, g, i: (g, b * nq + i, 0)),
                   pl.BlockSpec((ATT_TQ, LANE), lambda b, g, i: (b * nq + i, g))],
        out_shape=[jax.ShapeDtypeStruct((NSA_HEADS, m, HEAD_DIM), BF16),
                   jax.ShapeDtypeStruct((m, NSA_GROUPS * LANE), BF16)],
        compiler_params=_params("parallel", "parallel", "parallel"), name="nsa_cmp",
    )(proj, kvc, kvc, cover)


def _nsa_sw_kernel(q_ref, ks_ref, vs_ref, kw_ref, vw_ref, sel_ref, oc_ref, gl_ref, c_ref, s_ref, p_ref,
                   o_ref, ksr_ref, kwr_ref, vwp_ref, qr_ref, m_ref, l_ref, acc_ref, ow_ref, *, seq):
    grp = pl.program_id(1)
    qi = pl.program_id(2)
    tq, tk = ATT_TQ, ATT_TK
    rep = NSA_REP
    win = NSA_WINDOW
    gk = SEL_GROUP * tk

    @pl.when(qi == 0)
    def _():
        kwr_ref[:win, :] = jnp.zeros((win, HEAD_DIM), BF16)
        vwp_ref[:win, :] = jnp.zeros((win, HEAD_DIM), BF16)
        for j in range(seq // tk):
            rows = slice(j * tk, (j + 1) * tk)
            shifted = slice(win + j * tk, win + (j + 1) * tk)
            c, s = c_ref[rows, :], s_ref[rows, :]
            ksr_ref[rows, :] = _rope_partial(ks_ref[rows, :], c, s, p_ref).astype(BF16)
            kwr_ref[shifted, :] = _rope_partial(kw_ref[rows, :], c, s, p_ref).astype(BF16)
            vwp_ref[shifted, :] = vw_ref[rows, :]

    r0 = pl.multiple_of(qi * tq, tq)
    cq = c_ref[pl.ds(r0, tq), :]
    sq = s_ref[pl.ds(r0, tq), :]
    for r in range(rep):
        qr_ref[r] = (_rope_partial(q_ref[r], cq, sq, p_ref) * Q_SCALE).astype(BF16)

    q_all = qr_ref[...].reshape(rep * tq, HEAD_DIM)

    row_w = lax.broadcasted_iota(jnp.int32, (tq, win + tq), 0)
    col_w = lax.broadcasted_iota(jnp.int32, (tq, win + tq), 1)
    in_win = (col_w > row_w) & (col_w <= row_w + win) & (col_w >= win - r0)
    s = _dot_nt(q_all, kwr_ref[pl.ds(r0, win + tq), :]).reshape(rep, tq, win + tq)
    s = jnp.where(in_win[None], s, NEG)
    p = jnp.exp2(s - jnp.max(s, axis=-1, keepdims=True))
    l = jnp.sum(p, axis=-1, keepdims=True)
    o_w = _dot(p.astype(BF16).reshape(rep * tq, win + tq), vwp_ref[pl.ds(r0, win + tq), :])
    ow_ref[...] = o_w.reshape(rep, tq, HEAD_DIM) / l

    sel = sel_ref[...]
    row = lax.broadcasted_iota(jnp.int32, (tq, gk), 0)
    col = lax.broadcasted_iota(jnp.int32, (tq, gk), 1)

    def chosen_keys(jg):
        blk_row = lax.broadcasted_iota(jnp.int32, (LANE, gk), 0)
        blk_col = jg * (gk // NSA_SEL_LEN) + lax.broadcasted_iota(jnp.int32, (LANE, gk), 1) // NSA_SEL_LEN
        return _dot(sel, jnp.where(blk_row == blk_col, 1.0, 0.0).astype(BF16))

    def attend(off, valid, first):
        s = _dot_nt(q_all, ksr_ref[pl.ds(off, gk), :]).reshape(rep, tq, gk)
        s = jnp.where(valid[None], s, NEG)
        m_new = jnp.max(s, axis=-1, keepdims=True)
        if not first:
            m_old = m_ref[...]
            m_new = jnp.maximum(m_old, m_new)
        p = jnp.exp2(s - m_new)
        l = jnp.sum(p, axis=-1, keepdims=True)
        pv = _dot(p.astype(BF16).reshape(rep * tq, gk), vs_ref[pl.ds(off, gk), :]).reshape(rep, tq, HEAD_DIM)
        if first:
            l_ref[...] = l
            acc_ref[...] = pv
        else:
            a = jnp.exp2(m_old - m_new)
            l_ref[...] = a * l_ref[...] + l
            acc_ref[...] = a * acc_ref[...] + pv
        m_ref[...] = m_new

    gd = qi // SEL_GROUP
    off_d = pl.multiple_of(gd * gk, gk)
    causal = col + off_d <= row + r0
    attend(off_d, jnp.where(causal, chosen_keys(gd), 0.0) > 0.5, True)

    def sel_body(jg, carry):
        attend(pl.multiple_of(jg * gk, gk), chosen_keys(jg) > 0.5, False)
        return carry

    lax.fori_loop(0, gd, sel_body, 0)

    gates = _sigmoid(gl_ref[...].astype(F32))
    lane = lax.broadcasted_iota(jnp.int32, (tq, LANE), 1)

    def gate_col(c):
        return jnp.sum(jnp.where(lane == c, gates, 0.0), axis=-1, keepdims=True)

    for r in range(rep):
        base = (grp * rep + r) * 3
        o_s = acc_ref[r] / l_ref[r]
        o = gate_col(base) * oc_ref[r].astype(F32) + gate_col(base + 1) * o_s + gate_col(base + 2) * ow_ref[r]
        o_ref[:, r * HEAD_DIM:(r + 1) * HEAD_DIM] = o.astype(o_ref.dtype)


def _nsa_sw(proj, sel, o_cmp, cos, sin, rot, batch, seq):
    nq = seq // ATT_TQ
    assert seq % (SEL_GROUP * ATT_TK) == 0 and NSA_WINDOW % ATT_TK == 0 and ATT_TQ == ATT_TK
    m = batch * seq
    rep = NSA_REP
    head = lambda off: pl.BlockSpec((None, seq, HEAD_DIM), lambda b, g, i: (off + g, b, 0))
    table = pl.BlockSpec((seq, LANE), lambda b, g, i: (b, 0))
    q_like = lambda off: pl.BlockSpec((rep, ATT_TQ, HEAD_DIM), lambda b, g, i: (off + g, b * nq + i, 0))
    return pl.pallas_call(
        functools.partial(_nsa_sw_kernel, seq=seq),
        grid=(batch, NSA_GROUPS, nq),
        in_specs=[q_like(BLK_NQ // rep), head(BLK_NKS), head(BLK_NVS), head(BLK_NKW), head(BLK_NVW),
                  pl.BlockSpec((ATT_TQ, LANE), lambda b, g, i: (b * nq + i, g)),
                  q_like(0),
                  pl.BlockSpec((None, ATT_TQ, LANE), lambda b, g, i: (BLK_NG, b * nq + i, 0)),
                  table, table,
                  pl.BlockSpec((HEAD_DIM, HEAD_DIM), lambda b, g, i: (0, 0))],
        out_specs=pl.BlockSpec((ATT_TQ, rep * HEAD_DIM), lambda b, g, i: (b * nq + i, g)),
        out_shape=jax.ShapeDtypeStruct((m, NSA_HEADS * HEAD_DIM), BF16),
        scratch_shapes=[pltpu.VMEM((seq, HEAD_DIM), BF16),
                        pltpu.VMEM((seq + NSA_WINDOW, HEAD_DIM), BF16),
                        pltpu.VMEM((seq + NSA_WINDOW, HEAD_DIM), BF16),
                        pltpu.VMEM((rep, ATT_TQ, HEAD_DIM), BF16),
                        pltpu.VMEM((rep, ATT_TQ, 1), F32), pltpu.VMEM((rep, ATT_TQ, 1), F32),
                        pltpu.VMEM((rep, ATT_TQ, HEAD_DIM), F32), pltpu.VMEM((rep, ATT_TQ, HEAD_DIM), F32)],
        compiler_params=_params("parallel", "parallel", "arbitrary"), name="nsa_sel_win",
    )(proj, proj, proj, proj, proj, sel, o_cmp, proj, cos, sin, rot)


def _rope_full(x_bf16, c, s):
    half = x_bf16.shape[-1] // 2
    x = x_bf16.astype(F32)
    x1, x2 = x[:, :half], x[:, half:]
    return jnp.concatenate([x1 * c - x2 * s, x1 * s + x2 * c], axis=-1)


def _retention_kernel(q_ref, k_ref, v_ref, g_ref, c_ref, s_ref, lg_ref, gn_ref, o_ref, state_ref):
    ci = pl.program_id(1)
    ch = q_ref.shape[0]

    @pl.when(ci == 0)
    def _():
        state_ref[...] = jnp.zeros_like(state_ref)

    c, s = c_ref[...], s_ref[...]
    n_row = lax.broadcasted_iota(jnp.int32, (ch, ch), 0)
    n_col = lax.broadcasted_iota(jnp.int32, (ch, ch), 1)
    diff = (n_row - n_col).astype(F32)
    n_vec = lax.broadcasted_iota(jnp.int32, (ch, 1), 0).astype(F32)

    for h in range(RET_HEADS):
        log_g = lg_ref[h][:, 0:1]
        qk_cols = slice(h * RET_DK, (h + 1) * RET_DK)
        v_cols = slice(h * RET_DV, (h + 1) * RET_DV)
        q = _rope_full(q_ref[:, qk_cols], c, s)
        k = _rope_full(k_ref[:, qk_cols], c, s) * (RET_DK ** -0.5)
        v = v_ref[:, v_cols]

        decay = jnp.where(diff >= 0.0, jnp.exp(jnp.maximum(diff, 0.0) * log_g), 0.0)
        q_dec = jnp.exp((n_vec + 1.0) * log_g)
        k_dec = jnp.exp((ch - 1.0 - n_vec) * log_g)
        c_dec = jnp.exp(ch * log_g)

        qb = q.astype(BF16)
        state = state_ref[h]
        scores = _dot_nt(qb, k.astype(BF16)) * decay
        y = _dot(scores.astype(BF16), v) + _dot(qb, state.astype(BF16)) * q_dec
        state_ref[h] = state * c_dec + lax.dot_general((k * k_dec).astype(BF16), v, _TN,
                                                       preferred_element_type=F32)

        mu = jnp.mean(y, axis=-1, keepdims=True)
        yc = y - mu
        var = jnp.mean(yc * yc, axis=-1, keepdims=True)
        yn = yc * lax.rsqrt(var + RMS_EPS) * gn_ref[h]
        gate = g_ref[:, v_cols].astype(F32)
        o_ref[:, v_cols] = (gate * _sigmoid(gate) * yn).astype(o_ref.dtype)


def _retention(proj, cos_r, sin_r, gn, batch, seq):
    ch = RET_CHUNK
    nch = seq // ch
    assert seq % ch == 0
    m = batch * seq
    log_g = jnp.log(1.0 - jnp.exp2(-5.0 - jnp.arange(RET_HEADS, dtype=F32)))
    log_g = jnp.broadcast_to(log_g[:, None, None], (RET_HEADS, 1, LANE))
    qk_w = RET_HEADS * RET_DK
    v_w = RET_HEADS * RET_DV
    assert v_w == 2 * qk_w
    rows = lambda b, c: b * nch + c
    return pl.pallas_call(
        _retention_kernel, grid=(batch, nch),
        in_specs=[pl.BlockSpec((ch, qk_w), lambda b, c: (rows(b, c), 0)),
                  pl.BlockSpec((ch, qk_w), lambda b, c: (rows(b, c), 1)),
                  pl.BlockSpec((ch, v_w), lambda b, c: (rows(b, c), 1)),
                  pl.BlockSpec((ch, v_w), lambda b, c: (rows(b, c), 2)),
                  pl.BlockSpec((ch, LANE), lambda b, c: (rows(b, c), 0)),
                  pl.BlockSpec((ch, LANE), lambda b, c: (rows(b, c), 0)),
                  pl.BlockSpec((RET_HEADS, 1, LANE), lambda b, c: (0, 0, 0)),
                  pl.BlockSpec((RET_HEADS, 1, RET_DV), lambda b, c: (0, 0, 0))],
        out_specs=pl.BlockSpec((ch, v_w), lambda b, c: (rows(b, c), 0)),
        out_shape=jax.ShapeDtypeStruct((m, v_w), BF16),
        scratch_shapes=[pltpu.VMEM((RET_HEADS, RET_DK, RET_DV), F32)],
        compiler_params=_params("parallel", "arbitrary"), name="retention",
    )(proj, proj, proj, proj, cos_r, sin_r, log_g, gn.reshape(RET_HEADS, 1, RET_DV).astype(F32))


def _cross_kernel(h_ref, g_ref, wq_ref, kv_ref, wo_ref, o_ref):
    h = h_ref[...]
    hn = _rms_rows(h, g_ref[...]).astype(BF16)
    q = (_dot(hn, wq_ref[...]) * Q_SCALE).astype(BF16)
    kv_cols = X_HEADS * HEAD_DIM
    outs = []
    for hd in range(X_HEADS):
        cols = slice(hd * HEAD_DIM, (hd + 1) * HEAD_DIM)
        s = _dot_nt(q[:, cols], kv_ref[:, cols])
        mx = jnp.max(s, axis=-1, keepdims=True)
        p = jnp.exp2(s - mx)
        l = jnp.sum(p, axis=-1, keepdims=True)
        p = p / l
        outs.append(_dot(p.astype(BF16), kv_ref[:, kv_cols + hd * HEAD_DIM:kv_cols + (hd + 1) * HEAD_DIM]))
    o = jnp.concatenate(outs, axis=-1).astype(BF16)
    o_ref[...] = h + _dot(o, wo_ref[...])


def _cross(h, g, wq, kv, wo, layer, batch, seq, n_mem, *, tm):
    m, d = h.shape
    per_b = seq // tm
    kv_cols = 2 * X_HEADS * HEAD_DIM
    return pl.pallas_call(
        _cross_kernel, grid=(m // tm,),
        in_specs=[pl.BlockSpec((tm, d), lambda i: (i, 0)),
                  pl.BlockSpec((1, d), lambda i: (0, 0)),
                  pl.BlockSpec((d, X_HEADS * HEAD_DIM), lambda i: (0, 0)),
                  pl.BlockSpec((n_mem, kv_cols), lambda i: (i // per_b, layer)),
                  pl.BlockSpec((X_HEADS * HEAD_DIM, d), lambda i: (0, 0))],
        out_specs=pl.BlockSpec((tm, d), lambda i: (i, 0)),
        out_shape=jax.ShapeDtypeStruct((m, d), F32),
        compiler_params=_params("parallel"), name="cross_attn",
    )(h, g.reshape(1, d), wq, kv, wo)


def _ffn_kernel(h_ref, halo_ref, g_ref, wg_ref, wv_ref, cwg_ref, cwv_ref, cbg_ref, cbv_ref, wd_ref,
                o_ref, hn_ref, acc_ref, *, per_b):
    i = pl.program_id(0)
    f = pl.program_id(1)
    tm = h_ref.shape[0]
    pad = SUBLANE * 2

    @pl.when(f == 0)
    def _():
        g = g_ref[...]
        hn_ref[pad:, :] = _rms_rows(h_ref[...], g).astype(BF16)
        halo = _rms_rows(halo_ref[...], g)
        halo = jnp.where(i % per_b == 0, 0.0, halo)
        hn_ref[:pad, :] = halo.astype(BF16)
        acc_ref[...] = jnp.zeros_like(acc_ref)

    hn = hn_ref[...]

    def conv(u, w, b):
        return (b + w[0:1, :] * u[pad - 2:pad - 2 + tm] + w[1:2, :] * u[pad - 1:pad - 1 + tm]
                + w[2:3, :] * u[pad:pad + tm])

    gate = conv(_dot(hn, wg_ref[...]), cwg_ref[...], cbg_ref[...])
    val = conv(_dot(hn, wv_ref[...]), cwv_ref[...], cbv_ref[...])
    act = (gate * _sigmoid(gate) * val).astype(BF16)
    acc_ref[...] += _dot(act, wd_ref[...])

    @pl.when(f == pl.num_programs(1) - 1)
    def _():
        o_ref[...] = h_ref[...] + acc_ref[...]


def _ffn(h, g, w_up, conv_w, conv_b, w_down, batch, seq, *, tm, tf):
    m, d = h.shape
    dff = w_down.shape[0]
    assert seq % tm == 0 and dff % tf == 0
    per_b = seq // tm
    nf = dff // tf
    pad = SUBLANE * 2
    halo_blocks = tm // pad
    cw = jnp.zeros((SUBLANE, 2 * dff), F32).at[:CONV_WIDTH].set(conv_w)
    cb = conv_b.reshape(1, 2 * dff)
    w_up_t = _col_tiles(w_up, tf)
    return pl.pallas_call(
        functools.partial(_ffn_kernel, per_b=per_b),
        grid=(m // tm, nf),
        in_specs=[pl.BlockSpec((tm, d), lambda i, f: (i, 0)),
                  pl.BlockSpec((pad, d), lambda i, f: (jnp.maximum(i * halo_blocks - 1, 0), 0)),
                  pl.BlockSpec((1, d), lambda i, f: (0, 0)),
                  pl.BlockSpec((None, d, tf), lambda i, f: (f, 0, 0)),
                  pl.BlockSpec((None, d, tf), lambda i, f: (nf + f, 0, 0)),
                  pl.BlockSpec((SUBLANE, tf), lambda i, f: (0, f)),
                  pl.BlockSpec((SUBLANE, tf), lambda i, f: (0, nf + f)),
                  pl.BlockSpec((1, tf), lambda i, f: (0, f)),
                  pl.BlockSpec((1, tf), lambda i, f: (0, nf + f)),
                  pl.BlockSpec((tf, d), lambda i, f: (f, 0))],
        out_specs=pl.BlockSpec((tm, d), lambda i, f: (i, 0)),
        out_shape=jax.ShapeDtypeStruct((m, d), F32),
        scratch_shapes=[pltpu.VMEM((tm + pad, d), BF16), pltpu.VMEM((tm, d), F32)],
        compiler_params=_params("parallel", "arbitrary"), name="conv_ffn",
    )(h, h, g.reshape(1, d), w_up_t, w_up_t, cw, cw, cb, cb, w_down)


def _final_norm_kernel(x_ref, g_ref, o_ref):
    o_ref[...] = _rms_rows(x_ref[...], g_ref[...])


def _final_norm(x, g, *, tm):
    m, d = x.shape
    return pl.pallas_call(
        _final_norm_kernel, grid=(m // tm,),
        in_specs=[pl.BlockSpec((tm, d), lambda i: (i, 0)), pl.BlockSpec((1, d), lambda i: (0, 0))],
        out_specs=pl.BlockSpec((tm, d), lambda i: (i, 0)),
        out_shape=jax.ShapeDtypeStruct((m, d), F32),
        compiler_params=_params("parallel"), name="final_norm",
    )(x, g.reshape(1, d))


def _row_tile(m, want):
    return want if m % want == 0 else m


def _mixer_ab(h, g, w_in, pe, w1, w2, w_out, tabs, rot, batch, seq):
    m, d = h.shape
    cos, sin = tabs[0], tabs[1]
    w_pad = jnp.zeros((d, AB_BLOCKS * LANE), BF16).at[:, :AB_COLS].set(w_in.astype(BF16))
    proj = _norm_matmul(h, g, w_pad, tm=_row_tile(m, 1024), tn=1024, head_major=True)
    o_moba = _moba(proj, cos, sin, rot, batch, seq)
    kvc = _compress(proj, pe, w1, w2, batch, seq)
    o_cmp, sel = _nsa_cmp(proj, kvc, batch, seq)
    o_nsa = _nsa_sw(proj, sel, o_cmp, cos, sin, rot, batch, seq)
    return _matmul_res([o_moba, o_nsa], w_out.astype(BF16), h, tm=_row_tile(m, 1024), tn=512)


def _mixer_c(h, g, w_in, gn, w_out, tabs, batch, seq):
    m, d = h.shape
    proj = _norm_matmul(h, g, w_in.astype(BF16), tm=_row_tile(m, 1024), tn=1024)
    y = _retention(proj, tabs[2], tabs[3], gn, batch, seq)
    return _matmul_res([y], w_out.astype(BF16), h, tm=_row_tile(m, 1024), tn=512)


def kernel(x, mem, positions, norm_mix, norm_cross, norm_ffn, norm_mem, norm_final, w_in_ab, cmp_pe_k, cmp_w1_k,
           cmp_w2_k, cmp_pe_v, cmp_w1_v, cmp_w2_v, w_out_ab, w_in_c, ret_gn, w_out_c, w_q_x, w_kv_x, w_o_x, w_up,
           conv_w, conv_b, w_down):
    batch, seq, d = x.shape
    n_mem = mem.shape[1]
    depth = norm_mix.shape[0]
    m = batch * seq
    tabs = _rope_tables(positions)
    rot = _rot_matrix()

    kv_cols = 2 * X_HEADS * HEAD_DIM
    w_kv = jnp.transpose(w_kv_x, (1, 0, 2)).reshape(d, depth * kv_cols).astype(BF16)
    kv = _norm_matmul(mem.reshape(batch * n_mem, d), norm_mem, w_kv, tm=_row_tile(batch * n_mem, 512), tn=512)

    h = x.reshape(m, d)
    for l in range(depth):
        if l % 2 == 0:
            e = l // 2
            h = _mixer_ab(h, norm_mix[l], w_in_ab[e],
                          jnp.stack([cmp_pe_k[e], cmp_pe_v[e]]), jnp.stack([cmp_w1_k[e], cmp_w1_v[e]]),
                          jnp.stack([cmp_w2_k[e], cmp_w2_v[e]]), w_out_ab[e], tabs, rot, batch, seq)
        else:
            o = l // 2
            h = _mixer_c(h, norm_mix[l], w_in_c[o], ret_gn[o], w_out_c[o], tabs, batch, seq)
        h = _cross(h, norm_cross[l], w_q_x[l].astype(BF16), kv, w_o_x[l].astype(BF16), l, batch, seq, n_mem,
                   tm=_row_tile(seq, 256))
        h = _ffn(h, norm_ffn[l], w_up[l].astype(BF16), conv_w[l], conv_b[l], w_down[l].astype(BF16), batch, seq,
                 tm=_row_tile(seq, 512), tf=512)
    return _final_norm(h, norm_final, tm=_row_tile(m, 512)).reshape(batch, seq, d)
```

```python
import functools

import numpy as np
import jax
import jax.numpy as jnp
from jax import lax
from jax.experimental import pallas as pl
from jax.experimental.pallas import tpu as pltpu

F32 = jnp.float32
BF16 = jnp.bfloat16

D_MODEL = 2048
DEPTH = 4
HEAD_DIM = 128
ROPE_THETA = 500000.0
ROPE_DIM = HEAD_DIM // 4
X_HEADS = 4
MOBA_HEADS = 8
MOBA_BLOCK = 256
MOBA_TOPK = 3
NSA_HEADS = 8
NSA_GROUPS = 2
NSA_REP = NSA_HEADS // NSA_GROUPS
NSA_CMP_LEN = 32
NSA_CMP_STRIDE = 16
NSA_SEL_LEN = 64
NSA_TOPK = 16
NSA_WINDOW = 512
RET_HEADS = 8
RET_DK = 256
RET_DV = 512
RET_THETA = 10000.0
D_FF = 5632
CONV_WIDTH = 3
RMS_EPS = 1e-6
NEG = -1e30
FORCE = 1e9
LOG2E = 1.4426950408889634
Q_SCALE = HEAD_DIM ** -0.5 * LOG2E

LANE = 128
SUBLANE = 8
VMEM_LIMIT = 56 * 2 ** 20

AB_SIZES = (MOBA_HEADS * HEAD_DIM,) * 3 + (NSA_HEADS * HEAD_DIM,) + (NSA_GROUPS * HEAD_DIM,) * 6 + (NSA_HEADS * 3,)
AB_COLS = sum(AB_SIZES)
AB_BLOCKS = 48
BLK_MQ, BLK_MK, BLK_MV, BLK_NQ = 0, 8, 16, 24
BLK_NKC, BLK_NVC, BLK_NKS, BLK_NVS, BLK_NKW, BLK_NVW, BLK_NG = 32, 34, 36, 38, 40, 42, 44

ATT_TQ = 256
ATT_TK = 256
RET_CHUNK = 128
MOBA_GROUP = 4
MOBA_HB = 4
SEL_GROUP = 4

_NT = (((1,), (1,)), ((), ()))
_TN = (((0,), (0,)), ((), ()))


def _params(*sem):
    return pltpu.CompilerParams(dimension_semantics=sem, vmem_limit_bytes=VMEM_LIMIT)


def _dot(a, b):
    return jnp.dot(a, b, preferred_element_type=F32)


def _dot_nt(a, b):
    return lax.dot_general(a, b, _NT, preferred_element_type=F32)


def _sigmoid(x):
    return 1.0 / (1.0 + jnp.exp(-x))


def _rms_rows(x, g):
    ms = jnp.mean(x * x, axis=-1, keepdims=True)
    return x * lax.rsqrt(ms + RMS_EPS) * g


def _tables_kernel(pos_ref, inv_rope_ref, inv_ret_ref, c_ref, s_ref, cr_ref, sr_ref):
    pos = pos_ref[...]
    lane = lax.broadcasted_iota(jnp.int32, pos.shape, 1)
    rot = lane < ROPE_DIM
    ang = pos * inv_rope_ref[...]
    c_ref[...] = jnp.where(rot, jnp.cos(ang), 1.0)
    s_ref[...] = jnp.where(rot, jnp.sin(ang), 0.0)
    ang_r = pos * inv_ret_ref[...]
    cr_ref[...] = jnp.cos(ang_r)
    sr_ref[...] = jnp.sin(ang_r)


def _rope_tables(positions):
    m = positions.size
    posb = jnp.broadcast_to(positions.reshape(m, 1).astype(F32), (m, LANE))
    half = ROPE_DIM // 2
    inv = jnp.float32(ROPE_THETA) ** (-jnp.arange(half, dtype=F32) / half)
    inv_rope = jnp.concatenate([inv, inv, jnp.zeros((LANE - ROPE_DIM,), F32)]).reshape(1, LANE)
    half_r = RET_DK // 2
    inv_ret = (jnp.float32(RET_THETA) ** (-jnp.arange(half_r, dtype=F32) / half_r)).reshape(1, LANE)
    tm = 1024 if m % 1024 == 0 else m
    row = pl.BlockSpec((tm, LANE), lambda i: (i, 0))
    vec = pl.BlockSpec((1, LANE), lambda i: (0, 0))
    sds = jax.ShapeDtypeStruct((m, LANE), F32)
    return pl.pallas_call(
        _tables_kernel, grid=(m // tm,), in_specs=[row, vec, vec], out_specs=[row] * 4,
        out_shape=[sds] * 4, compiler_params=_params("parallel"), name="rope_tables",
    )(posb, inv_rope, inv_ret)


def _rot_matrix():
    half = ROPE_DIM // 2
    p = np.zeros((HEAD_DIM, HEAD_DIM), np.float32)
    for l in range(half):
        p[l + half, l] = -1.0
        p[l, l + half] = 1.0
    return jnp.asarray(p, BF16)


def _rope_partial(x_bf16, c, s, p_ref):
    return x_bf16.astype(F32) * c + _dot(x_bf16, p_ref[...]) * s


def _norm_matmul_kernel(x_ref, g_ref, w_ref, o_ref, xn_ref, *, head_major):
    @pl.when(pl.program_id(1) == 0)
    def _():
        xn_ref[...] = _rms_rows(x_ref[...], g_ref[...]).astype(BF16)

    acc = _dot(xn_ref[...], w_ref[...])
    if head_major:
        for c in range(o_ref.shape[0]):
            o_ref[c] = acc[:, c * LANE:(c + 1) * LANE].astype(o_ref.dtype)
    else:
        o_ref[...] = acc.astype(o_ref.dtype)


def _norm_matmul(x, g, w, *, tm, tn, head_major=False):
    m, k = x.shape
    n = w.shape[1]
    assert m % tm == 0 and n % tn == 0 and tn % LANE == 0
    if head_major:
        out_shape = jax.ShapeDtypeStruct((n // LANE, m, LANE), BF16)
        out_spec = pl.BlockSpec((tn // LANE, tm, LANE), lambda i, j: (j, i, 0))
    else:
        out_shape = jax.ShapeDtypeStruct((m, n), BF16)
        out_spec = pl.BlockSpec((tm, tn), lambda i, j: (i, j))
    return pl.pallas_call(
        functools.partial(_norm_matmul_kernel, head_major=head_major),
        grid=(m // tm, n // tn),
        in_specs=[pl.BlockSpec((tm, k), lambda i, j: (i, 0)),
                  pl.BlockSpec((1, k), lambda i, j: (0, 0)),
                  pl.BlockSpec((k, tn), lambda i, j: (0, j))],
        out_specs=out_spec, out_shape=out_shape,
        scratch_shapes=[pltpu.VMEM((tm, k), BF16)],
        compiler_params=_params("parallel", "arbitrary"), name="norm_matmul",
    )(x, g.reshape(1, k), w)


def _matmul_res_kernel(*refs):
    r_ref, o_ref = refs[-2], refs[-1]
    acc = r_ref[...]
    for x_ref, w_ref in zip(refs[:-2:2], refs[1:-2:2]):
        acc = acc + _dot(x_ref[...], w_ref[...])
    o_ref[...] = acc


def _matmul_res(xs, w, res, *, tm, tn):
    m, n = res.shape
    k = xs[0].shape[1]
    assert m % tm == 0 and n % tn == 0 and all(x.shape[1] == k for x in xs) and w.shape[0] == k * len(xs)
    in_specs, args = [], []
    for idx, x in enumerate(xs):
        in_specs += [pl.BlockSpec((tm, k), lambda i, j: (i, 0)),
                     pl.BlockSpec((k, tn), lambda i, j, idx=idx: (idx, j))]
        args += [x, w]
    return pl.pallas_call(
        _matmul_res_kernel, grid=(m // tm, n // tn),
        in_specs=in_specs + [pl.BlockSpec((tm, tn), lambda i, j: (i, j))],
        out_specs=pl.BlockSpec((tm, tn), lambda i, j: (i, j)),
        out_shape=jax.ShapeDtypeStruct((m, n), F32),
        compiler_params=_params("parallel", "arbitrary"), name="matmul_res",
    )(*args, res)


def _topk_rows(x, n_valid, k, n_rows):
    rows, nq = x.shape
    ridx = lax.broadcasted_iota(jnp.int32, (rows, nq), 0)
    x = jnp.where(ridx < n_valid, x, NEG)
    x = jnp.where(ridx < n_rows, x, -jnp.inf)
    rank = jnp.zeros((rows, nq), F32)
    for i in range(n_rows):
        xi = x[i:i + 1, :]
        rank = rank + jnp.where(ridx > i, jnp.where(xi >= x, 1.0, 0.0), jnp.where(xi > x, 1.0, 0.0))
    marks = jnp.where(rank < float(k), jnp.where(x > NEG / 2, 1.0, 0.0), 0.0)
    if rows < LANE:
        marks = jnp.concatenate([marks, jnp.zeros((LANE - rows, nq), F32)], axis=0)
    return marks.T


def _moba_kernel(q_ref, k_ref, v_ref, c_ref, s_ref, p_ref, o_ref, krot_ref, kmean_ref, *, nb):
    qi = pl.program_id(2)
    blk = MOBA_BLOCK
    nb_pad = -(-nb // SUBLANE) * SUBLANE
    grp = MOBA_GROUP
    heads = range(MOBA_HB)

    @pl.when(qi == 0)
    def _():
        kmean_ref[...] = jnp.zeros_like(kmean_ref)
        for hh in heads:
            for j in range(nb):
                rows = slice(j * blk, (j + 1) * blk)
                kr = _rope_partial(k_ref[hh, rows, :], c_ref[rows, :], s_ref[rows, :], p_ref)
                krot_ref[hh, rows, :] = kr.astype(BF16)
                kmean_ref[hh, j:j + 1, :] = jnp.mean(kr, axis=0, keepdims=True)

    r0 = pl.multiple_of(qi * blk, blk)
    cq = c_ref[pl.ds(r0, blk), :]
    sq = s_ref[pl.ds(r0, blk), :]
    lane = lax.broadcasted_iota(jnp.int32, (blk, LANE), 1)
    row = lax.broadcasted_iota(jnp.int32, (blk, blk), 0)
    col = lax.broadcasted_iota(jnp.int32, (blk, blk), 1)

    qbs, sels, state = [], [], []
    for hh in heads:
        q = _rope_partial(q_ref[hh], cq, sq, p_ref)
        qb = (q * Q_SCALE).astype(BF16)
        gate_t = lax.dot_general(kmean_ref[hh], q, _NT, precision=lax.Precision.HIGHEST,
                                 preferred_element_type=F32)[:nb_pad, :]
        sels.append(_topk_rows(gate_t, qi, min(MOBA_TOPK, nb), nb))
        qbs.append(qb)
        s = jnp.where(col <= row, _dot_nt(qb, krot_ref[hh, pl.ds(r0, blk), :]), NEG)
        m = jnp.max(s, axis=-1, keepdims=True)
        p = jnp.exp2(s - m)
        state += [m, jnp.sum(p, axis=-1, keepdims=True), _dot(p.astype(BF16), v_ref[hh, pl.ds(r0, blk), :])]

    def body(jg, carry):
        off = pl.multiple_of(jg * (grp * blk), grp * blk)
        out = []
        for hh in heads:
            m, l, acc = carry[3 * hh:3 * hh + 3]
            s = _dot_nt(qbs[hh], krot_ref[hh, pl.ds(off, grp * blk), :])
            parts = []
            for t in range(grp):
                chosen = jnp.sum(jnp.where(lane == jg * grp + t, sels[hh], 0.0), axis=-1, keepdims=True)
                parts.append(jnp.where(chosen > 0.5, s[:, t * blk:(t + 1) * blk], NEG))
            s = jnp.concatenate(parts, axis=-1)
            m_new = jnp.maximum(m, jnp.max(s, axis=-1, keepdims=True))
            a = jnp.exp2(m - m_new)
            p = jnp.exp2(s - m_new)
            out += [m_new, a * l + jnp.sum(p, axis=-1, keepdims=True),
                    a * acc + _dot(p.astype(BF16), v_ref[hh, pl.ds(off, grp * blk), :])]
        return tuple(out)

    state = lax.fori_loop(0, (qi + grp - 1) // grp, body, tuple(state))
    for hh in heads:
        o_ref[:, hh * HEAD_DIM:(hh + 1) * HEAD_DIM] = (state[3 * hh + 2] / state[3 * hh + 1]).astype(o_ref.dtype)


def _moba(proj, cos, sin, rot, batch, seq):
    nb = seq // MOBA_BLOCK
    hb = MOBA_HB
    assert seq % (MOBA_BLOCK * MOBA_GROUP) == 0 and nb <= LANE
    assert MOBA_HEADS % hb == 0 and BLK_MQ % hb == 0 and BLK_MK % hb == 0 and BLK_MV % hb == 0
    m = batch * seq
    head = lambda off: pl.BlockSpec((hb, seq, HEAD_DIM), lambda b, h, i: (off // hb + h, b, 0))
    table = pl.BlockSpec((seq, LANE), lambda b, h, i: (b, 0))
    return pl.pallas_call(
        functools.partial(_moba_kernel, nb=nb),
        grid=(batch, MOBA_HEADS // hb, nb),
        in_specs=[pl.BlockSpec((hb, MOBA_BLOCK, HEAD_DIM), lambda b, h, i: (BLK_MQ // hb + h, b * nb + i, 0)),
                  head(BLK_MK), head(BLK_MV), table, table,
                  pl.BlockSpec((HEAD_DIM, HEAD_DIM), lambda b, h, i: (0, 0))],
        out_specs=pl.BlockSpec((MOBA_BLOCK, hb * HEAD_DIM), lambda b, h, i: (b * nb + i, h)),
        out_shape=jax.ShapeDtypeStruct((m, MOBA_HEADS * HEAD_DIM), BF16),
        scratch_shapes=[pltpu.VMEM((hb, seq, HEAD_DIM), BF16), pltpu.VMEM((hb, LANE, HEAD_DIM), F32)],
        compiler_params=_params("parallel", "parallel", "arbitrary"), name="moba",
    )(proj, proj, proj, cos, sin, rot)


def _compress_kernel(x_ref, pe_ref, w1_ref, w2_ref, o_ref, xf_ref):
    seq = x_ref.shape[0]
    n_sub = seq // NSA_CMP_STRIDE
    xf_ref[:seq, :] = x_ref[...].astype(F32)
    xf_ref[seq:, :] = jnp.zeros((NSA_CMP_STRIDE, HEAD_DIM), F32)
    hid = _dot(pe_ref[...], w1_ref[...])[0:1, :]
    for r in range(NSA_CMP_LEN):
        rows = xf_ref[pl.ds(r, n_sub, stride=NSA_CMP_STRIDE), :]
        hid = hid + _dot(rows.astype(BF16), w1_ref[r * HEAD_DIM:(r + 1) * HEAD_DIM, :])
    hid = hid * _sigmoid(hid)
    o_ref[...] = _dot(hid.astype(BF16), w2_ref[...]).astype(o_ref.dtype)


def _compress(proj, pe, w1, w2, batch, seq):
    n_sub = seq // NSA_CMP_STRIDE
    flat = NSA_CMP_LEN * HEAD_DIM
    pe_flat = jnp.zeros((2, SUBLANE * 2, flat), BF16).at[:, 0].set(pe.reshape(2, flat).astype(BF16))
    return pl.pallas_call(
        _compress_kernel, grid=(2, batch, NSA_GROUPS),
        in_specs=[pl.BlockSpec((None, seq, HEAD_DIM), lambda t, b, g: (BLK_NKC + 2 * t + g, b, 0)),
                  pl.BlockSpec((None, SUBLANE * 2, flat), lambda t, b, g: (t, 0, 0)),
                  pl.BlockSpec((None, flat, HEAD_DIM), lambda t, b, g: (t, 0, 0)),
                  pl.BlockSpec((None, HEAD_DIM, HEAD_DIM), lambda t, b, g: (t, 0, 0))],
        out_specs=pl.BlockSpec((None, None, None, n_sub, HEAD_DIM), lambda t, b, g: (t, b, g, 0, 0)),
        out_shape=jax.ShapeDtypeStruct((2, batch, NSA_GROUPS, n_sub, HEAD_DIM), BF16),
        scratch_shapes=[pltpu.VMEM((seq + NSA_CMP_STRIDE, HEAD_DIM), F32)],
        compiler_params=_params("parallel", "parallel", "parallel"), name="nsa_compress",
    )(proj, pe_flat, w1.astype(BF16), w2.astype(BF16))


def _nsa_cmp_kernel(q_ref, kc_ref, vc_ref, cov_ref, oc_ref, sel_ref, *, n_sel):
    qi = pl.program_id(2)
    tq = ATT_TQ
    n_cmp = kc_ref.shape[0]
    scale = HEAD_DIM ** -0.5
    q_pos = qi * tq + lax.broadcasted_iota(jnp.int32, (tq, n_cmp), 0)
    n_idx = lax.broadcasted_iota(jnp.int32, (tq, n_cmp), 1)
    ok = n_idx * NSA_CMP_STRIDE + (NSA_CMP_LEN - 1) <= q_pos
    kc = kc_ref[...]
    vc = vc_ref[...]
    p_sum = jnp.zeros((tq, n_cmp), F32)
    for r in range(NSA_REP):
        s = jnp.where(ok, _dot_nt(q_ref[r], kc) * scale, NEG)
        mx = jnp.max(s, axis=-1, keepdims=True)
        e = jnp.where(ok, jnp.exp(s - mx), 0.0)
        l = jnp.sum(e, axis=-1, keepdims=True)
        p = e * jnp.where(l > 0.0, 1.0 / l, 0.0)
        oc_ref[r] = _dot(p.astype(BF16), vc).astype(oc_ref.dtype)
        p_sum = p_sum + p

    n_rows = -(-n_sel // SUBLANE) * SUBLANE
    imp = lax.dot_general(cov_ref[...], p_sum, _NT, precision=lax.Precision.HIGHEST,
                          preferred_element_type=F32)[:n_rows, :]
    ridx = lax.broadcasted_iota(jnp.int32, (n_rows, tq), 0)
    blk = (qi * tq + lax.broadcasted_iota(jnp.int32, (n_rows, tq), 1)) // NSA_SEL_LEN
    forced = (ridx == 0) | (ridx == blk) | (ridx == blk - 1)
    x = jnp.where(forced, FORCE, imp)
    sel_ref[...] = _topk_rows(x, blk[0:1, :] + 1, min(NSA_TOPK, n_sel), n_sel).astype(sel_ref.dtype)


def _cover_matrix(n_cmp_pad, n_cmp, n_sel):
    c_start = np.arange(n_cmp_pad) * NSA_CMP_STRIDE
    s_start = np.arange(LANE) * NSA_SEL_LEN
    cover = (c_start[:, None] < s_start[None, :] + NSA_SEL_LEN) & (c_start[:, None] + NSA_CMP_LEN > s_start[None, :])
    cover &= (np.arange(n_cmp_pad) < n_cmp)[:, None] & (np.arange(LANE) < n_sel)[None, :]
    return jnp.asarray(cover.T, F32)


def _nsa_cmp(proj, kvc, batch, seq):
    nq = seq // ATT_TQ
    n_sub = seq // NSA_CMP_STRIDE
    n_sel = seq // NSA_SEL_LEN
    assert n_sel <= LANE
    m = batch * seq
    cover = _cover_matrix(n_sub, n_sub - NSA_CMP_LEN // NSA_CMP_STRIDE + 1, n_sel)
    cmp_spec = lambda t: pl.BlockSpec((None, None, None, n_sub, HEAD_DIM), lambda b, g, i: (t, b, g, 0, 0))
    return pl.pallas_call(
        functools.partial(_nsa_cmp_kernel, n_sel=n_sel),
        grid=(batch, NSA_GROUPS, nq),
        in_specs=[pl.BlockSpec((NSA_REP, ATT_TQ, HEAD_DIM), lambda b, g, i: (BLK_NQ // NSA_REP + g, b * nq + i, 0)),
                  cmp_spec(0), cmp_spec(1),
                  pl.BlockSpec((LANE, n_sub), lambda b, g, i: (0, 0))],
        out_specs=[pl.BlockSpec((NSA_REP, ATT_TQ, HEAD_DIM), lambda b, g, i: (g, b * nq + i, 0)),
                   pl.BlockSpec((ATT_TQ, LANE), lambda b, g, i: (b * nq + i, g))],
        out_shape=[jax.ShapeDtypeStruct((NSA_HEADS, m, HEAD_DIM), BF16),
                   jax.ShapeDtypeStruct((m, NSA_GROUPS * LANE), BF16)],
        compiler_params=_params("parallel", "parallel", "parallel"), name="nsa_cmp",
    )(proj, kvc, kvc, cover)


def _nsa_sw_kernel(q_ref, ks_ref, vs_ref, kw_ref, vw_ref, sel_ref, oc_ref, gl_ref, c_ref, s_ref, p_ref,
                   o_ref, ksr_ref, kwr_ref, vwp_ref, qr_ref, m_ref, l_ref, acc_ref, ow_ref, *, seq):
    grp = pl.program_id(1)
    qi = pl.program_id(2)
    tq, tk = ATT_TQ, ATT_TK
    rep = NSA_REP
    win = NSA_WINDOW
    gk = SEL_GROUP * tk

    @pl.when(qi == 0)
    def _():
        kwr_ref[:win, :] = jnp.zeros((win, HEAD_DIM), BF16)
        vwp_ref[:win, :] = jnp.zeros((win, HEAD_DIM), BF16)
        for j in range(seq // tk):
            rows = slice(j * tk, (j + 1) * tk)
            shifted = slice(win + j * tk, win + (j + 1) * tk)
            c, s = c_ref[rows, :], s_ref[rows, :]
            ksr_ref[rows, :] = _rope_partial(ks_ref[rows, :], c, s, p_ref).astype(BF16)
            kwr_ref[shifted, :] = _rope_partial(kw_ref[rows, :], c, s, p_ref).astype(BF16)
            vwp_ref[shifted, :] = vw_ref[rows, :]

    r0 = pl.multiple_of(qi * tq, tq)
    cq = c_ref[pl.ds(r0, tq), :]
    sq = s_ref[pl.ds(r0, tq), :]
    for r in range(rep):
        qr_ref[r] = (_rope_partial(q_ref[r], cq, sq, p_ref) * Q_SCALE).astype(BF16)

    q_all = qr_ref[...].reshape(rep * tq, HEAD_DIM)

    row_w = lax.broadcasted_iota(jnp.int32, (tq, win + tq), 0)
    col_w = lax.broadcasted_iota(jnp.int32, (tq, win + tq), 1)
    in_win = (col_w > row_w) & (col_w <= row_w + win) & (col_w >= win - r0)
    s = _dot_nt(q_all, kwr_ref[pl.ds(r0, win + tq), :]).reshape(rep, tq, win + tq)
    s = jnp.where(in_win[None], s, NEG)
    p = jnp.exp2(s - jnp.max(s, axis=-1, keepdims=True))
    l = jnp.sum(p, axis=-1, keepdims=True)
    o_w = _dot(p.astype(BF16).reshape(rep * tq, win + tq), vwp_ref[pl.ds(r0, win + tq), :])
    ow_ref[...] = o_w.reshape(rep, tq, HEAD_DIM) / l

    sel = sel_ref[...]
    row = lax.broadcasted_iota(jnp.int32, (tq, gk), 0)
    col = lax.broadcasted_iota(jnp.int32, (tq, gk), 1)

    def chosen_keys(jg):
        blk_row = lax.broadcasted_iota(jnp.int32, (LANE, gk), 0)
        blk_col = jg * (gk // NSA_SEL_LEN) + lax.broadcasted_iota(jnp.int32, (LANE, gk), 1) // NSA_SEL_LEN
        return _dot(sel, jnp.where(blk_row == blk_col, 1.0, 0.0).astype(BF16))

    def attend(off, valid, first):
        s = _dot_nt(q_all, ksr_ref[pl.ds(off, gk), :]).reshape(rep, tq, gk)
        s = jnp.where(valid[None], s, NEG)
        m_new = jnp.max(s, axis=-1, keepdims=True)
        if not first:
            m_old = m_ref[...]
            m_new = jnp.maximum(m_old, m_new)
        p = jnp.exp2(s - m_new)
        l = jnp.sum(p, axis=-1, keepdims=True)
        pv = _dot(p.astype(BF16).reshape(rep * tq, gk), vs_ref[pl.ds(off, gk), :]).reshape(rep, tq, HEAD_DIM)
        if first:
            l_ref[...] = l
            acc_ref[...] = pv
        else:
            a = jnp.exp2(m_old - m_new)
            l_ref[...] = a * l_ref[...] + l
            acc_ref[...] = a * acc_ref[...] + pv
        m_ref[...] = m_new

    gd = qi // SEL_GROUP
    off_d = pl.multiple_of(gd * gk, gk)
    causal = col + off_d <= row + r0
    attend(off_d, jnp.where(causal, chosen_keys(gd), 0.0) > 0.5, True)

    def sel_body(jg, carry):
        attend(pl.multiple_of(jg * gk, gk), chosen_keys(jg) > 0.5, False)
        return carry

    lax.fori_loop(0, gd, sel_body, 0)

    gates = _sigmoid(gl_ref[...].astype(F32))
    lane = lax.broadcasted_iota(jnp.int32, (tq, LANE), 1)

    def gate_col(c):
        return jnp.sum(jnp.where(lane == c, gates, 0.0), axis=-1, keepdims=True)

    for r in range(rep):
        base = (grp * rep + r) * 3
        o_s = acc_ref[r] / l_ref[r]
        o = gate_col(base) * oc_ref[r].astype(F32) + gate_col(base + 1) * o_s + gate_col(base + 2) * ow_ref[r]
        o_ref[:, r * HEAD_DIM:(r + 1) * HEAD_DIM] = o.astype(o_ref.dtype)


def _nsa_sw(proj, sel, o_cmp, cos, sin, rot, batch, seq):
    nq = seq // ATT_TQ
    assert seq % (SEL_GROUP * ATT_TK) == 0 and NSA_WINDOW % ATT_TK == 0 and ATT_TQ == ATT_TK
    m = batch * seq
    rep = NSA_REP
    head = lambda off: pl.BlockSpec((None, seq, HEAD_DIM), lambda b, g, i: (off + g, b, 0))
    table = pl.BlockSpec((seq, LANE), lambda b, g, i: (b, 0))
    q_like = lambda off: pl.BlockSpec((rep, ATT_TQ, HEAD_DIM), lambda b, g, i: (off + g, b * nq + i, 0))
    return pl.pallas_call(
        functools.partial(_nsa_sw_kernel, seq=seq),
        grid=(batch, NSA_GROUPS, nq),
        in_specs=[q_like(BLK_NQ // rep), head(BLK_NKS), head(BLK_NVS), head(BLK_NKW), head(BLK_NVW),
                  pl.BlockSpec((ATT_TQ, LANE), lambda b, g, i: (b * nq + i, g)),
                  q_like(0),
                  pl.BlockSpec((None, ATT_TQ, LANE), lambda b, g, i: (BLK_NG, b * nq + i, 0)),
                  table, table,
                  pl.BlockSpec((HEAD_DIM, HEAD_DIM), lambda b, g, i: (0, 0))],
        out_specs=pl.BlockSpec((ATT_TQ, rep * HEAD_DIM), lambda b, g, i: (b * nq + i, g)),
        out_shape=jax.ShapeDtypeStruct((m, NSA_HEADS * HEAD_DIM), BF16),
        scratch_shapes=[pltpu.VMEM((seq, HEAD_DIM), BF16),
                        pltpu.VMEM((seq + NSA_WINDOW, HEAD_DIM), BF16),
                        pltpu.VMEM((seq + NSA_WINDOW, HEAD_DIM), BF16),
                        pltpu.VMEM((rep, ATT_TQ, HEAD_DIM), BF16),
                        pltpu.VMEM((rep, ATT_TQ, 1), F32), pltpu.VMEM((rep, ATT_TQ, 1), F32),
                        pltpu.VMEM((rep, ATT_TQ, HEAD_DIM), F32), pltpu.VMEM((rep, ATT_TQ, HEAD_DIM), F32)],
        compiler_params=_params("parallel", "parallel", "arbitrary"), name="nsa_sel_win",
    )(proj, proj, proj, proj, proj, sel, o_cmp, proj, cos, sin, rot)


def _rope_full(x_bf16, c, s):
    half = x_bf16.shape[-1] // 2
    x = x_bf16.astype(F32)
    x1, x2 = x[:, :half], x[:, half:]
    return jnp.concatenate([x1 * c - x2 * s, x1 * s + x2 * c], axis=-1)


def _retention_kernel(q_ref, k_ref, v_ref, g_ref, c_ref, s_ref, lg_ref, gn_ref, o_ref, state_ref):
    ci = pl.program_id(1)
    ch = q_ref.shape[0]

    @pl.when(ci == 0)
    def _():
        state_ref[...] = jnp.zeros_like(state_ref)

    c, s = c_ref[...], s_ref[...]
    n_row = lax.broadcasted_iota(jnp.int32, (ch, ch), 0)
    n_col = lax.broadcasted_iota(jnp.int32, (ch, ch), 1)
    diff = (n_row - n_col).astype(F32)
    n_vec = lax.broadcasted_iota(jnp.int32, (ch, 1), 0).astype(F32)

    for h in range(RET_HEADS):
        log_g = lg_ref[h][:, 0:1]
        qk_cols = slice(h * RET_DK, (h + 1) * RET_DK)
        v_cols = slice(h * RET_DV, (h + 1) * RET_DV)
        q = _rope_full(q_ref[:, qk_cols], c, s)
        k = _rope_full(k_ref[:, qk_cols], c, s) * (RET_DK ** -0.5)
        v = v_ref[:, v_cols]

        decay = jnp.where(diff >= 0.0, jnp.exp(jnp.maximum(diff, 0.0) * log_g), 0.0)
        q_dec = jnp.exp((n_vec + 1.0) * log_g)
        k_dec = jnp.exp((ch - 1.0 - n_vec) * log_g)
        c_dec = jnp.exp(ch * log_g)

        qb = q.astype(BF16)
        state = state_ref[h]
        scores = _dot_nt(qb, k.astype(BF16)) * decay
        y = _dot(scores.astype(BF16), v) + _dot(qb, state.astype(BF16)) * q_dec
        state_ref[h] = state * c_dec + lax.dot_general((k * k_dec).astype(BF16), v, _TN,
                                                       preferred_element_type=F32)

        mu = jnp.mean(y, axis=-1, keepdims=True)
        yc = y - mu
        var = jnp.mean(yc * yc, axis=-1, keepdims=True)
        yn = yc * lax.rsqrt(var + RMS_EPS) * gn_ref[h]
        gate = g_ref[:, v_cols].astype(F32)
        o_ref[:, v_cols] = (gate * _sigmoid(gate) * yn).astype(o_ref.dtype)


def _retention(proj, cos_r, sin_r, gn, batch, seq):
    ch = RET_CHUNK
    nch = seq // ch
    assert seq % ch == 0
    m = batch * seq
    log_g = jnp.log(1.0 - jnp.exp2(-5.0 - jnp.arange(RET_HEADS, dtype=F32)))
    log_g = jnp.broadcast_to(log_g[:, None, None], (RET_HEADS, 1, LANE))
    qk_w = RET_HEADS * RET_DK
    v_w = RET_HEADS * RET_DV
    assert v_w == 2 * qk_w
    rows = lambda b, c: b * nch + c
    return pl.pallas_call(
        _retention_kernel, grid=(batch, nch),
        in_specs=[pl.BlockSpec((ch, qk_w), lambda b, c: (rows(b, c), 0)),
                  pl.BlockSpec((ch, qk_w), lambda b, c: (rows(b, c), 1)),
                  pl.BlockSpec((ch, v_w), lambda b, c: (rows(b, c), 1)),
                  pl.BlockSpec((ch, v_w), lambda b, c: (rows(b, c), 2)),
                  pl.BlockSpec((ch, LANE), lambda b, c: (rows(b, c), 0)),
                  pl.BlockSpec((ch, LANE), lambda b, c: (rows(b, c), 0)),
                  pl.BlockSpec((RET_HEADS, 1, LANE), lambda b, c: (0, 0, 0)),
                  pl.BlockSpec((RET_HEADS, 1, RET_DV), lambda b, c: (0, 0, 0))],
        out_specs=pl.BlockSpec((ch, v_w), lambda b, c: (rows(b, c), 0)),
        out_shape=jax.ShapeDtypeStruct((m, v_w), BF16),
        scratch_shapes=[pltpu.VMEM((RET_HEADS, RET_DK, RET_DV), F32)],
        compiler_params=_params("parallel", "arbitrary"), name="retention",
    )(proj, proj, proj, proj, cos_r, sin_r, log_g, gn.reshape(RET_HEADS, 1, RET_DV).astype(F32))


def _cross_kernel(h_ref, g_ref, wq_ref, kv_ref, wo_ref, o_ref):
    h = h_ref[...]
    hn = _rms_rows(h, g_ref[...]).astype(BF16)
    q = (_dot(hn, wq_ref[...]) * Q_SCALE).astype(BF16)
    kv_cols = X_HEADS * HEAD_DIM
    outs = []
    for hd in range(X_HEADS):
        cols = slice(hd * HEAD_DIM, (hd + 1) * HEAD_DIM)
        s = _dot_nt(q[:, cols], kv_ref[:, cols])
        mx = jnp.max(s, axis=-1, keepdims=True)
        p = jnp.exp2(s - mx)
        l = jnp.sum(p, axis=-1, keepdims=True)
        p = p / l
        outs.append(_dot(p.astype(BF16), kv_ref[:, kv_cols + hd * HEAD_DIM:kv_cols + (hd + 1) * HEAD_DIM]))
    o = jnp.concatenate(outs, axis=-1).astype(BF16)
    o_ref[...] = h + _dot(o, wo_ref[...])


def _cross(h, g, wq, kv, wo, layer, batch, seq, n_mem, *, tm):
    m, d = h.shape
    per_b = seq // tm
    kv_cols = 2 * X_HEADS * HEAD_DIM
    return pl.pallas_call(
        _cross_kernel, grid=(m // tm,),
        in_specs=[pl.BlockSpec((tm, d), lambda i: (i, 0)),
                  pl.BlockSpec((1, d), lambda i: (0, 0)),
                  pl.BlockSpec((d, X_HEADS * HEAD_DIM), lambda i: (0, 0)),
                  pl.BlockSpec((n_mem, kv_cols), lambda i: (i // per_b, layer)),
                  pl.BlockSpec((X_HEADS * HEAD_DIM, d), lambda i: (0, 0))],
        out_specs=pl.BlockSpec((tm, d), lambda i: (i, 0)),
        out_shape=jax.ShapeDtypeStruct((m, d), F32),
        compiler_params=_params("parallel"), name="cross_attn",
    )(h, g.reshape(1, d), wq, kv, wo)


def _ffn_kernel(h_ref, halo_ref, g_ref, wg_ref, wv_ref, cwg_ref, cwv_ref, cbg_ref, cbv_ref, wd_ref,
                o_ref, hn_ref, acc_ref, *, per_b):
    i = pl.program_id(0)
    f = pl.program_id(1)
    tm = h_ref.shape[0]
    pad = SUBLANE * 2

    @pl.when(f == 0)
    def _():
        g = g_ref[...]
        hn_ref[pad:, :] = _rms_rows(h_ref[...], g).astype(BF16)
        halo = _rms_rows(halo_ref[...], g)
        halo = jnp.where(i % per_b == 0, 0.0, halo)
        hn_ref[:pad, :] = halo.astype(BF16)
        acc_ref[...] = jnp.zeros_like(acc_ref)

    hn = hn_ref[...]

    def conv(u, w, b):
        return (b + w[0:1, :] * u[pad - 2:pad - 2 + tm] + w[1:2, :] * u[pad - 1:pad - 1 + tm]
                + w[2:3, :] * u[pad:pad + tm])

    gate = conv(_dot(hn, wg_ref[...]), cwg_ref[...], cbg_ref[...])
    val = conv(_dot(hn, wv_ref[...]), cwv_ref[...], cbv_ref[...])
    act = (gate * _sigmoid(gate) * val).astype(BF16)
    acc_ref[...] += _dot(act, wd_ref[...])

    @pl.when(f == pl.num_programs(1) - 1)
    def _():
        o_ref[...] = h_ref[...] + acc_ref[...]


def _ffn(h, g, w_up, conv_w, conv_b, w_down, batch, seq, *, tm, tf):
    m, d = h.shape
    dff = w_down.shape[0]
    assert seq % tm == 0 and dff % tf == 0
    per_b = seq // tm
    nf = dff // tf
    pad = SUBLANE * 2
    halo_blocks = tm // pad
    cw = jnp.zeros((SUBLANE, 2 * dff), F32).at[:CONV_WIDTH].set(conv_w)
    cb = conv_b.reshape(1, 2 * dff)
    return pl.pallas_call(
        functools.partial(_ffn_kernel, per_b=per_b),
        grid=(m // tm, nf),
        in_specs=[pl.BlockSpec((tm, d), lambda i, f: (i, 0)),
                  pl.BlockSpec((pad, d), lambda i, f: (jnp.maximum(i * halo_blocks - 1, 0), 0)),
                  pl.BlockSpec((1, d), lambda i, f: (0, 0)),
                  pl.BlockSpec((d, tf), lambda i, f: (0, f)),
                  pl.BlockSpec((d, tf), lambda i, f: (0, nf + f)),
                  pl.BlockSpec((SUBLANE, tf), lambda i, f: (0, f)),
                  pl.BlockSpec((SUBLANE, tf), lambda i, f: (0, nf + f)),
                  pl.BlockSpec((1, tf), lambda i, f: (0, f)),
                  pl.BlockSpec((1, tf), lambda i, f: (0, nf + f)),
                  pl.BlockSpec((tf, d), lambda i, f: (f, 0))],
        out_specs=pl.BlockSpec((tm, d), lambda i, f: (i, 0)),
        out_shape=jax.ShapeDtypeStruct((m, d), F32),
        scratch_shapes=[pltpu.VMEM((tm + pad, d), BF16), pltpu.VMEM((tm, d), F32)],
        compiler_params=_params("parallel", "arbitrary"), name="conv_ffn",
    )(h, h, g.reshape(1, d), w_up, w_up, cw, cw, cb, cb, w_down)


def _final_norm_kernel(x_ref, g_ref, o_ref):
    o_ref[...] = _rms_rows(x_ref[...], g_ref[...])


def _final_norm(x, g, *, tm):
    m, d = x.shape
    return pl.pallas_call(
        _final_norm_kernel, grid=(m // tm,),
        in_specs=[pl.BlockSpec((tm, d), lambda i: (i, 0)), pl.BlockSpec((1, d), lambda i: (0, 0))],
        out_specs=pl.BlockSpec((tm, d), lambda i: (i, 0)),
        out_shape=jax.ShapeDtypeStruct((m, d), F32),
        compiler_params=_params("parallel"), name="final_norm",
    )(x, g.reshape(1, d))


def _row_tile(m, want):
    return want if m % want == 0 else m


def _mixer_ab(h, g, w_in, pe, w1, w2, w_out, tabs, rot, batch, seq):
    m, d = h.shape
    cos, sin = tabs[0], tabs[1]
    w_pad = jnp.zeros((d, AB_BLOCKS * LANE), BF16).at[:, :AB_COLS].set(w_in.astype(BF16))
    proj = _norm_matmul(h, g, w_pad, tm=_row_tile(m, 1024), tn=2048, head_major=True)
    o_moba = _moba(proj, cos, sin, rot, batch, seq)
    kvc = _compress(proj, pe, w1, w2, batch, seq)
    o_cmp, sel = _nsa_cmp(proj, kvc, batch, seq)
    o_nsa = _nsa_sw(proj, sel, o_cmp, cos, sin, rot, batch, seq)
    return _matmul_res([o_moba, o_nsa], w_out.astype(BF16), h, tm=_row_tile(m, 1024), tn=1024)


def _mixer_c(h, g, w_in, gn, w_out, tabs, batch, seq):
    m, d = h.shape
    proj = _norm_matmul(h, g, w_in.astype(BF16), tm=_row_tile(m, 1024), tn=2048)
    y = _retention(proj, tabs[2], tabs[3], gn, batch, seq)
    return _matmul_res([y], w_out.astype(BF16), h, tm=_row_tile(m, 1024), tn=1024)


def kernel(x, mem, positions, norm_mix, norm_cross, norm_ffn, norm_mem, norm_final, w_in_ab, cmp_pe_k, cmp_w1_k,
           cmp_w2_k, cmp_pe_v, cmp_w1_v, cmp_w2_v, w_out_ab, w_in_c, ret_gn, w_out_c, w_q_x, w_kv_x, w_o_x, w_up,
           conv_w, conv_b, w_down):
    batch, seq, d = x.shape
    n_mem = mem.shape[1]
    depth = norm_mix.shape[0]
    m = batch * seq
    tabs = _rope_tables(positions)
    rot = _rot_matrix()

    kv_cols = 2 * X_HEADS * HEAD_DIM
    w_kv = jnp.transpose(w_kv_x, (1, 0, 2)).reshape(d, depth * kv_cols).astype(BF16)
    kv = _norm_matmul(mem.reshape(batch * n_mem, d), norm_mem, w_kv, tm=_row_tile(batch * n_mem, 512), tn=512)

    h = x.reshape(m, d)
    for l in range(depth):
        if l % 2 == 0:
            e = l // 2
            h = _mixer_ab(h, norm_mix[l], w_in_ab[e],
                          jnp.stack([cmp_pe_k[e], cmp_pe_v[e]]), jnp.stack([cmp_w1_k[e], cmp_w1_v[e]]),
                          jnp.stack([cmp_w2_k[e], cmp_w2_v[e]]), w_out_ab[e], tabs, rot, batch, seq)
        else:
            o = l // 2
            h = _mixer_c(h, norm_mix[l], w_in_c[o], ret_gn[o], w_out_c[o], tabs, batch, seq)
        h = _cross(h, norm_cross[l], w_q_x[l].astype(BF16), kv, w_o_x[l].astype(BF16), l, batch, seq, n_mem,
                   tm=_row_tile(seq, 512))
        h = _ffn(h, norm_ffn[l], w_up[l].astype(BF16), conv_w[l], conv_b[l], w_down[l].astype(BF16), batch, seq,
                 tm=_row_tile(seq, 512), tf=512)
    return _final_norm(h, norm_final, tm=_row_tile(m, 512)).reshape(batch, seq, d)
```

```python
import functools

import numpy as np
import jax
import jax.numpy as jnp
from jax import lax
from jax.experimental import pallas as pl
from jax.experimental.pallas import tpu as pltpu

F32 = jnp.float32
BF16 = jnp.bfloat16

D_MODEL = 2048
DEPTH = 4
HEAD_DIM = 128
ROPE_THETA = 500000.0
ROPE_DIM = HEAD_DIM // 4
X_HEADS = 4
MOBA_HEADS = 8
MOBA_BLOCK = 256
MOBA_TOPK = 3
NSA_HEADS = 8
NSA_GROUPS = 2
NSA_REP = NSA_HEADS // NSA_GROUPS
NSA_CMP_LEN = 32
NSA_CMP_STRIDE = 16
NSA_SEL_LEN = 64
NSA_TOPK = 16
NSA_WINDOW = 512
RET_HEADS = 8
RET_DK = 256
RET_DV = 512
RET_THETA = 10000.0
D_FF = 5632
CONV_WIDTH = 3
RMS_EPS = 1e-6
NEG = -1e30
FORCE = 1e9
LOG2E = 1.4426950408889634
Q_SCALE = HEAD_DIM ** -0.5 * LOG2E

LANE = 128
SUBLANE = 8
VMEM_LIMIT = 56 * 2 ** 20

AB_SIZES = (MOBA_HEADS * HEAD_DIM,) * 3 + (NSA_HEADS * HEAD_DIM,) + (NSA_GROUPS * HEAD_DIM,) * 6 + (NSA_HEADS * 3,)
AB_COLS = sum(AB_SIZES)
AB_BLOCKS = 48
BLK_MQ, BLK_MK, BLK_MV, BLK_NQ = 0, 8, 16, 24
BLK_NKC, BLK_NVC, BLK_NKS, BLK_NVS, BLK_NKW, BLK_NVW, BLK_NG = 32, 34, 36, 38, 40, 42, 44

ATT_TQ = 256
ATT_TK = 256
RET_CHUNK = 128
MOBA_GROUP = 4
MOBA_HB = 4
SEL_GROUP = 4

_NT = (((1,), (1,)), ((), ()))
_TN = (((0,), (0,)), ((), ()))


def _params(*sem):
    return pltpu.CompilerParams(dimension_semantics=sem, vmem_limit_bytes=VMEM_LIMIT)


def _dot(a, b):
    return jnp.dot(a, b, preferred_element_type=F32)


def _dot_nt(a, b):
    return lax.dot_general(a, b, _NT, preferred_element_type=F32)


def _sigmoid(x):
    return 1.0 / (1.0 + jnp.exp(-x))


def _rms_rows(x, g):
    ms = jnp.mean(x * x, axis=-1, keepdims=True)
    return x * lax.rsqrt(ms + RMS_EPS) * g


def _tables_kernel(pos_ref, inv_rope_ref, inv_ret_ref, c_ref, s_ref, cr_ref, sr_ref):
    pos = pos_ref[...]
    lane = lax.broadcasted_iota(jnp.int32, pos.shape, 1)
    rot = lane < ROPE_DIM
    ang = pos * inv_rope_ref[...]
    c_ref[...] = jnp.where(rot, jnp.cos(ang), 1.0)
    s_ref[...] = jnp.where(rot, jnp.sin(ang), 0.0)
    ang_r = pos * inv_ret_ref[...]
    cr_ref[...] = jnp.cos(ang_r)
    sr_ref[...] = jnp.sin(ang_r)


def _rope_tables(positions):
    m = positions.size
    posb = jnp.broadcast_to(positions.reshape(m, 1).astype(F32), (m, LANE))
    half = ROPE_DIM // 2
    inv = jnp.float32(ROPE_THETA) ** (-jnp.arange(half, dtype=F32) / half)
    inv_rope = jnp.concatenate([inv, inv, jnp.zeros((LANE - ROPE_DIM,), F32)]).reshape(1, LANE)
    half_r = RET_DK // 2
    inv_ret = (jnp.float32(RET_THETA) ** (-jnp.arange(half_r, dtype=F32) / half_r)).reshape(1, LANE)
    tm = 1024 if m % 1024 == 0 else m
    row = pl.BlockSpec((tm, LANE), lambda i: (i, 0))
    vec = pl.BlockSpec((1, LANE), lambda i: (0, 0))
    sds = jax.ShapeDtypeStruct((m, LANE), F32)
    return pl.pallas_call(
        _tables_kernel, grid=(m // tm,), in_specs=[row, vec, vec], out_specs=[row] * 4,
        out_shape=[sds] * 4, compiler_params=_params("parallel"), name="rope_tables",
    )(posb, inv_rope, inv_ret)


def _rot_matrix():
    half = ROPE_DIM // 2
    p = np.zeros((HEAD_DIM, HEAD_DIM), np.float32)
    for l in range(half):
        p[l + half, l] = -1.0
        p[l, l + half] = 1.0
    return jnp.asarray(p, BF16)


def _rope_partial(x_bf16, c, s, p_ref):
    return x_bf16.astype(F32) * c + _dot(x_bf16, p_ref[...]) * s


def _norm_matmul_kernel(x_ref, g_ref, w_ref, o_ref, xn_ref, *, head_major):
    @pl.when(pl.program_id(1) == 0)
    def _():
        xn_ref[...] = _rms_rows(x_ref[...], g_ref[...]).astype(BF16)

    acc = _dot(xn_ref[...], w_ref[...])
    if head_major:
        for c in range(o_ref.shape[0]):
            o_ref[c] = acc[:, c * LANE:(c + 1) * LANE].astype(o_ref.dtype)
    else:
        o_ref[...] = acc.astype(o_ref.dtype)


def _norm_matmul(x, g, w, *, tm, tn, head_major=False):
    m, k = x.shape
    n = w.shape[1]
    assert m % tm == 0 and n % tn == 0 and tn % LANE == 0
    if head_major:
        out_shape = jax.ShapeDtypeStruct((n // LANE, m, LANE), BF16)
        out_spec = pl.BlockSpec((tn // LANE, tm, LANE), lambda i, j: (j, i, 0))
    else:
        out_shape = jax.ShapeDtypeStruct((m, n), BF16)
        out_spec = pl.BlockSpec((tm, tn), lambda i, j: (i, j))
    return pl.pallas_call(
        functools.partial(_norm_matmul_kernel, head_major=head_major),
        grid=(m // tm, n // tn),
        in_specs=[pl.BlockSpec((tm, k), lambda i, j: (i, 0)),
                  pl.BlockSpec((1, k), lambda i, j: (0, 0)),
                  pl.BlockSpec((k, tn), lambda i, j: (0, j))],
        out_specs=out_spec, out_shape=out_shape,
        scratch_shapes=[pltpu.VMEM((tm, k), BF16)],
        compiler_params=_params("parallel", "arbitrary"), name="norm_matmul",
    )(x, g.reshape(1, k), w)


def _matmul_res_kernel(*refs):
    r_ref, o_ref = refs[-2], refs[-1]
    acc = r_ref[...]
    for x_ref, w_ref in zip(refs[:-2:2], refs[1:-2:2]):
        acc = acc + _dot(x_ref[...], w_ref[...])
    o_ref[...] = acc


def _matmul_res(xs, w, res, *, tm, tn):
    m, n = res.shape
    k = xs[0].shape[1]
    assert m % tm == 0 and n % tn == 0 and all(x.shape[1] == k for x in xs) and w.shape[0] == k * len(xs)
    in_specs, args = [], []
    for idx, x in enumerate(xs):
        in_specs += [pl.BlockSpec((tm, k), lambda i, j: (i, 0)),
                     pl.BlockSpec((k, tn), lambda i, j, idx=idx: (idx, j))]
        args += [x, w]
    return pl.pallas_call(
        _matmul_res_kernel, grid=(m // tm, n // tn),
        in_specs=in_specs + [pl.BlockSpec((tm, tn), lambda i, j: (i, j))],
        out_specs=pl.BlockSpec((tm, tn), lambda i, j: (i, j)),
        out_shape=jax.ShapeDtypeStruct((m, n), F32),
        compiler_params=_params("parallel", "arbitrary"), name="matmul_res",
    )(*args, res)


def _topk_rows(x, n_valid, k, n_rows):
    rows, nq = x.shape
    ridx = lax.broadcasted_iota(jnp.int32, (rows, nq), 0)
    x = jnp.where(ridx < n_valid, x, NEG)
    x = jnp.where(ridx < n_rows, x, -jnp.inf)
    rank = jnp.zeros((rows, nq), F32)
    for i in range(n_rows):
        xi = x[i:i + 1, :]
        rank = rank + jnp.where(ridx > i, jnp.where(xi >= x, 1.0, 0.0), jnp.where(xi > x, 1.0, 0.0))
    marks = jnp.where(rank < float(k), jnp.where(x > NEG / 2, 1.0, 0.0), 0.0)
    if rows < LANE:
        marks = jnp.concatenate([marks, jnp.zeros((LANE - rows, nq), F32)], axis=0)
    return marks.T


def _moba_kernel(q_ref, k_ref, v_ref, c_ref, s_ref, p_ref, o_ref, krot_ref, kmean_ref, vext_ref, *, nb):
    qi = pl.program_id(2)
    blk = MOBA_BLOCK
    nb_pad = -(-nb // SUBLANE) * SUBLANE
    grp = MOBA_GROUP
    heads = range(MOBA_HB)

    @pl.when(qi == 0)
    def _():
        kmean_ref[...] = jnp.zeros_like(kmean_ref)
        for hh in heads:
            for j in range(nb):
                rows = slice(j * blk, (j + 1) * blk)
                kr = _rope_partial(k_ref[hh, rows, :], c_ref[rows, :], s_ref[rows, :], p_ref)
                krot_ref[hh, rows, :] = kr.astype(BF16)
                kmean_ref[hh, j:j + 1, :] = jnp.mean(kr, axis=0, keepdims=True)
            vext_ref[hh, :, :HEAD_DIM] = v_ref[hh]
            vext_ref[hh, :, HEAD_DIM:] = jnp.ones((v_ref.shape[1], HEAD_DIM), BF16)

    r0 = pl.multiple_of(qi * blk, blk)
    cq = c_ref[pl.ds(r0, blk), :]
    sq = s_ref[pl.ds(r0, blk), :]
    lane = lax.broadcasted_iota(jnp.int32, (blk, LANE), 1)
    row = lax.broadcasted_iota(jnp.int32, (blk, blk), 0)
    col = lax.broadcasted_iota(jnp.int32, (blk, blk), 1)

    qbs, sels, state = [], [], []
    for hh in heads:
        q = _rope_partial(q_ref[hh], cq, sq, p_ref)
        qb = (q * Q_SCALE).astype(BF16)
        gate_t = lax.dot_general(kmean_ref[hh], q, _NT, precision=lax.Precision.HIGHEST,
                                 preferred_element_type=F32)[:nb_pad, :]
        sels.append(_topk_rows(gate_t, qi, min(MOBA_TOPK, nb), nb))
        qbs.append(qb)
        s = jnp.where(col <= row, _dot_nt(qb, krot_ref[hh, pl.ds(r0, blk), :]), NEG)
        m = jnp.max(s, axis=-1, keepdims=True)
        p = jnp.exp2((s - m).astype(BF16))
        state += [m, _dot(p, vext_ref[hh, pl.ds(r0, blk), :])]

    def body(jg, carry):
        off = pl.multiple_of(jg * (grp * blk), grp * blk)
        out = []
        for hh in heads:
            m, acc = carry[2 * hh:2 * hh + 2]
            s = _dot_nt(qbs[hh], krot_ref[hh, pl.ds(off, grp * blk), :])
            parts = []
            for t in range(grp):
                chosen = jnp.sum(jnp.where(lane == jg * grp + t, sels[hh], 0.0), axis=-1, keepdims=True)
                parts.append(jnp.where(chosen > 0.5, s[:, t * blk:(t + 1) * blk], NEG))
            s = jnp.concatenate(parts, axis=-1)
            m_new = jnp.maximum(m, jnp.max(s, axis=-1, keepdims=True))
            p = jnp.exp2((s - m_new).astype(BF16))
            out += [m_new, jnp.exp2(m - m_new) * acc + _dot(p, vext_ref[hh, pl.ds(off, grp * blk), :])]
        return tuple(out)

    state = lax.fori_loop(0, (qi + grp - 1) // grp, body, tuple(state))
    for hh in heads:
        acc = state[2 * hh + 1]
        o_ref[:, hh * HEAD_DIM:(hh + 1) * HEAD_DIM] = (acc[:, :HEAD_DIM] / acc[:, HEAD_DIM:HEAD_DIM + 1]).astype(o_ref.dtype)


def _moba(proj, cos, sin, rot, batch, seq):
    nb = seq // MOBA_BLOCK
    hb = MOBA_HB
    assert seq % (MOBA_BLOCK * MOBA_GROUP) == 0 and nb <= LANE
    assert MOBA_HEADS % hb == 0 and BLK_MQ % hb == 0 and BLK_MK % hb == 0 and BLK_MV % hb == 0
    m = batch * seq
    head = lambda off: pl.BlockSpec((hb, seq, HEAD_DIM), lambda b, h, i: (off // hb + h, b, 0))
    table = pl.BlockSpec((seq, LANE), lambda b, h, i: (b, 0))
    return pl.pallas_call(
        functools.partial(_moba_kernel, nb=nb),
        grid=(batch, MOBA_HEADS // hb, nb),
        in_specs=[pl.BlockSpec((hb, MOBA_BLOCK, HEAD_DIM), lambda b, h, i: (BLK_MQ // hb + h, b * nb + i, 0)),
                  head(BLK_MK), head(BLK_MV), table, table,
                  pl.BlockSpec((HEAD_DIM, HEAD_DIM), lambda b, h, i: (0, 0))],
        out_specs=pl.BlockSpec((MOBA_BLOCK, hb * HEAD_DIM), lambda b, h, i: (b * nb + i, h)),
        out_shape=jax.ShapeDtypeStruct((m, MOBA_HEADS * HEAD_DIM), BF16),
        scratch_shapes=[pltpu.VMEM((hb, seq, HEAD_DIM), BF16), pltpu.VMEM((hb, LANE, HEAD_DIM), F32),
                        pltpu.VMEM((hb, seq, 2 * HEAD_DIM), BF16)],
        compiler_params=_params("parallel", "parallel", "arbitrary"), name="moba",
    )(proj, proj, proj, cos, sin, rot)


def _compress_kernel(x_ref, pe_ref, w1_ref, w2_ref, o_ref, xf_ref):
    seq = x_ref.shape[0]
    n_sub = seq // NSA_CMP_STRIDE
    xf_ref[:seq, :] = x_ref[...].astype(F32)
    xf_ref[seq:, :] = jnp.zeros((NSA_CMP_STRIDE, HEAD_DIM), F32)
    hid = _dot(pe_ref[...], w1_ref[...])[0:1, :]
    for r in range(NSA_CMP_LEN):
        rows = xf_ref[pl.ds(r, n_sub, stride=NSA_CMP_STRIDE), :]
        hid = hid + _dot(rows.astype(BF16), w1_ref[r * HEAD_DIM:(r + 1) * HEAD_DIM, :])
    hid = hid * _sigmoid(hid)
    o_ref[...] = _dot(hid.astype(BF16), w2_ref[...]).astype(o_ref.dtype)


def _compress(proj, pe, w1, w2, batch, seq):
    n_sub = seq // NSA_CMP_STRIDE
    flat = NSA_CMP_LEN * HEAD_DIM
    pe_flat = jnp.zeros((2, SUBLANE * 2, flat), BF16).at[:, 0].set(pe.reshape(2, flat).astype(BF16))
    return pl.pallas_call(
        _compress_kernel, grid=(2, batch, NSA_GROUPS),
        in_specs=[pl.BlockSpec((None, seq, HEAD_DIM), lambda t, b, g: (BLK_NKC + 2 * t + g, b, 0)),
                  pl.BlockSpec((None, SUBLANE * 2, flat), lambda t, b, g: (t, 0, 0)),
                  pl.BlockSpec((None, flat, HEAD_DIM), lambda t, b, g: (t, 0, 0)),
                  pl.BlockSpec((None, HEAD_DIM, HEAD_DIM), lambda t, b, g: (t, 0, 0))],
        out_specs=pl.BlockSpec((None, None, None, n_sub, HEAD_DIM), lambda t, b, g: (t, b, g, 0, 0)),
        out_shape=jax.ShapeDtypeStruct((2, batch, NSA_GROUPS, n_sub, HEAD_DIM), BF16),
        scratch_shapes=[pltpu.VMEM((seq + NSA_CMP_STRIDE, HEAD_DIM), F32)],
        compiler_params=_params("parallel", "parallel", "parallel"), name="nsa_compress",
    )(proj, pe_flat, w1.astype(BF16), w2.astype(BF16))


def _nsa_cmp_kernel(q_ref, kc_ref, vc_ref, cov_ref, oc_ref, sel_ref, *, n_sel):
    qi = pl.program_id(2)
    tq = ATT_TQ
    n_cmp = kc_ref.shape[0]
    scale = HEAD_DIM ** -0.5
    q_pos = qi * tq + lax.broadcasted_iota(jnp.int32, (tq, n_cmp), 0)
    n_idx = lax.broadcasted_iota(jnp.int32, (tq, n_cmp), 1)
    ok = n_idx * NSA_CMP_STRIDE + (NSA_CMP_LEN - 1) <= q_pos
    kc = kc_ref[...]
    vc = vc_ref[...]
    p_sum = jnp.zeros((tq, n_cmp), F32)
    for r in range(NSA_REP):
        s = jnp.where(ok, _dot_nt(q_ref[r], kc) * scale, NEG)
        mx = jnp.max(s, axis=-1, keepdims=True)
        e = jnp.where(ok, jnp.exp(s - mx), 0.0)
        l = jnp.sum(e, axis=-1, keepdims=True)
        p = e * jnp.where(l > 0.0, 1.0 / l, 0.0)
        oc_ref[r] = _dot(p.astype(BF16), vc).astype(oc_ref.dtype)
        p_sum = p_sum + p

    n_rows = -(-n_sel // SUBLANE) * SUBLANE
    imp = lax.dot_general(cov_ref[...], p_sum, _NT, precision=lax.Precision.HIGHEST,
                          preferred_element_type=F32)[:n_rows, :]
    ridx = lax.broadcasted_iota(jnp.int32, (n_rows, tq), 0)
    blk = (qi * tq + lax.broadcasted_iota(jnp.int32, (n_rows, tq), 1)) // NSA_SEL_LEN
    forced = (ridx == 0) | (ridx == blk) | (ridx == blk - 1)
    x = jnp.where(forced, FORCE, imp)
    sel_ref[...] = _topk_rows(x, blk[0:1, :] + 1, min(NSA_TOPK, n_sel), n_sel).astype(sel_ref.dtype)


def _cover_matrix(n_cmp_pad, n_cmp, n_sel):
    c_start = np.arange(n_cmp_pad) * NSA_CMP_STRIDE
    s_start = np.arange(LANE) * NSA_SEL_LEN
    cover = (c_start[:, None] < s_start[None, :] + NSA_SEL_LEN) & (c_start[:, None] + NSA_CMP_LEN > s_start[None, :])
    cover &= (np.arange(n_cmp_pad) < n_cmp)[:, None] & (np.arange(LANE) < n_sel)[None, :]
    return jnp.asarray(cover.T, F32)


def _nsa_cmp(proj, kvc, batch, seq):
    nq = seq // ATT_TQ
    n_sub = seq // NSA_CMP_STRIDE
    n_sel = seq // NSA_SEL_LEN
    assert n_sel <= LANE
    m = batch * seq
    cover = _cover_matrix(n_sub, n_sub - NSA_CMP_LEN // NSA_CMP_STRIDE + 1, n_sel)
    cmp_spec = lambda t: pl.BlockSpec((None, None, None, n_sub, HEAD_DIM), lambda b, g, i: (t, b, g, 0, 0))
    return pl.pallas_call(
        functools.partial(_nsa_cmp_kernel, n_sel=n_sel),
        grid=(batch, NSA_GROUPS, nq),
        in_specs=[pl.BlockSpec((NSA_REP, ATT_TQ, HEAD_DIM), lambda b, g, i: (BLK_NQ // NSA_REP + g, b * nq + i, 0)),
                  cmp_spec(0), cmp_spec(1),
                  pl.BlockSpec((LANE, n_sub), lambda b, g, i: (0, 0))],
        out_specs=[pl.BlockSpec((NSA_REP, ATT_TQ, HEAD_DIM), lambda b, g, i: (g, b * nq + i, 0)),
                   pl.BlockSpec((ATT_TQ, LANE), lambda b, g, i: (b * nq + i, g))],
        out_shape=[jax.ShapeDtypeStruct((NSA_HEADS, m, HEAD_DIM), BF16),
                   jax.ShapeDtypeStruct((m, NSA_GROUPS * LANE), BF16)],
        compiler_params=_params("parallel", "parallel", "parallel"), name="nsa_cmp",
    )(proj, kvc, kvc, cover)


def _nsa_sw_kernel(q_ref, ks_ref, vs_ref, kw_ref, vw_ref, sel_ref, oc_ref, gl_ref, c_ref, s_ref, p_ref,
                   o_ref, ksr_ref, kwr_ref, vwp_ref, vse_ref, qr_ref, m_ref, acc_ref, ow_ref, *, seq):
    grp = pl.program_id(1)
    qi = pl.program_id(2)
    tq, tk = ATT_TQ, ATT_TK
    rep = NSA_REP
    win = NSA_WINDOW
    gk = SEL_GROUP * tk

    @pl.when(qi == 0)
    def _():
        kwr_ref[:win, :] = jnp.zeros((win, HEAD_DIM), BF16)
        vwp_ref[:win, :] = jnp.zeros((win, 2 * HEAD_DIM), BF16)
        vwp_ref[win:, HEAD_DIM:] = jnp.ones((seq, HEAD_DIM), BF16)
        vse_ref[:, :HEAD_DIM] = vs_ref[...]
        vse_ref[:, HEAD_DIM:] = jnp.ones((seq, HEAD_DIM), BF16)
        for j in range(seq // tk):
            rows = slice(j * tk, (j + 1) * tk)
            shifted = slice(win + j * tk, win + (j + 1) * tk)
            c, s = c_ref[rows, :], s_ref[rows, :]
            ksr_ref[rows, :] = _rope_partial(ks_ref[rows, :], c, s, p_ref).astype(BF16)
            kwr_ref[shifted, :] = _rope_partial(kw_ref[rows, :], c, s, p_ref).astype(BF16)
            vwp_ref[shifted, :HEAD_DIM] = vw_ref[rows, :]

    r0 = pl.multiple_of(qi * tq, tq)
    cq = c_ref[pl.ds(r0, tq), :]
    sq = s_ref[pl.ds(r0, tq), :]
    for r in range(rep):
        qr_ref[r] = (_rope_partial(q_ref[r], cq, sq, p_ref) * Q_SCALE).astype(BF16)

    q_all = qr_ref[...].reshape(rep * tq, HEAD_DIM)

    row_w = lax.broadcasted_iota(jnp.int32, (tq, win + tq), 0)
    col_w = lax.broadcasted_iota(jnp.int32, (tq, win + tq), 1)
    in_win = (col_w > row_w) & (col_w <= row_w + win) & (col_w >= win - r0)
    s = _dot_nt(q_all, kwr_ref[pl.ds(r0, win + tq), :]).reshape(rep, tq, win + tq)
    s = jnp.where(in_win[None], s, NEG)
    p = jnp.exp2((s - jnp.max(s, axis=-1, keepdims=True)).astype(BF16))
    o_w = _dot(p.reshape(rep * tq, win + tq), vwp_ref[pl.ds(r0, win + tq), :]).reshape(rep, tq, 2 * HEAD_DIM)
    ow_ref[...] = o_w[..., :HEAD_DIM] / o_w[..., HEAD_DIM:HEAD_DIM + 1]

    sel = sel_ref[...]
    row = lax.broadcasted_iota(jnp.int32, (tq, gk), 0)
    col = lax.broadcasted_iota(jnp.int32, (tq, gk), 1)

    def chosen_keys(jg):
        blk_row = lax.broadcasted_iota(jnp.int32, (LANE, gk), 0)
        blk_col = jg * (gk // NSA_SEL_LEN) + lax.broadcasted_iota(jnp.int32, (LANE, gk), 1) // NSA_SEL_LEN
        return _dot(sel, jnp.where(blk_row == blk_col, 1.0, 0.0).astype(BF16))

    def attend(off, valid, first):
        s = _dot_nt(q_all, ksr_ref[pl.ds(off, gk), :]).reshape(rep, tq, gk)
        s = jnp.where(valid[None], s, NEG)
        m_new = jnp.max(s, axis=-1, keepdims=True)
        if not first:
            m_old = m_ref[...]
            m_new = jnp.maximum(m_old, m_new)
        p = jnp.exp2((s - m_new).astype(BF16))
        v_grp = vse_ref[pl.ds(off, gk), :]
        for r in range(rep):
            pv = _dot(p[r], v_grp)
            if first:
                acc_ref[r] = pv
            else:
                acc_ref[r] = jnp.exp2(m_old[r] - m_new[r]) * acc_ref[r] + pv
        m_ref[...] = m_new

    gd = qi // SEL_GROUP
    off_d = pl.multiple_of(gd * gk, gk)
    causal = col + off_d <= row + r0
    attend(off_d, jnp.where(causal, chosen_keys(gd), 0.0) > 0.5, True)

    def sel_body(jg, carry):
        attend(pl.multiple_of(jg * gk, gk), chosen_keys(jg) > 0.5, False)
        return carry

    lax.fori_loop(0, gd, sel_body, 0)

    gates = _sigmoid(gl_ref[...].astype(F32))
    lane = lax.broadcasted_iota(jnp.int32, (tq, LANE), 1)

    def gate_col(c):
        return jnp.sum(jnp.where(lane == c, gates, 0.0), axis=-1, keepdims=True)

    for r in range(rep):
        base = (grp * rep + r) * 3
        o_s = acc_ref[r, :, :HEAD_DIM] / acc_ref[r, :, HEAD_DIM:HEAD_DIM + 1]
        o = gate_col(base) * oc_ref[r].astype(F32) + gate_col(base + 1) * o_s + gate_col(base + 2) * ow_ref[r]
        o_ref[:, r * HEAD_DIM:(r + 1) * HEAD_DIM] = o.astype(o_ref.dtype)


def _nsa_sw(proj, sel, o_cmp, cos, sin, rot, batch, seq):
    nq = seq // ATT_TQ
    assert seq % (SEL_GROUP * ATT_TK) == 0 and NSA_WINDOW % ATT_TK == 0 and ATT_TQ == ATT_TK
    m = batch * seq
    rep = NSA_REP
    head = lambda off: pl.BlockSpec((None, seq, HEAD_DIM), lambda b, g, i: (off + g, b, 0))
    table = pl.BlockSpec((seq, LANE), lambda b, g, i: (b, 0))
    q_like = lambda off: pl.BlockSpec((rep, ATT_TQ, HEAD_DIM), lambda b, g, i: (off + g, b * nq + i, 0))
    return pl.pallas_call(
        functools.partial(_nsa_sw_kernel, seq=seq),
        grid=(batch, NSA_GROUPS, nq),
        in_specs=[q_like(BLK_NQ // rep), head(BLK_NKS), head(BLK_NVS), head(BLK_NKW), head(BLK_NVW),
                  pl.BlockSpec((ATT_TQ, LANE), lambda b, g, i: (b * nq + i, g)),
                  q_like(0),
                  pl.BlockSpec((None, ATT_TQ, LANE), lambda b, g, i: (BLK_NG, b * nq + i, 0)),
                  table, table,
                  pl.BlockSpec((HEAD_DIM, HEAD_DIM), lambda b, g, i: (0, 0))],
        out_specs=pl.BlockSpec((ATT_TQ, rep * HEAD_DIM), lambda b, g, i: (b * nq + i, g)),
        out_shape=jax.ShapeDtypeStruct((m, NSA_HEADS * HEAD_DIM), BF16),
        scratch_shapes=[pltpu.VMEM((seq, HEAD_DIM), BF16),
                        pltpu.VMEM((seq + NSA_WINDOW, HEAD_DIM), BF16),
                        pltpu.VMEM((seq + NSA_WINDOW, 2 * HEAD_DIM), BF16),
                        pltpu.VMEM((seq, 2 * HEAD_DIM), BF16),
                        pltpu.VMEM((rep, ATT_TQ, HEAD_DIM), BF16),
                        pltpu.VMEM((rep, ATT_TQ, 1), F32),
                        pltpu.VMEM((rep, ATT_TQ, 2 * HEAD_DIM), F32), pltpu.VMEM((rep, ATT_TQ, HEAD_DIM), F32)],
        compiler_params=_params("parallel", "parallel", "arbitrary"), name="nsa_sel_win",
    )(proj, proj, proj, proj, proj, sel, o_cmp, proj, cos, sin, rot)


def _rope_full(x_bf16, c, s):
    half = x_bf16.shape[-1] // 2
    x = x_bf16.astype(F32)
    x1, x2 = x[:, :half], x[:, half:]
    return jnp.concatenate([x1 * c - x2 * s, x1 * s + x2 * c], axis=-1)


def _retention_kernel(q_ref, k_ref, v_ref, g_ref, c_ref, s_ref, lg_ref, gn_ref, o_ref, state_ref):
    ci = pl.program_id(1)
    ch = q_ref.shape[0]

    @pl.when(ci == 0)
    def _():
        state_ref[...] = jnp.zeros_like(state_ref)

    c, s = c_ref[...], s_ref[...]
    n_row = lax.broadcasted_iota(jnp.int32, (ch, ch), 0)
    n_col = lax.broadcasted_iota(jnp.int32, (ch, ch), 1)
    diff = (n_row - n_col).astype(F32)
    n_vec = lax.broadcasted_iota(jnp.int32, (ch, 1), 0).astype(F32)

    for h in range(RET_HEADS):
        log_g = lg_ref[h][:, 0:1]
        qk_cols = slice(h * RET_DK, (h + 1) * RET_DK)
        v_cols = slice(h * RET_DV, (h + 1) * RET_DV)
        q = _rope_full(q_ref[:, qk_cols], c, s)
        k = _rope_full(k_ref[:, qk_cols], c, s) * (RET_DK ** -0.5)
        v = v_ref[:, v_cols]

        decay = jnp.where(diff >= 0.0, jnp.exp(jnp.maximum(diff, 0.0) * log_g), 0.0)
        q_dec = jnp.exp((n_vec + 1.0) * log_g)
        k_dec = jnp.exp((ch - 1.0 - n_vec) * log_g)
        c_dec = jnp.exp(ch * log_g)

        qb = q.astype(BF16)
        state = state_ref[h]
        scores = _dot_nt(qb, k.astype(BF16)) * decay
        y = _dot(scores.astype(BF16), v) + _dot(qb, state.astype(BF16)) * q_dec
        state_ref[h] = state * c_dec + lax.dot_general((k * k_dec).astype(BF16), v, _TN,
                                                       preferred_element_type=F32)

        mu = jnp.mean(y, axis=-1, keepdims=True)
        yc = y - mu
        var = jnp.mean(yc * yc, axis=-1, keepdims=True)
        yn = yc * lax.rsqrt(var + RMS_EPS) * gn_ref[h]
        gate = g_ref[:, v_cols].astype(F32)
        o_ref[:, v_cols] = (gate * _sigmoid(gate) * yn).astype(o_ref.dtype)


def _retention(proj, cos_r, sin_r, gn, batch, seq):
    ch = RET_CHUNK
    nch = seq // ch
    assert seq % ch == 0
    m = batch * seq
    log_g = jnp.log(1.0 - jnp.exp2(-5.0 - jnp.arange(RET_HEADS, dtype=F32)))
    log_g = jnp.broadcast_to(log_g[:, None, None], (RET_HEADS, 1, LANE))
    qk_w = RET_HEADS * RET_DK
    v_w = RET_HEADS * RET_DV
    assert v_w == 2 * qk_w
    rows = lambda b, c: b * nch + c
    return pl.pallas_call(
        _retention_kernel, grid=(batch, nch),
        in_specs=[pl.BlockSpec((ch, qk_w), lambda b, c: (rows(b, c), 0)),
                  pl.BlockSpec((ch, qk_w), lambda b, c: (rows(b, c), 1)),
                  pl.BlockSpec((ch, v_w), lambda b, c: (rows(b, c), 1)),
                  pl.BlockSpec((ch, v_w), lambda b, c: (rows(b, c), 2)),
                  pl.BlockSpec((ch, LANE), lambda b, c: (rows(b, c), 0)),
                  pl.BlockSpec((ch, LANE), lambda b, c: (rows(b, c), 0)),
                  pl.BlockSpec((RET_HEADS, 1, LANE), lambda b, c: (0, 0, 0)),
                  pl.BlockSpec((RET_HEADS, 1, RET_DV), lambda b, c: (0, 0, 0))],
        out_specs=pl.BlockSpec((ch, v_w), lambda b, c: (rows(b, c), 0)),
        out_shape=jax.ShapeDtypeStruct((m, v_w), BF16),
        scratch_shapes=[pltpu.VMEM((RET_HEADS, RET_DK, RET_DV), F32)],
        compiler_params=_params("parallel", "arbitrary"), name="retention",
    )(proj, proj, proj, proj, cos_r, sin_r, log_g, gn.reshape(RET_HEADS, 1, RET_DV).astype(F32))


def _cross_kernel(h_ref, g_ref, wq_ref, kv_ref, wo_ref, o_ref):
    h = h_ref[...]
    hn = _rms_rows(h, g_ref[...]).astype(BF16)
    q = (_dot(hn, wq_ref[...]) * Q_SCALE).astype(BF16)
    kv_cols = X_HEADS * HEAD_DIM
    outs = []
    for hd in range(X_HEADS):
        cols = slice(hd * HEAD_DIM, (hd + 1) * HEAD_DIM)
        s = _dot_nt(q[:, cols], kv_ref[:, cols])
        mx = jnp.max(s, axis=-1, keepdims=True)
        p = jnp.exp2(s - mx)
        l = jnp.sum(p, axis=-1, keepdims=True)
        p = p / l
        outs.append(_dot(p.astype(BF16), kv_ref[:, kv_cols + hd * HEAD_DIM:kv_cols + (hd + 1) * HEAD_DIM]))
    o = jnp.concatenate(outs, axis=-1).astype(BF16)
    o_ref[...] = h + _dot(o, wo_ref[...])


def _cross(h, g, wq, kv, wo, layer, batch, seq, n_mem, *, tm):
    m, d = h.shape
    per_b = seq // tm
    kv_cols = 2 * X_HEADS * HEAD_DIM
    return pl.pallas_call(
        _cross_kernel, grid=(m // tm,),
        in_specs=[pl.BlockSpec((tm, d), lambda i: (i, 0)),
                  pl.BlockSpec((1, d), lambda i: (0, 0)),
                  pl.BlockSpec((d, X_HEADS * HEAD_DIM), lambda i: (0, 0)),
                  pl.BlockSpec((n_mem, kv_cols), lambda i: (i // per_b, layer)),
                  pl.BlockSpec((X_HEADS * HEAD_DIM, d), lambda i: (0, 0))],
        out_specs=pl.BlockSpec((tm, d), lambda i: (i, 0)),
        out_shape=jax.ShapeDtypeStruct((m, d), F32),
        compiler_params=_params("parallel"), name="cross_attn",
    )(h, g.reshape(1, d), wq, kv, wo)


def _ffn_kernel(h_ref, halo_ref, g_ref, wg_ref, wv_ref, cwg_ref, cwv_ref, cbg_ref, cbv_ref, wd_ref,
                o_ref, hn_ref, acc_ref, *, per_b):
    i = pl.program_id(0)
    f = pl.program_id(1)
    tm = h_ref.shape[0]
    pad = SUBLANE * 2

    @pl.when(f == 0)
    def _():
        g = g_ref[...]
        hn_ref[pad:, :] = _rms_rows(h_ref[...], g).astype(BF16)
        halo = _rms_rows(halo_ref[...], g)
        halo = jnp.where(i % per_b == 0, 0.0, halo)
        hn_ref[:pad, :] = halo.astype(BF16)
        acc_ref[...] = jnp.zeros_like(acc_ref)

    hn = hn_ref[...]

    def conv(u, w, b):
        return (b + w[0:1, :] * u[pad - 2:pad - 2 + tm] + w[1:2, :] * u[pad - 1:pad - 1 + tm]
                + w[2:3, :] * u[pad:pad + tm])

    gate = conv(_dot(hn, wg_ref[...]), cwg_ref[...], cbg_ref[...])
    val = conv(_dot(hn, wv_ref[...]), cwv_ref[...], cbv_ref[...])
    act = (gate * _sigmoid(gate) * val).astype(BF16)
    acc_ref[...] += _dot(act, wd_ref[...])

    @pl.when(f == pl.num_programs(1) - 1)
    def _():
        o_ref[...] = h_ref[...] + acc_ref[...]


def _ffn(h, g, w_up, conv_w, conv_b, w_down, batch, seq, *, tm, tf):
    m, d = h.shape
    dff = w_down.shape[0]
    assert seq % tm == 0 and dff % tf == 0
    per_b = seq // tm
    nf = dff // tf
    pad = SUBLANE * 2
    halo_blocks = tm // pad
    cw = jnp.zeros((SUBLANE, 2 * dff), F32).at[:CONV_WIDTH].set(conv_w)
    cb = conv_b.reshape(1, 2 * dff)
    return pl.pallas_call(
        functools.partial(_ffn_kernel, per_b=per_b),
        grid=(m // tm, nf),
        in_specs=[pl.BlockSpec((tm, d), lambda i, f: (i, 0)),
                  pl.BlockSpec((pad, d), lambda i, f: (jnp.maximum(i * halo_blocks - 1, 0), 0)),
                  pl.BlockSpec((1, d), lambda i, f: (0, 0)),
                  pl.BlockSpec((d, tf), lambda i, f: (0, f)),
                  pl.BlockSpec((d, tf), lambda i, f: (0, nf + f)),
                  pl.BlockSpec((SUBLANE, tf), lambda i, f: (0, f)),
                  pl.BlockSpec((SUBLANE, tf), lambda i, f: (0, nf + f)),
                  pl.BlockSpec((1, tf), lambda i, f: (0, f)),
                  pl.BlockSpec((1, tf), lambda i, f: (0, nf + f)),
                  pl.BlockSpec((tf, d), lambda i, f: (f, 0))],
        out_specs=pl.BlockSpec((tm, d), lambda i, f: (i, 0), pipeline_mode=pl.Buffered(1)),
        out_shape=jax.ShapeDtypeStruct((m, d), F32),
        scratch_shapes=[pltpu.VMEM((tm + pad, d), BF16), pltpu.VMEM((tm, d), F32)],
        compiler_params=_params("parallel", "arbitrary"), name="conv_ffn",
    )(h, h, g.reshape(1, d), w_up, w_up, cw, cw, cb, cb, w_down)


def _final_norm_kernel(x_ref, g_ref, o_ref):
    o_ref[...] = _rms_rows(x_ref[...], g_ref[...])


def _final_norm(x, g, *, tm):
    m, d = x.shape
    return pl.pallas_call(
        _final_norm_kernel, grid=(m // tm,),
        in_specs=[pl.BlockSpec((tm, d), lambda i: (i, 0)), pl.BlockSpec((1, d), lambda i: (0, 0))],
        out_specs=pl.BlockSpec((tm, d), lambda i: (i, 0)),
        out_shape=jax.ShapeDtypeStruct((m, d), F32),
        compiler_params=_params("parallel"), name="final_norm",
    )(x, g.reshape(1, d))


def _row_tile(m, want):
    return want if m % want == 0 else m


def _mixer_ab(h, g, w_in, pe, w1, w2, w_out, tabs, rot, batch, seq):
    m, d = h.shape
    cos, sin = tabs[0], tabs[1]
    w_pad = jnp.zeros((d, AB_BLOCKS * LANE), BF16).at[:, :AB_COLS].set(w_in.astype(BF16))
    proj = _norm_matmul(h, g, w_pad, tm=_row_tile(m, 1024), tn=2048, head_major=True)
    o_moba = _moba(proj, cos, sin, rot, batch, seq)
    kvc = _compress(proj, pe, w1, w2, batch, seq)
    o_cmp, sel = _nsa_cmp(proj, kvc, batch, seq)
    o_nsa = _nsa_sw(proj, sel, o_cmp, cos, sin, rot, batch, seq)
    return _matmul_res([o_moba, o_nsa], w_out.astype(BF16), h, tm=_row_tile(m, 1024), tn=1024)


def _mixer_c(h, g, w_in, gn, w_out, tabs, batch, seq):
    m, d = h.shape
    proj = _norm_matmul(h, g, w_in.astype(BF16), tm=_row_tile(m, 1024), tn=2048)
    y = _retention(proj, tabs[2], tabs[3], gn, batch, seq)
    return _matmul_res([y], w_out.astype(BF16), h, tm=_row_tile(m, 1024), tn=1024)


def kernel(x, mem, positions, norm_mix, norm_cross, norm_ffn, norm_mem, norm_final, w_in_ab, cmp_pe_k, cmp_w1_k,
           cmp_w2_k, cmp_pe_v, cmp_w1_v, cmp_w2_v, w_out_ab, w_in_c, ret_gn, w_out_c, w_q_x, w_kv_x, w_o_x, w_up,
           conv_w, conv_b, w_down):
    batch, seq, d = x.shape
    n_mem = mem.shape[1]
    depth = norm_mix.shape[0]
    m = batch * seq
    tabs = _rope_tables(positions)
    rot = _rot_matrix()

    kv_cols = 2 * X_HEADS * HEAD_DIM
    w_kv = jnp.transpose(w_kv_x, (1, 0, 2)).reshape(d, depth * kv_cols).astype(BF16)
    kv = _norm_matmul(mem.reshape(batch * n_mem, d), norm_mem, w_kv, tm=_row_tile(batch * n_mem, 512), tn=512)

    h = x.reshape(m, d)
    for l in range(depth):
        if l % 2 == 0:
            e = l // 2
            h = _mixer_ab(h, norm_mix[l], w_in_ab[e],
                          jnp.stack([cmp_pe_k[e], cmp_pe_v[e]]), jnp.stack([cmp_w1_k[e], cmp_w1_v[e]]),
                          jnp.stack([cmp_w2_k[e], cmp_w2_v[e]]), w_out_ab[e], tabs, rot, batch, seq)
        else:
            o = l // 2
            h = _mixer_c(h, norm_mix[l], w_in_c[o], ret_gn[o], w_out_c[o], tabs, batch, seq)
        h = _cross(h, norm_cross[l], w_q_x[l].astype(BF16), kv, w_o_x[l].astype(BF16), l, batch, seq, n_mem,
                   tm=_row_tile(seq, 512))
        h = _ffn(h, norm_ffn[l], w_up[l].astype(BF16), conv_w[l], conv_b[l], w_down[l].astype(BF16), batch, seq,
                 tm=_row_tile(seq, 1024), tf=512)
    return _final_norm(h, norm_final, tm=_row_tile(m, 512)).reshape(batch, seq, d)
```

```python
import functools

import numpy as np
import jax
import jax.numpy as jnp
from jax import lax
from jax.experimental import pallas as pl
from jax.experimental.pallas import tpu as pltpu

F32 = jnp.float32
BF16 = jnp.bfloat16

D_MODEL = 2048
DEPTH = 4
HEAD_DIM = 128
ROPE_THETA = 500000.0
ROPE_DIM = HEAD_DIM // 4
X_HEADS = 4
MOBA_HEADS = 8
MOBA_BLOCK = 256
MOBA_TOPK = 3
NSA_HEADS = 8
NSA_GROUPS = 2
NSA_REP = NSA_HEADS // NSA_GROUPS
NSA_CMP_LEN = 32
NSA_CMP_STRIDE = 16
NSA_SEL_LEN = 64
NSA_TOPK = 16
NSA_WINDOW = 512
RET_HEADS = 8
RET_DK = 256
RET_DV = 512
RET_THETA = 10000.0
D_FF = 5632
CONV_WIDTH = 3
RMS_EPS = 1e-6
NEG = -1e30
FORCE = 1e9
LOG2E = 1.4426950408889634
Q_SCALE = HEAD_DIM ** -0.5 * LOG2E

LANE = 128
SUBLANE = 8
VMEM_LIMIT = 56 * 2 ** 20

AB_SIZES = (MOBA_HEADS * HEAD_DIM,) * 3 + (NSA_HEADS * HEAD_DIM,) + (NSA_GROUPS * HEAD_DIM,) * 6 + (NSA_HEADS * 3,)
AB_COLS = sum(AB_SIZES)
AB_BLOCKS = 48
BLK_MQ, BLK_MK, BLK_MV, BLK_NQ = 0, 8, 16, 24
BLK_NKC, BLK_NVC, BLK_NKS, BLK_NVS, BLK_NKW, BLK_NVW, BLK_NG = 32, 34, 36, 38, 40, 42, 44

ATT_TQ = 256
ATT_TK = 256
RET_CHUNK = 128
MOBA_GROUP = 4
MOBA_HB = 4
SEL_GROUP = 4

_NT = (((1,), (1,)), ((), ()))
_TN = (((0,), (0,)), ((), ()))


def _params(*sem):
    return pltpu.CompilerParams(dimension_semantics=sem, vmem_limit_bytes=VMEM_LIMIT)


def _dot(a, b):
    return jnp.dot(a, b, preferred_element_type=F32)


def _dot_nt(a, b):
    return lax.dot_general(a, b, _NT, preferred_element_type=F32)


def _sigmoid(x):
    return 1.0 / (1.0 + jnp.exp(-x))


def _rms_rows(x, g):
    ms = jnp.mean(x * x, axis=-1, keepdims=True)
    return x * lax.rsqrt(ms + RMS_EPS) * g


def _tables_kernel(pos_ref, inv_rope_ref, inv_ret_ref, c_ref, s_ref, cr_ref, sr_ref):
    pos = pos_ref[...]
    lane = lax.broadcasted_iota(jnp.int32, pos.shape, 1)
    rot = lane < ROPE_DIM
    ang = pos * inv_rope_ref[...]
    c_ref[...] = jnp.where(rot, jnp.cos(ang), 1.0)
    s_ref[...] = jnp.where(rot, jnp.sin(ang), 0.0)
    ang_r = pos * inv_ret_ref[...]
    cr_ref[...] = jnp.cos(ang_r)
    sr_ref[...] = jnp.sin(ang_r)


def _rope_tables(positions):
    m = positions.size
    posb = jnp.broadcast_to(positions.reshape(m, 1).astype(F32), (m, LANE))
    half = ROPE_DIM // 2
    inv = jnp.float32(ROPE_THETA) ** (-jnp.arange(half, dtype=F32) / half)
    inv_rope = jnp.concatenate([inv, inv, jnp.zeros((LANE - ROPE_DIM,), F32)]).reshape(1, LANE)
    half_r = RET_DK // 2
    inv_ret = (jnp.float32(RET_THETA) ** (-jnp.arange(half_r, dtype=F32) / half_r)).reshape(1, LANE)
    tm = 1024 if m % 1024 == 0 else m
    row = pl.BlockSpec((tm, LANE), lambda i: (i, 0))
    vec = pl.BlockSpec((1, LANE), lambda i: (0, 0))
    sds = jax.ShapeDtypeStruct((m, LANE), F32)
    return pl.pallas_call(
        _tables_kernel, grid=(m // tm,), in_specs=[row, vec, vec], out_specs=[row] * 4,
        out_shape=[sds] * 4, compiler_params=_params("parallel"), name="rope_tables",
    )(posb, inv_rope, inv_ret)


def _rot_matrix():
    half = ROPE_DIM // 2
    p = np.zeros((HEAD_DIM, HEAD_DIM), np.float32)
    for l in range(half):
        p[l + half, l] = -1.0
        p[l, l + half] = 1.0
    return jnp.asarray(p, BF16)


def _rope_partial(x_bf16, c, s, p_ref):
    return x_bf16.astype(F32) * c + _dot(x_bf16, p_ref[...]) * s


def _norm_matmul_kernel(x_ref, g_ref, w_ref, o_ref, xn_ref, *, head_major):
    @pl.when(pl.program_id(1) == 0)
    def _():
        xn_ref[...] = _rms_rows(x_ref[...], g_ref[...]).astype(BF16)

    acc = _dot(xn_ref[...], w_ref[...])
    if head_major:
        for c in range(o_ref.shape[0]):
            o_ref[c] = acc[:, c * LANE:(c + 1) * LANE].astype(o_ref.dtype)
    else:
        o_ref[...] = acc.astype(o_ref.dtype)


def _norm_matmul(x, g, w, *, tm, tn, head_major=False):
    m, k = x.shape
    n = w.shape[1]
    assert m % tm == 0 and n % tn == 0 and tn % LANE == 0
    if head_major:
        out_shape = jax.ShapeDtypeStruct((n // LANE, m, LANE), BF16)
        out_spec = pl.BlockSpec((tn // LANE, tm, LANE), lambda i, j: (j, i, 0))
    else:
        out_shape = jax.ShapeDtypeStruct((m, n), BF16)
        out_spec = pl.BlockSpec((tm, tn), lambda i, j: (i, j))
    return pl.pallas_call(
        functools.partial(_norm_matmul_kernel, head_major=head_major),
        grid=(m // tm, n // tn),
        in_specs=[pl.BlockSpec((tm, k), lambda i, j: (i, 0)),
                  pl.BlockSpec((1, k), lambda i, j: (0, 0)),
                  pl.BlockSpec((k, tn), lambda i, j: (0, j))],
        out_specs=out_spec, out_shape=out_shape,
        scratch_shapes=[pltpu.VMEM((tm, k), BF16)],
        compiler_params=_params("parallel", "arbitrary"), name="norm_matmul",
    )(x, g.reshape(1, k), w)


def _matmul_res_kernel(*refs):
    r_ref, o_ref = refs[-2], refs[-1]
    acc = r_ref[...]
    for x_ref, w_ref in zip(refs[:-2:2], refs[1:-2:2]):
        acc = acc + _dot(x_ref[...], w_ref[...])
    o_ref[...] = acc


def _matmul_res(xs, w, res, *, tm, tn):
    m, n = res.shape
    k = xs[0].shape[1]
    assert m % tm == 0 and n % tn == 0 and all(x.shape[1] == k for x in xs) and w.shape[0] == k * len(xs)
    in_specs, args = [], []
    for idx, x in enumerate(xs):
        in_specs += [pl.BlockSpec((tm, k), lambda i, j: (i, 0)),
                     pl.BlockSpec((k, tn), lambda i, j, idx=idx: (idx, j))]
        args += [x, w]
    return pl.pallas_call(
        _matmul_res_kernel, grid=(m // tm, n // tn),
        in_specs=in_specs + [pl.BlockSpec((tm, tn), lambda i, j: (i, j))],
        out_specs=pl.BlockSpec((tm, tn), lambda i, j: (i, j)),
        out_shape=jax.ShapeDtypeStruct((m, n), F32),
        compiler_params=_params("parallel", "arbitrary"), name="matmul_res",
    )(*args, res)


def _topk_rows(x, n_valid, k, n_rows, transpose=True):
    rows, nq = x.shape
    ridx = lax.broadcasted_iota(jnp.int32, (rows, nq), 0)
    x = jnp.where(ridx < n_valid, x, NEG)
    x = jnp.where(ridx < n_rows, x, -jnp.inf)
    rank = jnp.zeros((rows, nq), F32)
    for i in range(n_rows):
        xi = x[i:i + 1, :]
        rank = rank + jnp.where(ridx > i, jnp.where(xi >= x, 1.0, 0.0), jnp.where(xi > x, 1.0, 0.0))
    marks = jnp.where(rank < float(k), jnp.where(x > NEG / 2, 1.0, 0.0), 0.0)
    if not transpose:
        return marks
    if rows < LANE:
        marks = jnp.concatenate([marks, jnp.zeros((LANE - rows, nq), F32)], axis=0)
    return marks.T


def _moba_kernel(q_ref, k_ref, v_ref, c_ref, s_ref, p_ref, o_ref, krot_ref, kmean_ref, vt_ref, marks_ref, *, nb):
    qi = pl.program_id(2)
    blk = MOBA_BLOCK
    nb_pad = -(-nb // SUBLANE) * SUBLANE
    grp = MOBA_GROUP
    heads = range(MOBA_HB)

    @pl.when(qi == 0)
    def _():
        kmean_ref[...] = jnp.zeros_like(kmean_ref)
        for hh in heads:
            for j in range(nb):
                rows = slice(j * blk, (j + 1) * blk)
                kr = _rope_partial(k_ref[hh, rows, :], c_ref[rows, :], s_ref[rows, :], p_ref)
                krot_ref[hh, rows, :] = kr.astype(BF16)
                kmean_ref[hh, j:j + 1, :] = jnp.mean(kr, axis=0, keepdims=True)
                vt_ref[hh, j, :HEAD_DIM, :] = v_ref[hh, rows, :].astype(F32).T.astype(BF16)
                vt_ref[hh, j, HEAD_DIM:, :] = jnp.ones((HEAD_DIM, blk), BF16)

    r0 = pl.multiple_of(qi * blk, blk)
    cq = c_ref[pl.ds(r0, blk), :]
    sq = s_ref[pl.ds(r0, blk), :]
    key = lax.broadcasted_iota(jnp.int32, (blk, blk), 0)
    qry = lax.broadcasted_iota(jnp.int32, (blk, blk), 1)

    qts, state = [], []
    for hh in heads:
        q = _rope_partial(q_ref[hh], cq, sq, p_ref)
        gate_t = lax.dot_general(kmean_ref[hh], q, _NT, precision=lax.Precision.HIGHEST,
                                 preferred_element_type=F32)[:nb_pad, :]
        marks_ref[hh] = _topk_rows(gate_t, qi, min(MOBA_TOPK, nb), nb, transpose=False)
        qt = (q * Q_SCALE).T.astype(BF16)
        qts.append(qt)
        st = jnp.where(key <= qry, _dot(krot_ref[hh, pl.ds(r0, blk), :], qt), NEG)
        m = jnp.max(st, axis=0, keepdims=True)
        pt = jnp.exp2((st - m).astype(BF16))
        state += [m, _dot(vt_ref[hh, qi], pt)]

    def body(jg, carry):
        off = pl.multiple_of(jg * (grp * blk), grp * blk)
        out = []
        for hh in heads:
            m, acc = carry[2 * hh:2 * hh + 2]
            st = _dot(krot_ref[hh, pl.ds(off, grp * blk), :], qts[hh])
            parts = []
            for t in range(grp):
                chosen = marks_ref[hh, pl.ds(jg * grp + t, 1), :]
                parts.append(jnp.where(chosen > 0.5, st[t * blk:(t + 1) * blk], NEG))
            st = jnp.concatenate(parts, axis=0)
            m_new = jnp.maximum(m, jnp.max(st, axis=0, keepdims=True))
            pt = jnp.exp2((st - m_new).astype(BF16))
            pv = [_dot(vt_ref[hh, jg * grp + t], pt[t * blk:(t + 1) * blk]) for t in range(grp)]
            while len(pv) > 1:
                pv = [pv[i] + pv[i + 1] for i in range(0, len(pv) - 1, 2)] + pv[len(pv) - len(pv) % 2:]
            out += [m_new, jnp.exp2(m - m_new) * acc + pv[0]]
        return tuple(out)

    state = lax.fori_loop(0, (qi + grp - 1) // grp, body, tuple(state))
    for hh in heads:
        acc = state[2 * hh + 1]
        o_t = acc[:HEAD_DIM] / acc[HEAD_DIM:HEAD_DIM + 1]
        o_ref[:, hh * HEAD_DIM:(hh + 1) * HEAD_DIM] = o_t.T.astype(o_ref.dtype)


def _moba(proj, cos, sin, rot, batch, seq):
    nb = seq // MOBA_BLOCK
    hb = MOBA_HB
    assert seq % (MOBA_BLOCK * MOBA_GROUP) == 0 and nb <= LANE
    assert MOBA_HEADS % hb == 0 and BLK_MQ % hb == 0 and BLK_MK % hb == 0 and BLK_MV % hb == 0
    m = batch * seq
    head = lambda off: pl.BlockSpec((hb, seq, HEAD_DIM), lambda b, h, i: (off // hb + h, b, 0))
    table = pl.BlockSpec((seq, LANE), lambda b, h, i: (b, 0))
    return pl.pallas_call(
        functools.partial(_moba_kernel, nb=nb),
        grid=(batch, MOBA_HEADS // hb, nb),
        in_specs=[pl.BlockSpec((hb, MOBA_BLOCK, HEAD_DIM), lambda b, h, i: (BLK_MQ // hb + h, b * nb + i, 0)),
                  head(BLK_MK), head(BLK_MV), table, table,
                  pl.BlockSpec((HEAD_DIM, HEAD_DIM), lambda b, h, i: (0, 0))],
        out_specs=pl.BlockSpec((MOBA_BLOCK, hb * HEAD_DIM), lambda b, h, i: (b * nb + i, h)),
        out_shape=jax.ShapeDtypeStruct((m, MOBA_HEADS * HEAD_DIM), BF16),
        scratch_shapes=[pltpu.VMEM((hb, seq, HEAD_DIM), BF16), pltpu.VMEM((hb, LANE, HEAD_DIM), F32),
                        pltpu.VMEM((hb, nb, 2 * HEAD_DIM, MOBA_BLOCK), BF16),
                        pltpu.VMEM((hb, -(-nb // SUBLANE) * SUBLANE, MOBA_BLOCK), F32)],
        compiler_params=_params("parallel", "parallel", "arbitrary"), name="moba",
    )(proj, proj, proj, cos, sin, rot)


def _compress_kernel(x_ref, pe_ref, w1_ref, w2_ref, o_ref, xf_ref):
    seq = x_ref.shape[0]
    n_sub = seq // NSA_CMP_STRIDE
    xf_ref[:seq, :] = x_ref[...].astype(F32)
    xf_ref[seq:, :] = jnp.zeros((NSA_CMP_STRIDE, HEAD_DIM), F32)
    hid = _dot(pe_ref[...], w1_ref[...])[0:1, :]
    for r in range(NSA_CMP_LEN):
        rows = xf_ref[pl.ds(r, n_sub, stride=NSA_CMP_STRIDE), :]
        hid = hid + _dot(rows.astype(BF16), w1_ref[r * HEAD_DIM:(r + 1) * HEAD_DIM, :])
    hid = hid * _sigmoid(hid)
    o_ref[...] = _dot(hid.astype(BF16), w2_ref[...]).astype(o_ref.dtype)


def _compress(proj, pe, w1, w2, batch, seq):
    n_sub = seq // NSA_CMP_STRIDE
    flat = NSA_CMP_LEN * HEAD_DIM
    pe_flat = jnp.zeros((2, SUBLANE * 2, flat), BF16).at[:, 0].set(pe.reshape(2, flat).astype(BF16))
    return pl.pallas_call(
        _compress_kernel, grid=(2, batch, NSA_GROUPS),
        in_specs=[pl.BlockSpec((None, seq, HEAD_DIM), lambda t, b, g: (BLK_NKC + 2 * t + g, b, 0)),
                  pl.BlockSpec((None, SUBLANE * 2, flat), lambda t, b, g: (t, 0, 0)),
                  pl.BlockSpec((None, flat, HEAD_DIM), lambda t, b, g: (t, 0, 0)),
                  pl.BlockSpec((None, HEAD_DIM, HEAD_DIM), lambda t, b, g: (t, 0, 0))],
        out_specs=pl.BlockSpec((None, None, None, n_sub, HEAD_DIM), lambda t, b, g: (t, b, g, 0, 0)),
        out_shape=jax.ShapeDtypeStruct((2, batch, NSA_GROUPS, n_sub, HEAD_DIM), BF16),
        scratch_shapes=[pltpu.VMEM((seq + NSA_CMP_STRIDE, HEAD_DIM), F32)],
        compiler_params=_params("parallel", "parallel", "parallel"), name="nsa_compress",
    )(proj, pe_flat, w1.astype(BF16), w2.astype(BF16))


def _nsa_cmp_kernel(q_ref, kc_ref, vc_ref, cov_ref, oc_ref, sel_ref, *, n_sel):
    qi = pl.program_id(2)
    tq = ATT_TQ
    n_cmp = kc_ref.shape[0]
    scale = HEAD_DIM ** -0.5
    q_pos = qi * tq + lax.broadcasted_iota(jnp.int32, (tq, n_cmp), 0)
    n_idx = lax.broadcasted_iota(jnp.int32, (tq, n_cmp), 1)
    ok = n_idx * NSA_CMP_STRIDE + (NSA_CMP_LEN - 1) <= q_pos
    kc = kc_ref[...]
    vc = vc_ref[...]
    p_sum = jnp.zeros((tq, n_cmp), F32)
    for r in range(NSA_REP):
        s = jnp.where(ok, _dot_nt(q_ref[r], kc) * scale, NEG)
        mx = jnp.max(s, axis=-1, keepdims=True)
        e = jnp.where(ok, jnp.exp(s - mx), 0.0)
        l = jnp.sum(e, axis=-1, keepdims=True)
        p = e * jnp.where(l > 0.0, 1.0 / l, 0.0)
        oc_ref[r] = _dot(p.astype(BF16), vc).astype(oc_ref.dtype)
        p_sum = p_sum + p

    n_rows = -(-n_sel // SUBLANE) * SUBLANE
    imp = lax.dot_general(cov_ref[...], p_sum, _NT, precision=lax.Precision.HIGHEST,
                          preferred_element_type=F32)[:n_rows, :]
    ridx = lax.broadcasted_iota(jnp.int32, (n_rows, tq), 0)
    blk = (qi * tq + lax.broadcasted_iota(jnp.int32, (n_rows, tq), 1)) // NSA_SEL_LEN
    forced = (ridx == 0) | (ridx == blk) | (ridx == blk - 1)
    x = jnp.where(forced, FORCE, imp)
    sel_ref[...] = _topk_rows(x, blk[0:1, :] + 1, min(NSA_TOPK, n_sel), n_sel).astype(sel_ref.dtype)


def _cover_matrix(n_cmp_pad, n_cmp, n_sel):
    c_start = np.arange(n_cmp_pad) * NSA_CMP_STRIDE
    s_start = np.arange(LANE) * NSA_SEL_LEN
    cover = (c_start[:, None] < s_start[None, :] + NSA_SEL_LEN) & (c_start[:, None] + NSA_CMP_LEN > s_start[None, :])
    cover &= (np.arange(n_cmp_pad) < n_cmp)[:, None] & (np.arange(LANE) < n_sel)[None, :]
    return jnp.asarray(cover.T, F32)


def _nsa_cmp(proj, kvc, batch, seq):
    nq = seq // ATT_TQ
    n_sub = seq // NSA_CMP_STRIDE
    n_sel = seq // NSA_SEL_LEN
    assert n_sel <= LANE
    m = batch * seq
    cover = _cover_matrix(n_sub, n_sub - NSA_CMP_LEN // NSA_CMP_STRIDE + 1, n_sel)
    cmp_spec = lambda t: pl.BlockSpec((None, None, None, n_sub, HEAD_DIM), lambda b, g, i: (t, b, g, 0, 0))
    return pl.pallas_call(
        functools.partial(_nsa_cmp_kernel, n_sel=n_sel),
        grid=(batch, NSA_GROUPS, nq),
        in_specs=[pl.BlockSpec((NSA_REP, ATT_TQ, HEAD_DIM), lambda b, g, i: (BLK_NQ // NSA_REP + g, b * nq + i, 0)),
                  cmp_spec(0), cmp_spec(1),
                  pl.BlockSpec((LANE, n_sub), lambda b, g, i: (0, 0))],
        out_specs=[pl.BlockSpec((NSA_REP, ATT_TQ, HEAD_DIM), lambda b, g, i: (g, b * nq + i, 0)),
                   pl.BlockSpec((ATT_TQ, LANE), lambda b, g, i: (b * nq + i, g))],
        out_shape=[jax.ShapeDtypeStruct((NSA_HEADS, m, HEAD_DIM), BF16),
                   jax.ShapeDtypeStruct((m, NSA_GROUPS * LANE), BF16)],
        compiler_params=_params("parallel", "parallel", "parallel"), name="nsa_cmp",
    )(proj, kvc, kvc, cover)


def _nsa_sw_kernel(q_ref, ks_ref, vs_ref, kw_ref, vw_ref, sel_ref, oc_ref, gl_ref, c_ref, s_ref, p_ref,
                   o_ref, ksr_ref, kwr_ref, vwp_ref, vse_ref, qr_ref, m_ref, acc_ref, ow_ref, *, seq):
    grp = pl.program_id(1)
    qi = pl.program_id(2)
    tq, tk = ATT_TQ, ATT_TK
    rep = NSA_REP
    win = NSA_WINDOW
    gk = SEL_GROUP * tk

    @pl.when(qi == 0)
    def _():
        kwr_ref[:win, :] = jnp.zeros((win, HEAD_DIM), BF16)
        vwp_ref[:win, :] = jnp.zeros((win, 2 * HEAD_DIM), BF16)
        vwp_ref[win:, HEAD_DIM:] = jnp.ones((seq, HEAD_DIM), BF16)
        vse_ref[:, :HEAD_DIM] = vs_ref[...]
        vse_ref[:, HEAD_DIM:] = jnp.ones((seq, HEAD_DIM), BF16)
        for j in range(seq // tk):
            rows = slice(j * tk, (j + 1) * tk)
            shifted = slice(win + j * tk, win + (j + 1) * tk)
            c, s = c_ref[rows, :], s_ref[rows, :]
            ksr_ref[rows, :] = _rope_partial(ks_ref[rows, :], c, s, p_ref).astype(BF16)
            kwr_ref[shifted, :] = _rope_partial(kw_ref[rows, :], c, s, p_ref).astype(BF16)
            vwp_ref[shifted, :HEAD_DIM] = vw_ref[rows, :]

    r0 = pl.multiple_of(qi * tq, tq)
    cq = c_ref[pl.ds(r0, tq), :]
    sq = s_ref[pl.ds(r0, tq), :]
    for r in range(rep):
        qr_ref[r] = (_rope_partial(q_ref[r], cq, sq, p_ref) * Q_SCALE).astype(BF16)

    q_all = qr_ref[...].reshape(rep * tq, HEAD_DIM)

    row_w = lax.broadcasted_iota(jnp.int32, (tq, win + tq), 0)
    col_w = lax.broadcasted_iota(jnp.int32, (tq, win + tq), 1)
    in_win = (col_w > row_w) & (col_w <= row_w + win) & (col_w >= win - r0)
    s = _dot_nt(q_all, kwr_ref[pl.ds(r0, win + tq), :]).reshape(rep, tq, win + tq)
    s = jnp.where(in_win[None], s, NEG)
    p = jnp.exp2((s - jnp.max(s, axis=-1, keepdims=True)).astype(BF16))
    o_w = _dot(p.reshape(rep * tq, win + tq), vwp_ref[pl.ds(r0, win + tq), :]).reshape(rep, tq, 2 * HEAD_DIM)
    ow_ref[...] = o_w[..., :HEAD_DIM] / o_w[..., HEAD_DIM:HEAD_DIM + 1]

    sel = sel_ref[...]
    row = lax.broadcasted_iota(jnp.int32, (tq, gk), 0)
    col = lax.broadcasted_iota(jnp.int32, (tq, gk), 1)

    def chosen_keys(jg):
        blk_row = lax.broadcasted_iota(jnp.int32, (LANE, gk), 0)
        blk_col = jg * (gk // NSA_SEL_LEN) + lax.broadcasted_iota(jnp.int32, (LANE, gk), 1) // NSA_SEL_LEN
        return _dot(sel, jnp.where(blk_row == blk_col, 1.0, 0.0).astype(BF16))

    def attend(off, valid, first):
        s = _dot_nt(q_all, ksr_ref[pl.ds(off, gk), :]).reshape(rep, tq, gk)
        s = jnp.where(valid[None], s, NEG)
        m_new = jnp.max(s, axis=-1, keepdims=True)
        if not first:
            m_old = m_ref[...]
            m_new = jnp.maximum(m_old, m_new)
        p = jnp.exp2((s - m_new).astype(BF16))
        v_grp = vse_ref[pl.ds(off, gk), :]
        for r in range(rep):
            pv = _dot(p[r], v_grp)
            if first:
                acc_ref[r] = pv
            else:
                acc_ref[r] = jnp.exp2(m_old[r] - m_new[r]) * acc_ref[r] + pv
        m_ref[...] = m_new

    gd = qi // SEL_GROUP
    off_d = pl.multiple_of(gd * gk, gk)
    causal = col + off_d <= row + r0
    attend(off_d, jnp.where(causal, chosen_keys(gd), 0.0) > 0.5, True)

    def sel_body(jg, carry):
        attend(pl.multiple_of(jg * gk, gk), chosen_keys(jg) > 0.5, False)
        return carry

    lax.fori_loop(0, gd, sel_body, 0)

    gates = _sigmoid(gl_ref[...].astype(F32))
    lane = lax.broadcasted_iota(jnp.int32, (tq, LANE), 1)

    def gate_col(c):
        return jnp.sum(jnp.where(lane == c, gates, 0.0), axis=-1, keepdims=True)

    for r in range(rep):
        base = (grp * rep + r) * 3
        o_s = acc_ref[r, :, :HEAD_DIM] / acc_ref[r, :, HEAD_DIM:HEAD_DIM + 1]
        o = gate_col(base) * oc_ref[r].astype(F32) + gate_col(base + 1) * o_s + gate_col(base + 2) * ow_ref[r]
        o_ref[:, r * HEAD_DIM:(r + 1) * HEAD_DIM] = o.astype(o_ref.dtype)


def _nsa_sw(proj, sel, o_cmp, cos, sin, rot, batch, seq):
    nq = seq // ATT_TQ
    assert seq % (SEL_GROUP * ATT_TK) == 0 and NSA_WINDOW % ATT_TK == 0 and ATT_TQ == ATT_TK
    m = batch * seq
    rep = NSA_REP
    head = lambda off: pl.BlockSpec((None, seq, HEAD_DIM), lambda b, g, i: (off + g, b, 0))
    table = pl.BlockSpec((seq, LANE), lambda b, g, i: (b, 0))
    q_like = lambda off: pl.BlockSpec((rep, ATT_TQ, HEAD_DIM), lambda b, g, i: (off + g, b * nq + i, 0))
    return pl.pallas_call(
        functools.partial(_nsa_sw_kernel, seq=seq),
        grid=(batch, NSA_GROUPS, nq),
        in_specs=[q_like(BLK_NQ // rep), head(BLK_NKS), head(BLK_NVS), head(BLK_NKW), head(BLK_NVW),
                  pl.BlockSpec((ATT_TQ, LANE), lambda b, g, i: (b * nq + i, g)),
                  q_like(0),
                  pl.BlockSpec((None, ATT_TQ, LANE), lambda b, g, i: (BLK_NG, b * nq + i, 0)),
                  table, table,
                  pl.BlockSpec((HEAD_DIM, HEAD_DIM), lambda b, g, i: (0, 0))],
        out_specs=pl.BlockSpec((ATT_TQ, rep * HEAD_DIM), lambda b, g, i: (b * nq + i, g)),
        out_shape=jax.ShapeDtypeStruct((m, NSA_HEADS * HEAD_DIM), BF16),
        scratch_shapes=[pltpu.VMEM((seq, HEAD_DIM), BF16),
                        pltpu.VMEM((seq + NSA_WINDOW, HEAD_DIM), BF16),
                        pltpu.VMEM((seq + NSA_WINDOW, 2 * HEAD_DIM), BF16),
                        pltpu.VMEM((seq, 2 * HEAD_DIM), BF16),
                        pltpu.VMEM((rep, ATT_TQ, HEAD_DIM), BF16),
                        pltpu.VMEM((rep, ATT_TQ, 1), F32),
                        pltpu.VMEM((rep, ATT_TQ, 2 * HEAD_DIM), F32), pltpu.VMEM((rep, ATT_TQ, HEAD_DIM), F32)],
        compiler_params=_params("parallel", "parallel", "arbitrary"), name="nsa_sel_win",
    )(proj, proj, proj, proj, proj, sel, o_cmp, proj, cos, sin, rot)


def _rope_full(x_bf16, c, s):
    half = x_bf16.shape[-1] // 2
    x = x_bf16.astype(F32)
    x1, x2 = x[:, :half], x[:, half:]
    return jnp.concatenate([x1 * c - x2 * s, x1 * s + x2 * c], axis=-1)


def _retention_kernel(q_ref, k_ref, v_ref, g_ref, c_ref, s_ref, lg_ref, gn_ref, o_ref, state_ref):
    ci = pl.program_id(1)
    ch = q_ref.shape[0]

    @pl.when(ci == 0)
    def _():
        state_ref[...] = jnp.zeros_like(state_ref)

    c, s = c_ref[...], s_ref[...]
    n_row = lax.broadcasted_iota(jnp.int32, (ch, ch), 0)
    n_col = lax.broadcasted_iota(jnp.int32, (ch, ch), 1)
    diff = (n_row - n_col).astype(F32)
    n_vec = lax.broadcasted_iota(jnp.int32, (ch, 1), 0).astype(F32)

    for h in range(RET_HEADS):
        log_g = lg_ref[h][:, 0:1]
        qk_cols = slice(h * RET_DK, (h + 1) * RET_DK)
        v_cols = slice(h * RET_DV, (h + 1) * RET_DV)
        q = _rope_full(q_ref[:, qk_cols], c, s)
        k = _rope_full(k_ref[:, qk_cols], c, s) * (RET_DK ** -0.5)
        v = v_ref[:, v_cols]

        decay = jnp.where(diff >= 0.0, jnp.exp(jnp.maximum(diff, 0.0) * log_g), 0.0)
        q_dec = jnp.exp((n_vec + 1.0) * log_g)
        k_dec = jnp.exp((ch - 1.0 - n_vec) * log_g)
        c_dec = jnp.exp(ch * log_g)

        qb = q.astype(BF16)
        state = state_ref[h]
        scores = _dot_nt(qb, k.astype(BF16)) * decay
        y = _dot(scores.astype(BF16), v) + _dot(qb, state.astype(BF16)) * q_dec
        state_ref[h] = state * c_dec + lax.dot_general((k * k_dec).astype(BF16), v, _TN,
                                                       preferred_element_type=F32)

        mu = jnp.mean(y, axis=-1, keepdims=True)
        yc = y - mu
        var = jnp.mean(yc * yc, axis=-1, keepdims=True)
        yn = yc * lax.rsqrt(var + RMS_EPS) * gn_ref[h]
        gate = g_ref[:, v_cols].astype(F32)
        o_ref[:, v_cols] = (gate * _sigmoid(gate) * yn).astype(o_ref.dtype)


def _retention(proj, cos_r, sin_r, gn, batch, seq):
    ch = RET_CHUNK
    nch = seq // ch
    assert seq % ch == 0
    m = batch * seq
    log_g = jnp.log(1.0 - jnp.exp2(-5.0 - jnp.arange(RET_HEADS, dtype=F32)))
    log_g = jnp.broadcast_to(log_g[:, None, None], (RET_HEADS, 1, LANE))
    qk_w = RET_HEADS * RET_DK
    v_w = RET_HEADS * RET_DV
    assert v_w == 2 * qk_w
    rows = lambda b, c: b * nch + c
    return pl.pallas_call(
        _retention_kernel, grid=(batch, nch),
        in_specs=[pl.BlockSpec((ch, qk_w), lambda b, c: (rows(b, c), 0)),
                  pl.BlockSpec((ch, qk_w), lambda b, c: (rows(b, c), 1)),
                  pl.BlockSpec((ch, v_w), lambda b, c: (rows(b, c), 1)),
                  pl.BlockSpec((ch, v_w), lambda b, c: (rows(b, c), 2)),
                  pl.BlockSpec((ch, LANE), lambda b, c: (rows(b, c), 0)),
                  pl.BlockSpec((ch, LANE), lambda b, c: (rows(b, c), 0)),
                  pl.BlockSpec((RET_HEADS, 1, LANE), lambda b, c: (0, 0, 0)),
                  pl.BlockSpec((RET_HEADS, 1, RET_DV), lambda b, c: (0, 0, 0))],
        out_specs=pl.BlockSpec((ch, v_w), lambda b, c: (rows(b, c), 0)),
        out_shape=jax.ShapeDtypeStruct((m, v_w), BF16),
        scratch_shapes=[pltpu.VMEM((RET_HEADS, RET_DK, RET_DV), F32)],
        compiler_params=_params("parallel", "arbitrary"), name="retention",
    )(proj, proj, proj, proj, cos_r, sin_r, log_g, gn.reshape(RET_HEADS, 1, RET_DV).astype(F32))


def _cross_kernel(h_ref, g_ref, wq_ref, kv_ref, wo_ref, o_ref):
    h = h_ref[...]
    hn = _rms_rows(h, g_ref[...]).astype(BF16)
    q = (_dot(hn, wq_ref[...]) * Q_SCALE).astype(BF16)
    kv_cols = X_HEADS * HEAD_DIM
    outs = []
    for hd in range(X_HEADS):
        cols = slice(hd * HEAD_DIM, (hd + 1) * HEAD_DIM)
        s = _dot_nt(q[:, cols], kv_ref[:, cols])
        mx = jnp.max(s, axis=-1, keepdims=True)
        p = jnp.exp2(s - mx)
        l = jnp.sum(p, axis=-1, keepdims=True)
        p = p / l
        outs.append(_dot(p.astype(BF16), kv_ref[:, kv_cols + hd * HEAD_DIM:kv_cols + (hd + 1) * HEAD_DIM]))
    o = jnp.concatenate(outs, axis=-1).astype(BF16)
    o_ref[...] = h + _dot(o, wo_ref[...])


def _cross(h, g, wq, kv, wo, layer, batch, seq, n_mem, *, tm):
    m, d = h.shape
    per_b = seq // tm
    kv_cols = 2 * X_HEADS * HEAD_DIM
    return pl.pallas_call(
        _cross_kernel, grid=(m // tm,),
        in_specs=[pl.BlockSpec((tm, d), lambda i: (i, 0)),
                  pl.BlockSpec((1, d), lambda i: (0, 0)),
                  pl.BlockSpec((d, X_HEADS * HEAD_DIM), lambda i: (0, 0)),
                  pl.BlockSpec((n_mem, kv_cols), lambda i: (i // per_b, layer)),
                  pl.BlockSpec((X_HEADS * HEAD_DIM, d), lambda i: (0, 0))],
        out_specs=pl.BlockSpec((tm, d), lambda i: (i, 0)),
        out_shape=jax.ShapeDtypeStruct((m, d), F32),
        compiler_params=_params("parallel"), name="cross_attn",
    )(h, g.reshape(1, d), wq, kv, wo)


def _ffn_kernel(h_ref, halo_ref, g_ref, wg_ref, wv_ref, cwg_ref, cwv_ref, cbg_ref, cbv_ref, wd_ref,
                o_ref, hn_ref, acc_ref, *, per_b):
    i = pl.program_id(0)
    f = pl.program_id(1)
    tm = h_ref.shape[0]
    pad = SUBLANE * 2

    @pl.when(f == 0)
    def _():
        g = g_ref[...]
        hn_ref[pad:, :] = _rms_rows(h_ref[...], g).astype(BF16)
        halo = _rms_rows(halo_ref[...], g)
        halo = jnp.where(i % per_b == 0, 0.0, halo)
        hn_ref[:pad, :] = halo.astype(BF16)
        acc_ref[...] = jnp.zeros_like(acc_ref)

    hn = hn_ref[...]

    def conv(u, w, b):
        return (b + w[0:1, :] * u[pad - 2:pad - 2 + tm] + w[1:2, :] * u[pad - 1:pad - 1 + tm]
                + w[2:3, :] * u[pad:pad + tm])

    gate = conv(_dot(hn, wg_ref[...]), cwg_ref[...], cbg_ref[...])
    val = conv(_dot(hn, wv_ref[...]), cwv_ref[...], cbv_ref[...])
    act = (gate * _sigmoid(gate) * val).astype(BF16)
    acc_ref[...] += _dot(act, wd_ref[...])

    @pl.when(f == pl.num_programs(1) - 1)
    def _():
        o_ref[...] = h_ref[...] + acc_ref[...]


def _ffn(h, g, w_up, conv_w, conv_b, w_down, batch, seq, *, tm, tf):
    m, d = h.shape
    dff = w_down.shape[0]
    assert seq % tm == 0 and dff % tf == 0
    per_b = seq // tm
    nf = dff // tf
    pad = SUBLANE * 2
    halo_blocks = tm // pad
    cw = jnp.zeros((SUBLANE, 2 * dff), F32).at[:CONV_WIDTH].set(conv_w)
    cb = conv_b.reshape(1, 2 * dff)
    return pl.pallas_call(
        functools.partial(_ffn_kernel, per_b=per_b),
        grid=(m // tm, nf),
        in_specs=[pl.BlockSpec((tm, d), lambda i, f: (i, 0)),
                  pl.BlockSpec((pad, d), lambda i, f: (jnp.maximum(i * halo_blocks - 1, 0), 0)),
                  pl.BlockSpec((1, d), lambda i, f: (0, 0)),
                  pl.BlockSpec((d, tf), lambda i, f: (0, f)),
                  pl.BlockSpec((d, tf), lambda i, f: (0, nf + f)),
                  pl.BlockSpec((SUBLANE, tf), lambda i, f: (0, f)),
                  pl.BlockSpec((SUBLANE, tf), lambda i, f: (0, nf + f)),
                  pl.BlockSpec((1, tf), lambda i, f: (0, f)),
                  pl.BlockSpec((1, tf), lambda i, f: (0, nf + f)),
                  pl.BlockSpec((tf, d), lambda i, f: (f, 0))],
        out_specs=pl.BlockSpec((tm, d), lambda i, f: (i, 0), pipeline_mode=pl.Buffered(1)),
        out_shape=jax.ShapeDtypeStruct((m, d), F32),
        scratch_shapes=[pltpu.VMEM((tm + pad, d), BF16), pltpu.VMEM((tm, d), F32)],
        compiler_params=_params("parallel", "arbitrary"), name="conv_ffn",
    )(h, h, g.reshape(1, d), w_up, w_up, cw, cw, cb, cb, w_down)


def _final_norm_kernel(x_ref, g_ref, o_ref):
    o_ref[...] = _rms_rows(x_ref[...], g_ref[...])


def _final_norm(x, g, *, tm):
    m, d = x.shape
    return pl.pallas_call(
        _final_norm_kernel, grid=(m // tm,),
        in_specs=[pl.BlockSpec((tm, d), lambda i: (i, 0)), pl.BlockSpec((1, d), lambda i: (0, 0))],
        out_specs=pl.BlockSpec((tm, d), lambda i: (i, 0)),
        out_shape=jax.ShapeDtypeStruct((m, d), F32),
        compiler_params=_params("parallel"), name="final_norm",
    )(x, g.reshape(1, d))


def _row_tile(m, want):
    return want if m % want == 0 else m


def _mixer_ab(h, g, w_in, pe, w1, w2, w_out, tabs, rot, batch, seq):
    m, d = h.shape
    cos, sin = tabs[0], tabs[1]
    w_pad = jnp.zeros((d, AB_BLOCKS * LANE), BF16).at[:, :AB_COLS].set(w_in.astype(BF16))
    proj = _norm_matmul(h, g, w_pad, tm=_row_tile(m, 1024), tn=2048, head_major=True)
    o_moba = _moba(proj, cos, sin, rot, batch, seq)
    kvc = _compress(proj, pe, w1, w2, batch, seq)
    o_cmp, sel = _nsa_cmp(proj, kvc, batch, seq)
    o_nsa = _nsa_sw(proj, sel, o_cmp, cos, sin, rot, batch, seq)
    return _matmul_res([o_moba, o_nsa], w_out.astype(BF16), h, tm=_row_tile(m, 1024), tn=1024)


def _mixer_c(h, g, w_in, gn, w_out, tabs, batch, seq):
    m, d = h.shape
    proj = _norm_matmul(h, g, w_in.astype(BF16), tm=_row_tile(m, 1024), tn=2048)
    y = _retention(proj, tabs[2], tabs[3], gn, batch, seq)
    return _matmul_res([y], w_out.astype(BF16), h, tm=_row_tile(m, 1024), tn=1024)


def kernel(x, mem, positions, norm_mix, norm_cross, norm_ffn, norm_mem, norm_final, w_in_ab, cmp_pe_k, cmp_w1_k,
           cmp_w2_k, cmp_pe_v, cmp_w1_v, cmp_w2_v, w_out_ab, w_in_c, ret_gn, w_out_c, w_q_x, w_kv_x, w_o_x, w_up,
           conv_w, conv_b, w_down):
    batch, seq, d = x.shape
    n_mem = mem.shape[1]
    depth = norm_mix.shape[0]
    m = batch * seq
    tabs = _rope_tables(positions)
    rot = _rot_matrix()

    kv_cols = 2 * X_HEADS * HEAD_DIM
    w_kv = jnp.transpose(w_kv_x, (1, 0, 2)).reshape(d, depth * kv_cols).astype(BF16)
    kv = _norm_matmul(mem.reshape(batch * n_mem, d), norm_mem, w_kv, tm=_row_tile(batch * n_mem, 512), tn=512)

    h = x.reshape(m, d)
    for l in range(depth):
        if l % 2 == 0:
            e = l // 2
            h = _mixer_ab(h, norm_mix[l], w_in_ab[e],
                          jnp.stack([cmp_pe_k[e], cmp_pe_v[e]]), jnp.stack([cmp_w1_k[e], cmp_w1_v[e]]),
                          jnp.stack([cmp_w2_k[e], cmp_w2_v[e]]), w_out_ab[e], tabs, rot, batch, seq)
        else:
            o = l // 2
            h = _mixer_c(h, norm_mix[l], w_in_c[o], ret_gn[o], w_out_c[o], tabs, batch, seq)
        h = _cross(h, norm_cross[l], w_q_x[l].astype(BF16), kv, w_o_x[l].astype(BF16), l, batch, seq, n_mem,
                   tm=_row_tile(seq, 512))
        h = _ffn(h, norm_ffn[l], w_up[l].astype(BF16), conv_w[l], conv_b[l], w_down[l].astype(BF16), batch, seq,
                 tm=_row_tile(seq, 1024), tf=512)
    return _final_norm(h, norm_final, tm=_row_tile(m, 512)).reshape(batch, seq, d)
```

```python
import functools

import numpy as np
import jax
import jax.numpy as jnp
from jax import lax
from jax.experimental import pallas as pl
from jax.experimental.pallas import tpu as pltpu

F32 = jnp.float32
BF16 = jnp.bfloat16

D_MODEL = 2048
DEPTH = 4
HEAD_DIM = 128
ROPE_THETA = 500000.0
ROPE_DIM = HEAD_DIM // 4
X_HEADS = 4
MOBA_HEADS = 8
MOBA_BLOCK = 256
MOBA_TOPK = 3
NSA_HEADS = 8
NSA_GROUPS = 2
NSA_REP = NSA_HEADS // NSA_GROUPS
NSA_CMP_LEN = 32
NSA_CMP_STRIDE = 16
NSA_SEL_LEN = 64
NSA_TOPK = 16
NSA_WINDOW = 512
RET_HEADS = 8
RET_DK = 256
RET_DV = 512
RET_THETA = 10000.0
D_FF = 5632
CONV_WIDTH = 3
RMS_EPS = 1e-6
NEG = -1e30
FORCE = 1e9
LOG2E = 1.4426950408889634
Q_SCALE = HEAD_DIM ** -0.5 * LOG2E

LANE = 128
SUBLANE = 8
VMEM_LIMIT = 56 * 2 ** 20

AB_SIZES = (MOBA_HEADS * HEAD_DIM,) * 3 + (NSA_HEADS * HEAD_DIM,) + (NSA_GROUPS * HEAD_DIM,) * 6 + (NSA_HEADS * 3,)
AB_COLS = sum(AB_SIZES)
AB_BLOCKS = 48
BLK_MQ, BLK_MK, BLK_MV, BLK_NQ = 0, 8, 16, 24
BLK_NKC, BLK_NVC, BLK_NKS, BLK_NVS, BLK_NKW, BLK_NVW, BLK_NG = 32, 34, 36, 38, 40, 42, 44

ATT_TQ = 256
ATT_TK = 256
RET_CHUNK = 128
MOBA_GROUP = 4
MOBA_HB = 4
SEL_GROUP = 4

_NT = (((1,), (1,)), ((), ()))
_TN = (((0,), (0,)), ((), ()))


def _params(*sem):
    return pltpu.CompilerParams(dimension_semantics=sem, vmem_limit_bytes=VMEM_LIMIT)


def _dot(a, b):
    return jnp.dot(a, b, preferred_element_type=F32)


def _dot_nt(a, b):
    return lax.dot_general(a, b, _NT, preferred_element_type=F32)


def _sigmoid(x):
    return 1.0 / (1.0 + jnp.exp(-x))


def _rms_rows(x, g):
    ms = jnp.mean(x * x, axis=-1, keepdims=True)
    return x * lax.rsqrt(ms + RMS_EPS) * g


def _tables_kernel(pos_ref, inv_rope_ref, inv_ret_ref, c_ref, s_ref, cr_ref, sr_ref):
    pos = pos_ref[...]
    lane = lax.broadcasted_iota(jnp.int32, pos.shape, 1)
    rot = lane < ROPE_DIM
    ang = pos * inv_rope_ref[...]
    c_ref[...] = jnp.where(rot, jnp.cos(ang), 1.0)
    s_ref[...] = jnp.where(rot, jnp.sin(ang), 0.0)
    ang_r = pos * inv_ret_ref[...]
    cr_ref[...] = jnp.cos(ang_r)
    sr_ref[...] = jnp.sin(ang_r)


def _rope_tables(positions):
    m = positions.size
    posb = jnp.broadcast_to(positions.reshape(m, 1).astype(F32), (m, LANE))
    half = ROPE_DIM // 2
    inv = jnp.float32(ROPE_THETA) ** (-jnp.arange(half, dtype=F32) / half)
    inv_rope = jnp.concatenate([inv, inv, jnp.zeros((LANE - ROPE_DIM,), F32)]).reshape(1, LANE)
    half_r = RET_DK // 2
    inv_ret = (jnp.float32(RET_THETA) ** (-jnp.arange(half_r, dtype=F32) / half_r)).reshape(1, LANE)
    tm = 1024 if m % 1024 == 0 else m
    row = pl.BlockSpec((tm, LANE), lambda i: (i, 0))
    vec = pl.BlockSpec((1, LANE), lambda i: (0, 0))
    sds = jax.ShapeDtypeStruct((m, LANE), F32)
    return pl.pallas_call(
        _tables_kernel, grid=(m // tm,), in_specs=[row, vec, vec], out_specs=[row] * 4,
        out_shape=[sds] * 4, compiler_params=_params("parallel"), name="rope_tables",
    )(posb, inv_rope, inv_ret)


def _rot_matrix():
    half = ROPE_DIM // 2
    p = np.zeros((HEAD_DIM, HEAD_DIM), np.float32)
    for l in range(half):
        p[l + half, l] = -1.0
        p[l, l + half] = 1.0
    return jnp.asarray(p, BF16)


def _rope_partial(x_bf16, c, s, p_ref):
    return x_bf16.astype(F32) * c + _dot(x_bf16, p_ref[...]) * s


def _norm_matmul_kernel(x_ref, g_ref, w_ref, o_ref, xn_ref, *, head_major):
    @pl.when(pl.program_id(1) == 0)
    def _():
        xn_ref[...] = _rms_rows(x_ref[...], g_ref[...]).astype(BF16)

    acc = _dot(xn_ref[...], w_ref[...])
    if head_major:
        for c in range(o_ref.shape[0]):
            o_ref[c] = acc[:, c * LANE:(c + 1) * LANE].astype(o_ref.dtype)
    else:
        o_ref[...] = acc.astype(o_ref.dtype)


def _norm_matmul(x, g, w, *, tm, tn, head_major=False):
    stack = w.shape[0] if w.ndim == 3 else 1
    m, k = x.shape
    per_w = w.shape[-1] // tn
    n = stack * w.shape[-1]
    assert m % tm == 0 and w.shape[-1] % tn == 0 and tn % LANE == 0
    if head_major:
        out_shape = jax.ShapeDtypeStruct((n // LANE, m, LANE), BF16)
        out_spec = pl.BlockSpec((tn // LANE, tm, LANE), lambda i, j: (j, i, 0))
    else:
        out_shape = jax.ShapeDtypeStruct((m, n), BF16)
        out_spec = pl.BlockSpec((tm, tn), lambda i, j: (i, j))
    return pl.pallas_call(
        functools.partial(_norm_matmul_kernel, head_major=head_major),
        grid=(m // tm, n // tn),
        in_specs=[pl.BlockSpec((tm, k), lambda i, j: (i, 0)),
                  pl.BlockSpec((1, k), lambda i, j: (0, 0)),
                  pl.BlockSpec((k, tn), lambda i, j: (0, j)) if w.ndim == 2 else
                  pl.BlockSpec((None, k, tn), lambda i, j: (j // per_w, 0, j % per_w))],
        out_specs=out_spec, out_shape=out_shape,
        scratch_shapes=[pltpu.VMEM((tm, k), BF16)],
        compiler_params=_params("parallel", "arbitrary"), name="norm_matmul",
    )(x, g.reshape(1, k), w)


def _matmul_res_kernel(*refs):
    r_ref, o_ref = refs[-2], refs[-1]
    acc = r_ref[...]
    for x_ref, w_ref in zip(refs[:-2:2], refs[1:-2:2]):
        acc = acc + _dot(x_ref[...], w_ref[...])
    o_ref[...] = acc


def _matmul_res(xs, w, res, *, tm, tn):
    m, n = res.shape
    k = xs[0].shape[1]
    assert m % tm == 0 and n % tn == 0 and all(x.shape[1] == k for x in xs) and w.shape[0] == k * len(xs)
    in_specs, args = [], []
    for idx, x in enumerate(xs):
        in_specs += [pl.BlockSpec((tm, k), lambda i, j: (i, 0)),
                     pl.BlockSpec((k, tn), lambda i, j, idx=idx: (idx, j))]
        args += [x, w]
    return pl.pallas_call(
        _matmul_res_kernel, grid=(m // tm, n // tn),
        in_specs=in_specs + [pl.BlockSpec((tm, tn), lambda i, j: (i, j))],
        out_specs=pl.BlockSpec((tm, tn), lambda i, j: (i, j)),
        out_shape=jax.ShapeDtypeStruct((m, n), F32),
        compiler_params=_params("parallel", "arbitrary"), name="matmul_res",
    )(*args, res)


def _topk_rows(x, n_valid, k, n_rows):
    rows, nq = x.shape
    ridx = lax.broadcasted_iota(jnp.int32, (rows, nq), 0)
    x = jnp.where(ridx < n_valid, x, NEG)
    x = jnp.where(ridx < n_rows, x, -jnp.inf)
    rank = jnp.zeros((rows, nq), F32)
    for i in range(n_rows):
        xi = x[i:i + 1, :]
        rank = rank + jnp.where(ridx > i, jnp.where(xi >= x, 1.0, 0.0), jnp.where(xi > x, 1.0, 0.0))
    marks = jnp.where(rank < float(k), jnp.where(x > NEG / 2, 1.0, 0.0), 0.0)
    if rows < LANE:
        marks = jnp.concatenate([marks, jnp.zeros((LANE - rows, nq), F32)], axis=0)
    return marks.T


def _moba_kernel(q_ref, k_ref, v_ref, c_ref, s_ref, p_ref, o_ref, krot_ref, kmean_ref, vext_ref, *, nb):
    qi = pl.program_id(2)
    blk = MOBA_BLOCK
    nb_pad = -(-nb // SUBLANE) * SUBLANE
    grp = MOBA_GROUP
    heads = range(MOBA_HB)

    @pl.when(qi == 0)
    def _():
        kmean_ref[...] = jnp.zeros_like(kmean_ref)
        for hh in heads:
            for j in range(nb):
                rows = slice(j * blk, (j + 1) * blk)
                kr = _rope_partial(k_ref[hh, rows, :], c_ref[rows, :], s_ref[rows, :], p_ref)
                krot_ref[hh, rows, :] = kr.astype(BF16)
                kmean_ref[hh, j:j + 1, :] = jnp.mean(kr, axis=0, keepdims=True)
            vext_ref[hh, :, :HEAD_DIM] = v_ref[hh]
            vext_ref[hh, :, HEAD_DIM:] = jnp.ones((v_ref.shape[1], HEAD_DIM), BF16)

    r0 = pl.multiple_of(qi * blk, blk)
    cq = c_ref[pl.ds(r0, blk), :]
    sq = s_ref[pl.ds(r0, blk), :]
    lane = lax.broadcasted_iota(jnp.int32, (blk, LANE), 1)
    row = lax.broadcasted_iota(jnp.int32, (blk, blk), 0)
    col = lax.broadcasted_iota(jnp.int32, (blk, blk), 1)

    qbs, sels, state = [], [], []
    for hh in heads:
        q = _rope_partial(q_ref[hh], cq, sq, p_ref)
        qb = (q * Q_SCALE).astype(BF16)
        gate_t = lax.dot_general(kmean_ref[hh], q, _NT, precision=lax.Precision.HIGHEST,
                                 preferred_element_type=F32)[:nb_pad, :]
        sels.append(_topk_rows(gate_t, qi, min(MOBA_TOPK, nb), nb))
        qbs.append(qb)
        s = jnp.where(col <= row, _dot_nt(qb, krot_ref[hh, pl.ds(r0, blk), :]), NEG)
        m = jnp.max(s, axis=-1, keepdims=True)
        p = jnp.exp2((s - m).astype(BF16))
        state += [m, _dot(p, vext_ref[hh, pl.ds(r0, blk), :])]

    def body(jg, carry):
        off = pl.multiple_of(jg * (grp * blk), grp * blk)
        out = []
        for hh in heads:
            m, acc = carry[2 * hh:2 * hh + 2]
            s = _dot_nt(qbs[hh], krot_ref[hh, pl.ds(off, grp * blk), :])
            parts = []
            for t in range(grp):
                chosen = jnp.sum(jnp.where(lane == jg * grp + t, sels[hh], 0.0), axis=-1, keepdims=True)
                parts.append(jnp.where(chosen > 0.5, s[:, t * blk:(t + 1) * blk], NEG))
            s = jnp.concatenate(parts, axis=-1)
            m_new = jnp.maximum(m, jnp.max(s, axis=-1, keepdims=True))
            p = jnp.exp2((s - m_new).astype(BF16))
            out += [m_new, jnp.exp2(m - m_new) * acc + _dot(p, vext_ref[hh, pl.ds(off, grp * blk), :])]
        return tuple(out)

    state = lax.fori_loop(0, (qi + grp - 1) // grp, body, tuple(state))
    for hh in heads:
        acc = state[2 * hh + 1]
        o_ref[:, hh * HEAD_DIM:(hh + 1) * HEAD_DIM] = (acc[:, :HEAD_DIM] / acc[:, HEAD_DIM:HEAD_DIM + 1]).astype(o_ref.dtype)


def _moba(proj, cos, sin, rot, batch, seq):
    nb = seq // MOBA_BLOCK
    hb = MOBA_HB
    assert seq % (MOBA_BLOCK * MOBA_GROUP) == 0 and nb <= LANE
    assert MOBA_HEADS % hb == 0 and BLK_MQ % hb == 0 and BLK_MK % hb == 0 and BLK_MV % hb == 0
    m = batch * seq
    head = lambda off: pl.BlockSpec((hb, seq, HEAD_DIM), lambda b, h, i: (off // hb + h, b, 0))
    table = pl.BlockSpec((seq, LANE), lambda b, h, i: (b, 0))
    return pl.pallas_call(
        functools.partial(_moba_kernel, nb=nb),
        grid=(batch, MOBA_HEADS // hb, nb),
        in_specs=[pl.BlockSpec((hb, MOBA_BLOCK, HEAD_DIM), lambda b, h, i: (BLK_MQ // hb + h, b * nb + i, 0)),
                  head(BLK_MK), head(BLK_MV), table, table,
                  pl.BlockSpec((HEAD_DIM, HEAD_DIM), lambda b, h, i: (0, 0))],
        out_specs=pl.BlockSpec((MOBA_BLOCK, hb * HEAD_DIM), lambda b, h, i: (b * nb + i, h)),
        out_shape=jax.ShapeDtypeStruct((m, MOBA_HEADS * HEAD_DIM), BF16),
        scratch_shapes=[pltpu.VMEM((hb, seq, HEAD_DIM), BF16), pltpu.VMEM((hb, LANE, HEAD_DIM), F32),
                        pltpu.VMEM((hb, seq, 2 * HEAD_DIM), BF16)],
        compiler_params=_params("parallel", "parallel", "arbitrary"), name="moba",
    )(proj, proj, proj, cos, sin, rot)


def _compress_kernel(x_ref, pe_ref, w1_ref, w2_ref, o_ref, xf_ref):
    seq = x_ref.shape[0]
    n_sub = seq // NSA_CMP_STRIDE
    xf_ref[:seq, :] = x_ref[...].astype(F32)
    xf_ref[seq:, :] = jnp.zeros((NSA_CMP_STRIDE, HEAD_DIM), F32)
    hid = _dot(pe_ref[...], w1_ref[...])[0:1, :]
    for r in range(NSA_CMP_LEN):
        rows = xf_ref[pl.ds(r, n_sub, stride=NSA_CMP_STRIDE), :]
        hid = hid + _dot(rows.astype(BF16), w1_ref[r * HEAD_DIM:(r + 1) * HEAD_DIM, :])
    hid = hid * _sigmoid(hid)
    o_ref[...] = _dot(hid.astype(BF16), w2_ref[...]).astype(o_ref.dtype)


def _compress(proj, pe, w1, w2, batch, seq):
    n_sub = seq // NSA_CMP_STRIDE
    flat = NSA_CMP_LEN * HEAD_DIM
    pe_flat = jnp.zeros((2, SUBLANE * 2, flat), BF16).at[:, 0].set(pe.reshape(2, flat).astype(BF16))
    return pl.pallas_call(
        _compress_kernel, grid=(2, batch, NSA_GROUPS),
        in_specs=[pl.BlockSpec((None, seq, HEAD_DIM), lambda t, b, g: (BLK_NKC + 2 * t + g, b, 0)),
                  pl.BlockSpec((None, SUBLANE * 2, flat), lambda t, b, g: (t, 0, 0)),
                  pl.BlockSpec((None, flat, HEAD_DIM), lambda t, b, g: (t, 0, 0)),
                  pl.BlockSpec((None, HEAD_DIM, HEAD_DIM), lambda t, b, g: (t, 0, 0))],
        out_specs=pl.BlockSpec((None, None, None, n_sub, HEAD_DIM), lambda t, b, g: (t, b, g, 0, 0)),
        out_shape=jax.ShapeDtypeStruct((2, batch, NSA_GROUPS, n_sub, HEAD_DIM), BF16),
        scratch_shapes=[pltpu.VMEM((seq + NSA_CMP_STRIDE, HEAD_DIM), F32)],
        compiler_params=_params("parallel", "parallel", "parallel"), name="nsa_compress",
    )(proj, pe_flat, w1.astype(BF16), w2.astype(BF16))


def _nsa_cmp_kernel(q_ref, kc_ref, vc_ref, cov_ref, oc_ref, sel_ref, *, n_sel):
    qi = pl.program_id(2)
    tq = ATT_TQ
    n_cmp = kc_ref.shape[0]
    scale = HEAD_DIM ** -0.5
    q_pos = qi * tq + lax.broadcasted_iota(jnp.int32, (tq, n_cmp), 0)
    n_idx = lax.broadcasted_iota(jnp.int32, (tq, n_cmp), 1)
    ok = n_idx * NSA_CMP_STRIDE + (NSA_CMP_LEN - 1) <= q_pos
    kc = kc_ref[...]
    vc = vc_ref[...]
    p_sum = jnp.zeros((tq, n_cmp), F32)
    for r in range(NSA_REP):
        s = jnp.where(ok, _dot_nt(q_ref[r], kc) * scale, NEG)
        mx = jnp.max(s, axis=-1, keepdims=True)
        e = jnp.where(ok, jnp.exp(s - mx), 0.0)
        l = jnp.sum(e, axis=-1, keepdims=True)
        p = e * jnp.where(l > 0.0, 1.0 / l, 0.0)
        oc_ref[r] = _dot(p.astype(BF16), vc).astype(oc_ref.dtype)
        p_sum = p_sum + p

    n_rows = -(-n_sel // SUBLANE) * SUBLANE
    imp = lax.dot_general(cov_ref[...], p_sum, _NT, precision=lax.Precision.HIGHEST,
                          preferred_element_type=F32)[:n_rows, :]
    ridx = lax.broadcasted_iota(jnp.int32, (n_rows, tq), 0)
    blk = (qi * tq + lax.broadcasted_iota(jnp.int32, (n_rows, tq), 1)) // NSA_SEL_LEN
    forced = (ridx == 0) | (ridx == blk) | (ridx == blk - 1)
    x = jnp.where(forced, FORCE, imp)
    sel_ref[...] = _topk_rows(x, blk[0:1, :] + 1, min(NSA_TOPK, n_sel), n_sel).astype(sel_ref.dtype)


def _cover_matrix(n_cmp_pad, n_cmp, n_sel):
    c_start = np.arange(n_cmp_pad) * NSA_CMP_STRIDE
    s_start = np.arange(LANE) * NSA_SEL_LEN
    cover = (c_start[:, None] < s_start[None, :] + NSA_SEL_LEN) & (c_start[:, None] + NSA_CMP_LEN > s_start[None, :])
    cover &= (np.arange(n_cmp_pad) < n_cmp)[:, None] & (np.arange(LANE) < n_sel)[None, :]
    return jnp.asarray(cover.T, F32)


def _nsa_cmp(proj, kvc, batch, seq):
    nq = seq // ATT_TQ
    n_sub = seq // NSA_CMP_STRIDE
    n_sel = seq // NSA_SEL_LEN
    assert n_sel <= LANE
    m = batch * seq
    cover = _cover_matrix(n_sub, n_sub - NSA_CMP_LEN // NSA_CMP_STRIDE + 1, n_sel)
    cmp_spec = lambda t: pl.BlockSpec((None, None, None, n_sub, HEAD_DIM), lambda b, g, i: (t, b, g, 0, 0))
    return pl.pallas_call(
        functools.partial(_nsa_cmp_kernel, n_sel=n_sel),
        grid=(batch, NSA_GROUPS, nq),
        in_specs=[pl.BlockSpec((NSA_REP, ATT_TQ, HEAD_DIM), lambda b, g, i: (BLK_NQ // NSA_REP + g, b * nq + i, 0)),
                  cmp_spec(0), cmp_spec(1),
                  pl.BlockSpec((LANE, n_sub), lambda b, g, i: (0, 0))],
        out_specs=[pl.BlockSpec((NSA_REP, ATT_TQ, HEAD_DIM), lambda b, g, i: (g, b * nq + i, 0)),
                   pl.BlockSpec((ATT_TQ, LANE), lambda b, g, i: (b * nq + i, g))],
        out_shape=[jax.ShapeDtypeStruct((NSA_HEADS, m, HEAD_DIM), BF16),
                   jax.ShapeDtypeStruct((m, NSA_GROUPS * LANE), BF16)],
        compiler_params=_params("parallel", "parallel", "parallel"), name="nsa_cmp",
    )(proj, kvc, kvc, cover)


def _nsa_sw_kernel(q_ref, ks_ref, vs_ref, kw_ref, vw_ref, sel_ref, oc_ref, gl_ref, c_ref, s_ref, p_ref,
                   o_ref, ksr_ref, kwr_ref, vwp_ref, vse_ref, qr_ref, m_ref, acc_ref, ow_ref, *, seq):
    grp = pl.program_id(1)
    qi = pl.program_id(2)
    tq, tk = ATT_TQ, ATT_TK
    rep = NSA_REP
    win = NSA_WINDOW
    gk = SEL_GROUP * tk

    @pl.when(qi == 0)
    def _():
        kwr_ref[:win, :] = jnp.zeros((win, HEAD_DIM), BF16)
        vwp_ref[:win, :] = jnp.zeros((win, 2 * HEAD_DIM), BF16)
        vwp_ref[win:, HEAD_DIM:] = jnp.ones((seq, HEAD_DIM), BF16)
        vse_ref[:, :HEAD_DIM] = vs_ref[...]
        vse_ref[:, HEAD_DIM:] = jnp.ones((seq, HEAD_DIM), BF16)
        for j in range(seq // tk):
            rows = slice(j * tk, (j + 1) * tk)
            shifted = slice(win + j * tk, win + (j + 1) * tk)
            c, s = c_ref[rows, :], s_ref[rows, :]
            ksr_ref[rows, :] = _rope_partial(ks_ref[rows, :], c, s, p_ref).astype(BF16)
            kwr_ref[shifted, :] = _rope_partial(kw_ref[rows, :], c, s, p_ref).astype(BF16)
            vwp_ref[shifted, :HEAD_DIM] = vw_ref[rows, :]

    r0 = pl.multiple_of(qi * tq, tq)
    cq = c_ref[pl.ds(r0, tq), :]
    sq = s_ref[pl.ds(r0, tq), :]
    for r in range(rep):
        qr_ref[r] = (_rope_partial(q_ref[r], cq, sq, p_ref) * Q_SCALE).astype(BF16)

    q_all = qr_ref[...].reshape(rep * tq, HEAD_DIM)

    row_w = lax.broadcasted_iota(jnp.int32, (tq, win + tq), 0)
    col_w = lax.broadcasted_iota(jnp.int32, (tq, win + tq), 1)
    in_win = (col_w > row_w) & (col_w <= row_w + win) & (col_w >= win - r0)
    s = _dot_nt(q_all, kwr_ref[pl.ds(r0, win + tq), :]).reshape(rep, tq, win + tq)
    s = jnp.where(in_win[None], s, NEG)
    p = jnp.exp2((s - jnp.max(s, axis=-1, keepdims=True)).astype(BF16))
    o_w = _dot(p.reshape(rep * tq, win + tq), vwp_ref[pl.ds(r0, win + tq), :]).reshape(rep, tq, 2 * HEAD_DIM)
    ow_ref[...] = o_w[..., :HEAD_DIM] / o_w[..., HEAD_DIM:HEAD_DIM + 1]

    sel = sel_ref[...]
    row = lax.broadcasted_iota(jnp.int32, (tq, gk), 0)
    col = lax.broadcasted_iota(jnp.int32, (tq, gk), 1)

    def chosen_keys(jg):
        blk_row = lax.broadcasted_iota(jnp.int32, (LANE, gk), 0)
        blk_col = jg * (gk // NSA_SEL_LEN) + lax.broadcasted_iota(jnp.int32, (LANE, gk), 1) // NSA_SEL_LEN
        return _dot(sel, jnp.where(blk_row == blk_col, 1.0, 0.0).astype(BF16))

    def attend(off, valid, first):
        s = _dot_nt(q_all, ksr_ref[pl.ds(off, gk), :]).reshape(rep, tq, gk)
        s = jnp.where(valid[None], s, NEG)
        m_new = jnp.max(s, axis=-1, keepdims=True)
        if not first:
            m_old = m_ref[...]
            m_new = jnp.maximum(m_old, m_new)
        p = jnp.exp2((s - m_new).astype(BF16))
        v_grp = vse_ref[pl.ds(off, gk), :]
        for r in range(rep):
            pv = _dot(p[r], v_grp)
            if first:
                acc_ref[r] = pv
            else:
                acc_ref[r] = jnp.exp2(m_old[r] - m_new[r]) * acc_ref[r] + pv
        m_ref[...] = m_new

    gd = qi // SEL_GROUP
    off_d = pl.multiple_of(gd * gk, gk)
    causal = col + off_d <= row + r0
    attend(off_d, jnp.where(causal, chosen_keys(gd), 0.0) > 0.5, True)

    def sel_body(jg, carry):
        attend(pl.multiple_of(jg * gk, gk), chosen_keys(jg) > 0.5, False)
        return carry

    lax.fori_loop(0, gd, sel_body, 0)

    gates = _sigmoid(gl_ref[...].astype(F32))
    lane = lax.broadcasted_iota(jnp.int32, (tq, LANE), 1)

    def gate_col(c):
        return jnp.sum(jnp.where(lane == c, gates, 0.0), axis=-1, keepdims=True)

    for r in range(rep):
        base = (grp * rep + r) * 3
        o_s = acc_ref[r, :, :HEAD_DIM] / acc_ref[r, :, HEAD_DIM:HEAD_DIM + 1]
        o = gate_col(base) * oc_ref[r].astype(F32) + gate_col(base + 1) * o_s + gate_col(base + 2) * ow_ref[r]
        o_ref[:, r * HEAD_DIM:(r + 1) * HEAD_DIM] = o.astype(o_ref.dtype)


def _nsa_sw(proj, sel, o_cmp, cos, sin, rot, batch, seq):
    nq = seq // ATT_TQ
    assert seq % (SEL_GROUP * ATT_TK) == 0 and NSA_WINDOW % ATT_TK == 0 and ATT_TQ == ATT_TK
    m = batch * seq
    rep = NSA_REP
    head = lambda off: pl.BlockSpec((None, seq, HEAD_DIM), lambda b, g, i: (off + g, b, 0))
    table = pl.BlockSpec((seq, LANE), lambda b, g, i: (b, 0))
    q_like = lambda off: pl.BlockSpec((rep, ATT_TQ, HEAD_DIM), lambda b, g, i: (off + g, b * nq + i, 0))
    return pl.pallas_call(
        functools.partial(_nsa_sw_kernel, seq=seq),
        grid=(batch, NSA_GROUPS, nq),
        in_specs=[q_like(BLK_NQ // rep), head(BLK_NKS), head(BLK_NVS), head(BLK_NKW), head(BLK_NVW),
                  pl.BlockSpec((ATT_TQ, LANE), lambda b, g, i: (b * nq + i, g)),
                  q_like(0),
                  pl.BlockSpec((None, ATT_TQ, LANE), lambda b, g, i: (BLK_NG, b * nq + i, 0)),
                  table, table,
                  pl.BlockSpec((HEAD_DIM, HEAD_DIM), lambda b, g, i: (0, 0))],
        out_specs=pl.BlockSpec((ATT_TQ, rep * HEAD_DIM), lambda b, g, i: (b * nq + i, g)),
        out_shape=jax.ShapeDtypeStruct((m, NSA_HEADS * HEAD_DIM), BF16),
        scratch_shapes=[pltpu.VMEM((seq, HEAD_DIM), BF16),
                        pltpu.VMEM((seq + NSA_WINDOW, HEAD_DIM), BF16),
                        pltpu.VMEM((seq + NSA_WINDOW, 2 * HEAD_DIM), BF16),
                        pltpu.VMEM((seq, 2 * HEAD_DIM), BF16),
                        pltpu.VMEM((rep, ATT_TQ, HEAD_DIM), BF16),
                        pltpu.VMEM((rep, ATT_TQ, 1), F32),
                        pltpu.VMEM((rep, ATT_TQ, 2 * HEAD_DIM), F32), pltpu.VMEM((rep, ATT_TQ, HEAD_DIM), F32)],
        compiler_params=_params("parallel", "parallel", "arbitrary"), name="nsa_sel_win",
    )(proj, proj, proj, proj, proj, sel, o_cmp, proj, cos, sin, rot)


def _rope_full(x_bf16, c, s):
    half = x_bf16.shape[-1] // 2
    x = x_bf16.astype(F32)
    x1, x2 = x[:, :half], x[:, half:]
    return jnp.concatenate([x1 * c - x2 * s, x1 * s + x2 * c], axis=-1)


def _retention_kernel(q_ref, k_ref, v_ref, g_ref, c_ref, s_ref, lg_ref, gn_ref, o_ref, state_ref):
    ci = pl.program_id(1)
    ch = q_ref.shape[0]

    @pl.when(ci == 0)
    def _():
        state_ref[...] = jnp.zeros_like(state_ref)

    c, s = c_ref[...], s_ref[...]
    n_row = lax.broadcasted_iota(jnp.int32, (ch, ch), 0)
    n_col = lax.broadcasted_iota(jnp.int32, (ch, ch), 1)
    diff = (n_row - n_col).astype(F32)
    n_vec = lax.broadcasted_iota(jnp.int32, (ch, 1), 0).astype(F32)

    for h in range(RET_HEADS):
        log_g = lg_ref[h][:, 0:1]
        qk_cols = slice(h * RET_DK, (h + 1) * RET_DK)
        v_cols = slice(h * RET_DV, (h + 1) * RET_DV)
        q = _rope_full(q_ref[:, qk_cols], c, s)
        k = _rope_full(k_ref[:, qk_cols], c, s) * (RET_DK ** -0.5)
        v = v_ref[:, v_cols]

        decay = jnp.where(diff >= 0.0, jnp.exp(jnp.maximum(diff, 0.0) * log_g), 0.0)
        q_dec = jnp.exp((n_vec + 1.0) * log_g)
        k_dec = jnp.exp((ch - 1.0 - n_vec) * log_g)
        c_dec = jnp.exp(ch * log_g)

        qb = q.astype(BF16)
        state = state_ref[h]
        scores = _dot_nt(qb, k.astype(BF16)) * decay
        y = _dot(scores.astype(BF16), v) + _dot(qb, state.astype(BF16)) * q_dec
        state_ref[h] = state * c_dec + lax.dot_general((k * k_dec).astype(BF16), v, _TN,
                                                       preferred_element_type=F32)

        mu = jnp.mean(y, axis=-1, keepdims=True)
        yc = y - mu
        var = jnp.mean(yc * yc, axis=-1, keepdims=True)
        yn = yc * lax.rsqrt(var + RMS_EPS) * gn_ref[h]
        gate = g_ref[:, v_cols].astype(F32)
        o_ref[:, v_cols] = (gate * _sigmoid(gate) * yn).astype(o_ref.dtype)


def _retention(proj, cos_r, sin_r, gn, batch, seq):
    ch = RET_CHUNK
    nch = seq // ch
    assert seq % ch == 0
    m = batch * seq
    log_g = jnp.log(1.0 - jnp.exp2(-5.0 - jnp.arange(RET_HEADS, dtype=F32)))
    log_g = jnp.broadcast_to(log_g[:, None, None], (RET_HEADS, 1, LANE))
    qk_w = RET_HEADS * RET_DK
    v_w = RET_HEADS * RET_DV
    assert v_w == 2 * qk_w
    rows = lambda b, c: b * nch + c
    return pl.pallas_call(
        _retention_kernel, grid=(batch, nch),
        in_specs=[pl.BlockSpec((ch, qk_w), lambda b, c: (rows(b, c), 0)),
                  pl.BlockSpec((ch, qk_w), lambda b, c: (rows(b, c), 1)),
                  pl.BlockSpec((ch, v_w), lambda b, c: (rows(b, c), 1)),
                  pl.BlockSpec((ch, v_w), lambda b, c: (rows(b, c), 2)),
                  pl.BlockSpec((ch, LANE), lambda b, c: (rows(b, c), 0)),
                  pl.BlockSpec((ch, LANE), lambda b, c: (rows(b, c), 0)),
                  pl.BlockSpec((RET_HEADS, 1, LANE), lambda b, c: (0, 0, 0)),
                  pl.BlockSpec((RET_HEADS, 1, RET_DV), lambda b, c: (0, 0, 0))],
        out_specs=pl.BlockSpec((ch, v_w), lambda b, c: (rows(b, c), 0)),
        out_shape=jax.ShapeDtypeStruct((m, v_w), BF16),
        scratch_shapes=[pltpu.VMEM((RET_HEADS, RET_DK, RET_DV), F32)],
        compiler_params=_params("parallel", "arbitrary"), name="retention",
    )(proj, proj, proj, proj, cos_r, sin_r, log_g, gn.reshape(RET_HEADS, 1, RET_DV).astype(F32))


def _cross_kernel(h_ref, g_ref, wq_ref, kv_ref, wo_ref, o_ref):
    h = h_ref[...]
    hn = _rms_rows(h, g_ref[...]).astype(BF16)
    q = (_dot(hn, wq_ref[...]) * Q_SCALE).astype(BF16)
    kv_cols = X_HEADS * HEAD_DIM
    outs = []
    for hd in range(X_HEADS):
        cols = slice(hd * HEAD_DIM, (hd + 1) * HEAD_DIM)
        s = _dot_nt(q[:, cols], kv_ref[:, cols])
        mx = jnp.max(s, axis=-1, keepdims=True)
        p = jnp.exp2(s - mx)
        l = jnp.sum(p, axis=-1, keepdims=True)
        p = p / l
        outs.append(_dot(p.astype(BF16), kv_ref[:, kv_cols + hd * HEAD_DIM:kv_cols + (hd + 1) * HEAD_DIM]))
    o = jnp.concatenate(outs, axis=-1).astype(BF16)
    o_ref[...] = h + _dot(o, wo_ref[...])


def _cross(h, g, wq, kv, wo, layer, batch, seq, n_mem, *, tm):
    m, d = h.shape
    per_b = seq // tm
    kv_cols = 2 * X_HEADS * HEAD_DIM
    return pl.pallas_call(
        _cross_kernel, grid=(m // tm,),
        in_specs=[pl.BlockSpec((tm, d), lambda i: (i, 0)),
                  pl.BlockSpec((1, d), lambda i: (0, 0)),
                  pl.BlockSpec((d, X_HEADS * HEAD_DIM), lambda i: (0, 0)),
                  pl.BlockSpec((n_mem, kv_cols), lambda i: (i // per_b, layer)),
                  pl.BlockSpec((X_HEADS * HEAD_DIM, d), lambda i: (0, 0))],
        out_specs=pl.BlockSpec((tm, d), lambda i: (i, 0)),
        out_shape=jax.ShapeDtypeStruct((m, d), F32),
        compiler_params=_params("parallel"), name="cross_attn",
    )(h, g.reshape(1, d), wq, kv, wo)


def _ffn_kernel(h_ref, halo_ref, g_ref, wg_ref, wv_ref, cwg_ref, cwv_ref, cbg_ref, cbv_ref, wd_ref, gf_ref,
                o_ref, hn_ref, acc_ref, *, per_b, final_norm):
    i = pl.program_id(0)
    f = pl.program_id(1)
    tm = h_ref.shape[0]
    pad = SUBLANE * 2

    @pl.when(f == 0)
    def _():
        g = g_ref[...]
        hn_ref[pad:, :] = _rms_rows(h_ref[...], g).astype(BF16)
        halo = _rms_rows(halo_ref[...], g)
        halo = jnp.where(i % per_b == 0, 0.0, halo)
        hn_ref[:pad, :] = halo.astype(BF16)
        acc_ref[...] = jnp.zeros_like(acc_ref)

    hn = hn_ref[...]

    def conv(u, w, b):
        return (b + w[0:1, :] * u[pad - 2:pad - 2 + tm] + w[1:2, :] * u[pad - 1:pad - 1 + tm]
                + w[2:3, :] * u[pad:pad + tm])

    gate = conv(_dot(hn, wg_ref[...]), cwg_ref[...], cbg_ref[...])
    val = conv(_dot(hn, wv_ref[...]), cwv_ref[...], cbv_ref[...])
    act = (gate * _sigmoid(gate) * val).astype(BF16)
    acc_ref[...] += _dot(act, wd_ref[...])

    @pl.when(f == pl.num_programs(1) - 1)
    def _():
        slab = SUBLANE * 16
        assert tm % slab == 0

        def finish(r, carry):
            rows = pl.ds(pl.multiple_of(r * slab, slab), slab)
            out = h_ref[rows, :] + acc_ref[rows, :]
            o_ref[rows, :] = _rms_rows(out, gf_ref[...]) if final_norm else out
            return carry

        lax.fori_loop(0, tm // slab, finish, 0)


def _ffn(h, g, w_up, conv_w, conv_b, w_down, g_final, batch, seq, *, tm, tf):
    m, d = h.shape
    dff = w_down.shape[0]
    assert seq % tm == 0 and dff % tf == 0
    per_b = seq // tm
    nf = dff // tf
    pad = SUBLANE * 2
    halo_blocks = tm // pad
    cw = jnp.zeros((SUBLANE, 2 * dff), F32).at[:CONV_WIDTH].set(conv_w)
    cb = conv_b.reshape(1, 2 * dff)
    return pl.pallas_call(
        functools.partial(_ffn_kernel, per_b=per_b, final_norm=g_final is not None),
        grid=(m // tm, nf),
        in_specs=[pl.BlockSpec((tm, d), lambda i, f: (i, 0)),
                  pl.BlockSpec((pad, d), lambda i, f: (jnp.maximum(i * halo_blocks - 1, 0), 0)),
                  pl.BlockSpec((1, d), lambda i, f: (0, 0)),
                  pl.BlockSpec((d, tf), lambda i, f: (0, f)),
                  pl.BlockSpec((d, tf), lambda i, f: (0, nf + f)),
                  pl.BlockSpec((SUBLANE, tf), lambda i, f: (0, f)),
                  pl.BlockSpec((SUBLANE, tf), lambda i, f: (0, nf + f)),
                  pl.BlockSpec((1, tf), lambda i, f: (0, f)),
                  pl.BlockSpec((1, tf), lambda i, f: (0, nf + f)),
                  pl.BlockSpec((tf, d), lambda i, f: (f, 0)),
                  pl.BlockSpec((1, d), lambda i, f: (0, 0))],
        out_specs=pl.BlockSpec((tm, d), lambda i, f: (i, 0), pipeline_mode=pl.Buffered(1)),
        out_shape=jax.ShapeDtypeStruct((m, d), F32),
        scratch_shapes=[pltpu.VMEM((tm + pad, d), BF16), pltpu.VMEM((tm, d), F32)],
        compiler_params=_params("parallel", "arbitrary"), name="conv_ffn",
    )(h, h, g.reshape(1, d), w_up, w_up, cw, cw, cb, cb, w_down,
      (g if g_final is None else g_final).reshape(1, d))


def _row_tile(m, want):
    return want if m % want == 0 else m


def _mixer_ab(h, g, w_in, pe, w1, w2, w_out, tabs, rot, batch, seq):
    m, d = h.shape
    cos, sin = tabs[0], tabs[1]
    w_pad = jnp.zeros((d, AB_BLOCKS * LANE), BF16).at[:, :AB_COLS].set(w_in.astype(BF16))
    proj = _norm_matmul(h, g, w_pad, tm=_row_tile(m, 1024), tn=2048, head_major=True)
    o_moba = _moba(proj, cos, sin, rot, batch, seq)
    kvc = _compress(proj, pe, w1, w2, batch, seq)
    o_cmp, sel = _nsa_cmp(proj, kvc, batch, seq)
    o_nsa = _nsa_sw(proj, sel, o_cmp, cos, sin, rot, batch, seq)
    return _matmul_res([o_moba, o_nsa], w_out.astype(BF16), h, tm=_row_tile(m, 1024), tn=1024)


def _mixer_c(h, g, w_in, gn, w_out, tabs, batch, seq):
    m, d = h.shape
    proj = _norm_matmul(h, g, w_in.astype(BF16), tm=_row_tile(m, 1024), tn=2048)
    y = _retention(proj, tabs[2], tabs[3], gn, batch, seq)
    return _matmul_res([y], w_out.astype(BF16), h, tm=_row_tile(m, 1024), tn=1024)


def kernel(x, mem, positions, norm_mix, norm_cross, norm_ffn, norm_mem, norm_final, w_in_ab, cmp_pe_k, cmp_w1_k,
           cmp_w2_k, cmp_pe_v, cmp_w1_v, cmp_w2_v, w_out_ab, w_in_c, ret_gn, w_out_c, w_q_x, w_kv_x, w_o_x, w_up,
           conv_w, conv_b, w_down):
    batch, seq, d = x.shape
    n_mem = mem.shape[1]
    depth = norm_mix.shape[0]
    m = batch * seq
    tabs = _rope_tables(positions)
    rot = _rot_matrix()

    kv = _norm_matmul(mem.reshape(batch * n_mem, d), norm_mem, w_kv_x.astype(BF16),
                      tm=_row_tile(batch * n_mem, 512), tn=512)

    h = x.reshape(m, d)
    for l in range(depth):
        if l % 2 == 0:
            e = l // 2
            h = _mixer_ab(h, norm_mix[l], w_in_ab[e],
                          jnp.stack([cmp_pe_k[e], cmp_pe_v[e]]), jnp.stack([cmp_w1_k[e], cmp_w1_v[e]]),
                          jnp.stack([cmp_w2_k[e], cmp_w2_v[e]]), w_out_ab[e], tabs, rot, batch, seq)
        else:
            o = l // 2
            h = _mixer_c(h, norm_mix[l], w_in_c[o], ret_gn[o], w_out_c[o], tabs, batch, seq)
        h = _cross(h, norm_cross[l], w_q_x[l].astype(BF16), kv, w_o_x[l].astype(BF16), l, batch, seq, n_mem,
                   tm=_row_tile(seq, 512))
        h = _ffn(h, norm_ffn[l], w_up[l].astype(BF16), conv_w[l], conv_b[l], w_down[l].astype(BF16),
                 norm_final if l == depth - 1 else None, batch, seq, tm=_row_tile(seq, 1024), tf=512)
    return h.reshape(batch, seq, d)
```

```python
import functools

import numpy as np
import jax
import jax.numpy as jnp
from jax import lax
from jax.experimental import pallas as pl
from jax.experimental.pallas import tpu as pltpu

F32 = jnp.float32
BF16 = jnp.bfloat16

D_MODEL = 2048
DEPTH = 4
HEAD_DIM = 128
ROPE_THETA = 500000.0
ROPE_DIM = HEAD_DIM // 4
X_HEADS = 4
MOBA_HEADS = 8
MOBA_BLOCK = 256
MOBA_TOPK = 3
NSA_HEADS = 8
NSA_GROUPS = 2
NSA_REP = NSA_HEADS // NSA_GROUPS
NSA_CMP_LEN = 32
NSA_CMP_STRIDE = 16
NSA_SEL_LEN = 64
NSA_TOPK = 16
NSA_WINDOW = 512
RET_HEADS = 8
RET_DK = 256
RET_DV = 512
RET_THETA = 10000.0
D_FF = 5632
CONV_WIDTH = 3
RMS_EPS = 1e-6
NEG = -1e30
FORCE = 1e9
LOG2E = 1.4426950408889634
Q_SCALE = HEAD_DIM ** -0.5 * LOG2E

LANE = 128
SUBLANE = 8
VMEM_LIMIT = 56 * 2 ** 20

AB_SIZES = (MOBA_HEADS * HEAD_DIM,) * 3 + (NSA_HEADS * HEAD_DIM,) + (NSA_GROUPS * HEAD_DIM,) * 6 + (NSA_HEADS * 3,)
AB_COLS = sum(AB_SIZES)
AB_BLOCKS = 48
BLK_MQ, BLK_MK, BLK_MV, BLK_NQ = 0, 8, 16, 24
BLK_NKC, BLK_NVC, BLK_NKS, BLK_NVS, BLK_NKW, BLK_NVW, BLK_NG = 32, 34, 36, 38, 40, 42, 44

ATT_TQ = 256
ATT_TK = 256
RET_CHUNK = 256
MOBA_GROUP = 4
MOBA_HB = 4
SEL_GROUP = 4

_NT = (((1,), (1,)), ((), ()))
_TN = (((0,), (0,)), ((), ()))


def _params(*sem):
    return pltpu.CompilerParams(dimension_semantics=sem, vmem_limit_bytes=VMEM_LIMIT)


def _dot(a, b):
    return jnp.dot(a, b, preferred_element_type=F32)


def _dot_nt(a, b):
    return lax.dot_general(a, b, _NT, preferred_element_type=F32)


def _sigmoid(x):
    return 1.0 / (1.0 + jnp.exp(-x))


def _rms_rows(x, g):
    ms = jnp.mean(x * x, axis=-1, keepdims=True)
    return x * lax.rsqrt(ms + RMS_EPS) * g


def _tables_kernel(pos_ref, inv_rope_ref, inv_ret_ref, c_ref, s_ref, cr_ref, sr_ref):
    pos = pos_ref[...]
    lane = lax.broadcasted_iota(jnp.int32, pos.shape, 1)
    rot = lane < ROPE_DIM
    ang = pos * inv_rope_ref[...]
    c_ref[...] = jnp.where(rot, jnp.cos(ang), 1.0)
    s_ref[...] = jnp.where(rot, jnp.sin(ang), 0.0)
    ang_r = pos * inv_ret_ref[...]
    cr_ref[...] = jnp.cos(ang_r)
    sr_ref[...] = jnp.sin(ang_r)


def _rope_tables(positions):
    m = positions.size
    posb = jnp.broadcast_to(positions.reshape(m, 1).astype(F32), (m, LANE))
    half = ROPE_DIM // 2
    inv = jnp.float32(ROPE_THETA) ** (-jnp.arange(half, dtype=F32) / half)
    inv_rope = jnp.concatenate([inv, inv, jnp.zeros((LANE - ROPE_DIM,), F32)]).reshape(1, LANE)
    half_r = RET_DK // 2
    inv_ret = (jnp.float32(RET_THETA) ** (-jnp.arange(half_r, dtype=F32) / half_r)).reshape(1, LANE)
    tm = 1024 if m % 1024 == 0 else m
    row = pl.BlockSpec((tm, LANE), lambda i: (i, 0))
    vec = pl.BlockSpec((1, LANE), lambda i: (0, 0))
    sds = jax.ShapeDtypeStruct((m, LANE), F32)
    return pl.pallas_call(
        _tables_kernel, grid=(m // tm,), in_specs=[row, vec, vec], out_specs=[row] * 4,
        out_shape=[sds] * 4, compiler_params=_params("parallel"), name="rope_tables",
    )(posb, inv_rope, inv_ret)


def _rot_matrix():
    half = ROPE_DIM // 2
    p = np.zeros((HEAD_DIM, HEAD_DIM), np.float32)
    for l in range(half):
        p[l + half, l] = -1.0
        p[l, l + half] = 1.0
    return jnp.asarray(p, BF16)


def _rope_partial(x_bf16, c, s, p_ref):
    return x_bf16.astype(F32) * c + _dot(x_bf16, p_ref[...]) * s


def _norm_matmul_kernel(x_ref, g_ref, w_ref, o_ref, xn_ref, *, head_major):
    @pl.when(pl.program_id(1) == 0)
    def _():
        xn_ref[...] = _rms_rows(x_ref[...], g_ref[...]).astype(BF16)

    acc = _dot(xn_ref[...], w_ref[...])
    if head_major:
        for c in range(o_ref.shape[0]):
            o_ref[c] = acc[:, c * LANE:(c + 1) * LANE].astype(o_ref.dtype)
    else:
        o_ref[...] = acc.astype(o_ref.dtype)


def _norm_matmul(x, g, w, *, tm, tn, head_major=False):
    stack = w.shape[0] if w.ndim == 3 else 1
    m, k = x.shape
    per_w = w.shape[-1] // tn
    n = stack * w.shape[-1]
    assert m % tm == 0 and w.shape[-1] % tn == 0 and tn % LANE == 0
    if head_major:
        out_shape = jax.ShapeDtypeStruct((n // LANE, m, LANE), BF16)
        out_spec = pl.BlockSpec((tn // LANE, tm, LANE), lambda i, j: (j, i, 0))
    else:
        out_shape = jax.ShapeDtypeStruct((m, n), BF16)
        out_spec = pl.BlockSpec((tm, tn), lambda i, j: (i, j))
    return pl.pallas_call(
        functools.partial(_norm_matmul_kernel, head_major=head_major),
        grid=(m // tm, n // tn),
        in_specs=[pl.BlockSpec((tm, k), lambda i, j: (i, 0)),
                  pl.BlockSpec((1, k), lambda i, j: (0, 0)),
                  pl.BlockSpec((k, tn), lambda i, j: (0, j)) if w.ndim == 2 else
                  pl.BlockSpec((None, k, tn), lambda i, j: (j // per_w, 0, j % per_w))],
        out_specs=out_spec, out_shape=out_shape,
        scratch_shapes=[pltpu.VMEM((tm, k), BF16)],
        compiler_params=_params("parallel", "arbitrary"), name="norm_matmul",
    )(x, g.reshape(1, k), w)


def _matmul_res_kernel(*refs):
    r_ref, o_ref = refs[-2], refs[-1]
    acc = r_ref[...]
    for x_ref, w_ref in zip(refs[:-2:2], refs[1:-2:2]):
        acc = acc + _dot(x_ref[...], w_ref[...])
    o_ref[...] = acc


def _matmul_res(xs, w, res, *, tm, tn):
    m, n = res.shape
    k = xs[0].shape[1]
    assert m % tm == 0 and n % tn == 0 and all(x.shape[1] == k for x in xs) and w.shape[0] == k * len(xs)
    in_specs, args = [], []
    for idx, x in enumerate(xs):
        in_specs += [pl.BlockSpec((tm, k), lambda i, j: (i, 0)),
                     pl.BlockSpec((k, tn), lambda i, j, idx=idx: (idx, j))]
        args += [x, w]
    return pl.pallas_call(
        _matmul_res_kernel, grid=(m // tm, n // tn),
        in_specs=in_specs + [pl.BlockSpec((tm, tn), lambda i, j: (i, j))],
        out_specs=pl.BlockSpec((tm, tn), lambda i, j: (i, j)),
        out_shape=jax.ShapeDtypeStruct((m, n), F32),
        compiler_params=_params("parallel", "arbitrary"), name="matmul_res",
    )(*args, res)


def _topk_rows(x, n_valid, k, n_rows):
    rows, nq = x.shape
    ridx = lax.broadcasted_iota(jnp.int32, (rows, nq), 0)
    x = jnp.where(ridx < n_valid, x, NEG)
    x = jnp.where(ridx < n_rows, x, -jnp.inf)
    rank = jnp.zeros((rows, nq), F32)
    for i in range(n_rows):
        xi = x[i:i + 1, :]
        rank = rank + jnp.where(ridx > i, jnp.where(xi >= x, 1.0, 0.0), jnp.where(xi > x, 1.0, 0.0))
    marks = jnp.where(rank < float(k), jnp.where(x > NEG / 2, 1.0, 0.0), 0.0)
    if rows < LANE:
        marks = jnp.concatenate([marks, jnp.zeros((LANE - rows, nq), F32)], axis=0)
    return marks.T


def _moba_kernel(q_ref, k_ref, v_ref, c_ref, s_ref, p_ref, o_ref, krot_ref, kmean_ref, vext_ref, *, nb):
    qi = pl.program_id(2)
    blk = MOBA_BLOCK
    nb_pad = -(-nb // SUBLANE) * SUBLANE
    grp = MOBA_GROUP
    heads = range(MOBA_HB)

    @pl.when(qi == 0)
    def _():
        kmean_ref[...] = jnp.zeros_like(kmean_ref)
        for hh in heads:
            for j in range(nb):
                rows = slice(j * blk, (j + 1) * blk)
                kr = _rope_partial(k_ref[hh, rows, :], c_ref[rows, :], s_ref[rows, :], p_ref)
                krot_ref[hh, rows, :] = kr.astype(BF16)
                kmean_ref[hh, j:j + 1, :] = jnp.mean(kr, axis=0, keepdims=True)
            vext_ref[hh, :, :HEAD_DIM] = v_ref[hh]
            vext_ref[hh, :, HEAD_DIM:] = jnp.ones((v_ref.shape[1], HEAD_DIM), BF16)

    r0 = pl.multiple_of(qi * blk, blk)
    cq = c_ref[pl.ds(r0, blk), :]
    sq = s_ref[pl.ds(r0, blk), :]
    lane = lax.broadcasted_iota(jnp.int32, (blk, LANE), 1)
    row = lax.broadcasted_iota(jnp.int32, (blk, blk), 0)
    col = lax.broadcasted_iota(jnp.int32, (blk, blk), 1)

    qbs, sels, state = [], [], []
    for hh in heads:
        q = _rope_partial(q_ref[hh], cq, sq, p_ref)
        qb = (q * Q_SCALE).astype(BF16)
        gate_t = lax.dot_general(kmean_ref[hh], q, _NT, precision=lax.Precision.HIGHEST,
                                 preferred_element_type=F32)[:nb_pad, :]
        sels.append(_topk_rows(gate_t, qi, min(MOBA_TOPK, nb), nb))
        qbs.append(qb)
        s = jnp.where(col <= row, _dot_nt(qb, krot_ref[hh, pl.ds(r0, blk), :]), NEG)
        m = jnp.max(s, axis=-1, keepdims=True)
        p = jnp.exp2((s - m).astype(BF16))
        state += [m, _dot(p, vext_ref[hh, pl.ds(r0, blk), :])]

    def body(jg, carry):
        off = pl.multiple_of(jg * (grp * blk), grp * blk)
        out = []
        for hh in heads:
            m, acc = carry[2 * hh:2 * hh + 2]
            s = _dot_nt(qbs[hh], krot_ref[hh, pl.ds(off, grp * blk), :])
            parts = []
            for t in range(grp):
                chosen = jnp.sum(jnp.where(lane == jg * grp + t, sels[hh], 0.0), axis=-1, keepdims=True)
                parts.append(jnp.where(chosen > 0.5, s[:, t * blk:(t + 1) * blk], NEG))
            s = jnp.concatenate(parts, axis=-1)
            m_new = jnp.maximum(m, jnp.max(s, axis=-1, keepdims=True))
            p = jnp.exp2((s - m_new).astype(BF16))
            out += [m_new, jnp.exp2(m - m_new) * acc + _dot(p, vext_ref[hh, pl.ds(off, grp * blk), :])]
        return tuple(out)

    state = lax.fori_loop(0, (qi + grp - 1) // grp, body, tuple(state))
    for hh in heads:
        acc = state[2 * hh + 1]
        o_ref[:, hh * HEAD_DIM:(hh + 1) * HEAD_DIM] = (acc[:, :HEAD_DIM] / acc[:, HEAD_DIM:HEAD_DIM + 1]).astype(o_ref.dtype)


def _moba(proj, cos, sin, rot, batch, seq):
    nb = seq // MOBA_BLOCK
    hb = MOBA_HB
    assert seq % (MOBA_BLOCK * MOBA_GROUP) == 0 and nb <= LANE
    assert MOBA_HEADS % hb == 0 and BLK_MQ % hb == 0 and BLK_MK % hb == 0 and BLK_MV % hb == 0
    m = batch * seq
    head = lambda off: pl.BlockSpec((hb, seq, HEAD_DIM), lambda b, h, i: (off // hb + h, b, 0))
    table = pl.BlockSpec((seq, LANE), lambda b, h, i: (b, 0))
    return pl.pallas_call(
        functools.partial(_moba_kernel, nb=nb),
        grid=(batch, MOBA_HEADS // hb, nb),
        in_specs=[pl.BlockSpec((hb, MOBA_BLOCK, HEAD_DIM), lambda b, h, i: (BLK_MQ // hb + h, b * nb + i, 0)),
                  head(BLK_MK), head(BLK_MV), table, table,
                  pl.BlockSpec((HEAD_DIM, HEAD_DIM), lambda b, h, i: (0, 0))],
        out_specs=pl.BlockSpec((MOBA_BLOCK, hb * HEAD_DIM), lambda b, h, i: (b * nb + i, h)),
        out_shape=jax.ShapeDtypeStruct((m, MOBA_HEADS * HEAD_DIM), BF16),
        scratch_shapes=[pltpu.VMEM((hb, seq, HEAD_DIM), BF16), pltpu.VMEM((hb, LANE, HEAD_DIM), F32),
                        pltpu.VMEM((hb, seq, 2 * HEAD_DIM), BF16)],
        compiler_params=_params("parallel", "parallel", "arbitrary"), name="moba",
    )(proj, proj, proj, cos, sin, rot)


def _compress_kernel(x_ref, pe_ref, w1_ref, w2_ref, o_ref, xf_ref):
    seq = x_ref.shape[0]
    n_sub = seq // NSA_CMP_STRIDE
    xf_ref[:seq, :] = x_ref[...].astype(F32)
    xf_ref[seq:, :] = jnp.zeros((NSA_CMP_STRIDE, HEAD_DIM), F32)
    hid = _dot(pe_ref[...], w1_ref[...])[0:1, :]
    for r in range(NSA_CMP_LEN):
        rows = xf_ref[pl.ds(r, n_sub, stride=NSA_CMP_STRIDE), :]
        hid = hid + _dot(rows.astype(BF16), w1_ref[r * HEAD_DIM:(r + 1) * HEAD_DIM, :])
    hid = hid * _sigmoid(hid)
    o_ref[...] = _dot(hid.astype(BF16), w2_ref[...]).astype(o_ref.dtype)


def _compress(proj, pe, w1, w2, batch, seq):
    n_sub = seq // NSA_CMP_STRIDE
    flat = NSA_CMP_LEN * HEAD_DIM
    pe_flat = jnp.zeros((2, SUBLANE * 2, flat), BF16).at[:, 0].set(pe.reshape(2, flat).astype(BF16))
    return pl.pallas_call(
        _compress_kernel, grid=(2, batch, NSA_GROUPS),
        in_specs=[pl.BlockSpec((None, seq, HEAD_DIM), lambda t, b, g: (BLK_NKC + 2 * t + g, b, 0)),
                  pl.BlockSpec((None, SUBLANE * 2, flat), lambda t, b, g: (t, 0, 0)),
                  pl.BlockSpec((None, flat, HEAD_DIM), lambda t, b, g: (t, 0, 0)),
                  pl.BlockSpec((None, HEAD_DIM, HEAD_DIM), lambda t, b, g: (t, 0, 0))],
        out_specs=pl.BlockSpec((None, None, None, n_sub, HEAD_DIM), lambda t, b, g: (t, b, g, 0, 0)),
        out_shape=jax.ShapeDtypeStruct((2, batch, NSA_GROUPS, n_sub, HEAD_DIM), BF16),
        scratch_shapes=[pltpu.VMEM((seq + NSA_CMP_STRIDE, HEAD_DIM), F32)],
        compiler_params=_params("parallel", "parallel", "parallel"), name="nsa_compress",
    )(proj, pe_flat, w1.astype(BF16), w2.astype(BF16))


def _nsa_cmp_kernel(q_ref, kc_ref, vc_ref, cov_ref, oc_ref, sel_ref, *, n_sel):
    qi = pl.program_id(2)
    tq = ATT_TQ
    n_cmp = kc_ref.shape[0]
    scale = HEAD_DIM ** -0.5
    q_pos = qi * tq + lax.broadcasted_iota(jnp.int32, (tq, n_cmp), 0)
    n_idx = lax.broadcasted_iota(jnp.int32, (tq, n_cmp), 1)
    ok = n_idx * NSA_CMP_STRIDE + (NSA_CMP_LEN - 1) <= q_pos
    kc = kc_ref[...]
    vc = vc_ref[...]
    p_sum = jnp.zeros((tq, n_cmp), F32)
    for r in range(NSA_REP):
        s = jnp.where(ok, _dot_nt(q_ref[r], kc) * scale, NEG)
        mx = jnp.max(s, axis=-1, keepdims=True)
        e = jnp.where(ok, jnp.exp(s - mx), 0.0)
        l = jnp.sum(e, axis=-1, keepdims=True)
        p = e * jnp.where(l > 0.0, 1.0 / l, 0.0)
        oc_ref[r] = _dot(p.astype(BF16), vc).astype(oc_ref.dtype)
        p_sum = p_sum + p

    n_rows = -(-n_sel // SUBLANE) * SUBLANE
    imp = lax.dot_general(cov_ref[...], p_sum, _NT, precision=lax.Precision.HIGHEST,
                          preferred_element_type=F32)[:n_rows, :]
    ridx = lax.broadcasted_iota(jnp.int32, (n_rows, tq), 0)
    blk = (qi * tq + lax.broadcasted_iota(jnp.int32, (n_rows, tq), 1)) // NSA_SEL_LEN
    forced = (ridx == 0) | (ridx == blk) | (ridx == blk - 1)
    x = jnp.where(forced, FORCE, imp)
    sel_ref[...] = _topk_rows(x, blk[0:1, :] + 1, min(NSA_TOPK, n_sel), n_sel).astype(sel_ref.dtype)


def _cover_matrix(n_cmp_pad, n_cmp, n_sel):
    c_start = np.arange(n_cmp_pad) * NSA_CMP_STRIDE
    s_start = np.arange(LANE) * NSA_SEL_LEN
    cover = (c_start[:, None] < s_start[None, :] + NSA_SEL_LEN) & (c_start[:, None] + NSA_CMP_LEN > s_start[None, :])
    cover &= (np.arange(n_cmp_pad) < n_cmp)[:, None] & (np.arange(LANE) < n_sel)[None, :]
    return jnp.asarray(cover.T, F32)


def _nsa_cmp(proj, kvc, batch, seq):
    nq = seq // ATT_TQ
    n_sub = seq // NSA_CMP_STRIDE
    n_sel = seq // NSA_SEL_LEN
    assert n_sel <= LANE
    m = batch * seq
    cover = _cover_matrix(n_sub, n_sub - NSA_CMP_LEN // NSA_CMP_STRIDE + 1, n_sel)
    cmp_spec = lambda t: pl.BlockSpec((None, None, None, n_sub, HEAD_DIM), lambda b, g, i: (t, b, g, 0, 0))
    return pl.pallas_call(
        functools.partial(_nsa_cmp_kernel, n_sel=n_sel),
        grid=(batch, NSA_GROUPS, nq),
        in_specs=[pl.BlockSpec((NSA_REP, ATT_TQ, HEAD_DIM), lambda b, g, i: (BLK_NQ // NSA_REP + g, b * nq + i, 0)),
                  cmp_spec(0), cmp_spec(1),
                  pl.BlockSpec((LANE, n_sub), lambda b, g, i: (0, 0))],
        out_specs=[pl.BlockSpec((NSA_REP, ATT_TQ, HEAD_DIM), lambda b, g, i: (g, b * nq + i, 0)),
                   pl.BlockSpec((ATT_TQ, LANE), lambda b, g, i: (b * nq + i, g))],
        out_shape=[jax.ShapeDtypeStruct((NSA_HEADS, m, HEAD_DIM), BF16),
                   jax.ShapeDtypeStruct((m, NSA_GROUPS * LANE), BF16)],
        compiler_params=_params("parallel", "parallel", "parallel"), name="nsa_cmp",
    )(proj, kvc, kvc, cover)


def _nsa_sw_kernel(q_ref, ks_ref, vs_ref, kw_ref, vw_ref, sel_ref, oc_ref, gl_ref, c_ref, s_ref, p_ref,
                   o_ref, ksr_ref, kwr_ref, vwp_ref, vse_ref, qr_ref, m_ref, acc_ref, ow_ref, *, seq):
    grp = pl.program_id(1)
    qi = pl.program_id(2)
    tq, tk = ATT_TQ, ATT_TK
    rep = NSA_REP
    win = NSA_WINDOW
    gk = SEL_GROUP * tk

    @pl.when(qi == 0)
    def _():
        kwr_ref[:win, :] = jnp.zeros((win, HEAD_DIM), BF16)
        vwp_ref[:win, :] = jnp.zeros((win, 2 * HEAD_DIM), BF16)
        vwp_ref[win:, HEAD_DIM:] = jnp.ones((seq, HEAD_DIM), BF16)
        vse_ref[:, :HEAD_DIM] = vs_ref[...]
        vse_ref[:, HEAD_DIM:] = jnp.ones((seq, HEAD_DIM), BF16)
        for j in range(seq // tk):
            rows = slice(j * tk, (j + 1) * tk)
            shifted = slice(win + j * tk, win + (j + 1) * tk)
            c, s = c_ref[rows, :], s_ref[rows, :]
            ksr_ref[rows, :] = _rope_partial(ks_ref[rows, :], c, s, p_ref).astype(BF16)
            kwr_ref[shifted, :] = _rope_partial(kw_ref[rows, :], c, s, p_ref).astype(BF16)
            vwp_ref[shifted, :HEAD_DIM] = vw_ref[rows, :]

    r0 = pl.multiple_of(qi * tq, tq)
    cq = c_ref[pl.ds(r0, tq), :]
    sq = s_ref[pl.ds(r0, tq), :]
    for r in range(rep):
        qr_ref[r] = (_rope_partial(q_ref[r], cq, sq, p_ref) * Q_SCALE).astype(BF16)

    q_all = qr_ref[...].reshape(rep * tq, HEAD_DIM)

    row_w = lax.broadcasted_iota(jnp.int32, (tq, win + tq), 0)
    col_w = lax.broadcasted_iota(jnp.int32, (tq, win + tq), 1)
    in_win = (col_w > row_w) & (col_w <= row_w + win) & (col_w >= win - r0)
    s = _dot_nt(q_all, kwr_ref[pl.ds(r0, win + tq), :]).reshape(rep, tq, win + tq)
    s = jnp.where(in_win[None], s, NEG)
    p = jnp.exp2((s - jnp.max(s, axis=-1, keepdims=True)).astype(BF16))
    o_w = _dot(p.reshape(rep * tq, win + tq), vwp_ref[pl.ds(r0, win + tq), :]).reshape(rep, tq, 2 * HEAD_DIM)
    ow_ref[...] = o_w[..., :HEAD_DIM] / o_w[..., HEAD_DIM:HEAD_DIM + 1]

    sel = sel_ref[...]
    row = lax.broadcasted_iota(jnp.int32, (tq, gk), 0)
    col = lax.broadcasted_iota(jnp.int32, (tq, gk), 1)

    def chosen_keys(jg):
        blk_row = lax.broadcasted_iota(jnp.int32, (LANE, gk), 0)
        blk_col = jg * (gk // NSA_SEL_LEN) + lax.broadcasted_iota(jnp.int32, (LANE, gk), 1) // NSA_SEL_LEN
        return _dot(sel, jnp.where(blk_row == blk_col, 1.0, 0.0).astype(BF16))

    def attend(off, valid, first):
        s = _dot_nt(q_all, ksr_ref[pl.ds(off, gk), :]).reshape(rep, tq, gk)
        s = jnp.where(valid[None], s, NEG)
        m_new = jnp.max(s, axis=-1, keepdims=True)
        if not first:
            m_old = m_ref[...]
            m_new = jnp.maximum(m_old, m_new)
        p = jnp.exp2((s - m_new).astype(BF16))
        v_grp = vse_ref[pl.ds(off, gk), :]
        for r in range(rep):
            pv = _dot(p[r], v_grp)
            if first:
                acc_ref[r] = pv
            else:
                acc_ref[r] = jnp.exp2(m_old[r] - m_new[r]) * acc_ref[r] + pv
        m_ref[...] = m_new

    gd = qi // SEL_GROUP
    off_d = pl.multiple_of(gd * gk, gk)
    causal = col + off_d <= row + r0
    attend(off_d, jnp.where(causal, chosen_keys(gd), 0.0) > 0.5, True)

    def sel_body(jg, carry):
        attend(pl.multiple_of(jg * gk, gk), chosen_keys(jg) > 0.5, False)
        return carry

    lax.fori_loop(0, gd, sel_body, 0)

    gates = _sigmoid(gl_ref[...].astype(F32))
    lane = lax.broadcasted_iota(jnp.int32, (tq, LANE), 1)

    def gate_col(c):
        return jnp.sum(jnp.where(lane == c, gates, 0.0), axis=-1, keepdims=True)

    for r in range(rep):
        base = (grp * rep + r) * 3
        o_s = acc_ref[r, :, :HEAD_DIM] / acc_ref[r, :, HEAD_DIM:HEAD_DIM + 1]
        o = gate_col(base) * oc_ref[r].astype(F32) + gate_col(base + 1) * o_s + gate_col(base + 2) * ow_ref[r]
        o_ref[:, r * HEAD_DIM:(r + 1) * HEAD_DIM] = o.astype(o_ref.dtype)


def _nsa_sw(proj, sel, o_cmp, cos, sin, rot, batch, seq):
    nq = seq // ATT_TQ
    assert seq % (SEL_GROUP * ATT_TK) == 0 and NSA_WINDOW % ATT_TK == 0 and ATT_TQ == ATT_TK
    m = batch * seq
    rep = NSA_REP
    head = lambda off: pl.BlockSpec((None, seq, HEAD_DIM), lambda b, g, i: (off + g, b, 0))
    table = pl.BlockSpec((seq, LANE), lambda b, g, i: (b, 0))
    q_like = lambda off: pl.BlockSpec((rep, ATT_TQ, HEAD_DIM), lambda b, g, i: (off + g, b * nq + i, 0))
    return pl.pallas_call(
        functools.partial(_nsa_sw_kernel, seq=seq),
        grid=(batch, NSA_GROUPS, nq),
        in_specs=[q_like(BLK_NQ // rep), head(BLK_NKS), head(BLK_NVS), head(BLK_NKW), head(BLK_NVW),
                  pl.BlockSpec((ATT_TQ, LANE), lambda b, g, i: (b * nq + i, g)),
                  q_like(0),
                  pl.BlockSpec((None, ATT_TQ, LANE), lambda b, g, i: (BLK_NG, b * nq + i, 0)),
                  table, table,
                  pl.BlockSpec((HEAD_DIM, HEAD_DIM), lambda b, g, i: (0, 0))],
        out_specs=pl.BlockSpec((ATT_TQ, rep * HEAD_DIM), lambda b, g, i: (b * nq + i, g)),
        out_shape=jax.ShapeDtypeStruct((m, NSA_HEADS * HEAD_DIM), BF16),
        scratch_shapes=[pltpu.VMEM((seq, HEAD_DIM), BF16),
                        pltpu.VMEM((seq + NSA_WINDOW, HEAD_DIM), BF16),
                        pltpu.VMEM((seq + NSA_WINDOW, 2 * HEAD_DIM), BF16),
                        pltpu.VMEM((seq, 2 * HEAD_DIM), BF16),
                        pltpu.VMEM((rep, ATT_TQ, HEAD_DIM), BF16),
                        pltpu.VMEM((rep, ATT_TQ, 1), F32),
                        pltpu.VMEM((rep, ATT_TQ, 2 * HEAD_DIM), F32), pltpu.VMEM((rep, ATT_TQ, HEAD_DIM), F32)],
        compiler_params=_params("parallel", "parallel", "arbitrary"), name="nsa_sel_win",
    )(proj, proj, proj, proj, proj, sel, o_cmp, proj, cos, sin, rot)


def _rope_full(x_bf16, c, s):
    half = x_bf16.shape[-1] // 2
    x = x_bf16.astype(F32)
    x1, x2 = x[:, :half], x[:, half:]
    return jnp.concatenate([x1 * c - x2 * s, x1 * s + x2 * c], axis=-1)


def _retention_kernel(q_ref, k_ref, v_ref, g_ref, c_ref, s_ref, lg_ref, gn_ref, o_ref, state_ref, decay_ref):
    ci = pl.program_id(1)
    ch = q_ref.shape[0]

    @pl.when(ci == 0)
    def _():
        state_ref[...] = jnp.zeros_like(state_ref)
        n_row = lax.broadcasted_iota(jnp.int32, (ch, ch), 0)
        n_col = lax.broadcasted_iota(jnp.int32, (ch, ch), 1)
        diff = (n_row - n_col).astype(F32)
        for h in range(RET_HEADS):
            decay_ref[h] = jnp.where(diff >= 0.0, jnp.exp(jnp.maximum(diff, 0.0) * lg_ref[h][:, 0:1]), 0.0)

    c, s = c_ref[...], s_ref[...]
    n_vec = lax.broadcasted_iota(jnp.int32, (ch, 1), 0).astype(F32)

    for h in range(RET_HEADS):
        log_g = lg_ref[h][:, 0:1]
        qk_cols = slice(h * RET_DK, (h + 1) * RET_DK)
        v_cols = slice(h * RET_DV, (h + 1) * RET_DV)
        q = _rope_full(q_ref[:, qk_cols], c, s)
        k = _rope_full(k_ref[:, qk_cols], c, s) * (RET_DK ** -0.5)
        v = v_ref[:, v_cols]

        q_dec = jnp.exp((n_vec + 1.0) * log_g)
        k_dec = jnp.exp((ch - 1.0 - n_vec) * log_g)
        c_dec = jnp.exp(ch * log_g)

        qb = q.astype(BF16)
        state = state_ref[h]
        scores = _dot_nt(qb, k.astype(BF16)) * decay_ref[h]
        y = _dot(scores.astype(BF16), v) + _dot(qb, state.astype(BF16)) * q_dec
        state_ref[h] = state * c_dec + lax.dot_general((k * k_dec).astype(BF16), v, _TN,
                                                       preferred_element_type=F32)

        mu = jnp.mean(y, axis=-1, keepdims=True)
        yc = y - mu
        var = jnp.mean(yc * yc, axis=-1, keepdims=True)
        yn = yc * lax.rsqrt(var + RMS_EPS) * gn_ref[h]
        gate = g_ref[:, v_cols].astype(F32)
        o_ref[:, v_cols] = (gate * _sigmoid(gate) * yn).astype(o_ref.dtype)


def _retention(proj, cos_r, sin_r, gn, batch, seq):
    ch = RET_CHUNK
    nch = seq // ch
    assert seq % ch == 0
    m = batch * seq
    log_g = jnp.log(1.0 - jnp.exp2(-5.0 - jnp.arange(RET_HEADS, dtype=F32)))
    log_g = jnp.broadcast_to(log_g[:, None, None], (RET_HEADS, 1, LANE))
    qk_w = RET_HEADS * RET_DK
    v_w = RET_HEADS * RET_DV
    assert v_w == 2 * qk_w
    rows = lambda b, c: b * nch + c
    return pl.pallas_call(
        _retention_kernel, grid=(batch, nch),
        in_specs=[pl.BlockSpec((ch, qk_w), lambda b, c: (rows(b, c), 0)),
                  pl.BlockSpec((ch, qk_w), lambda b, c: (rows(b, c), 1)),
                  pl.BlockSpec((ch, v_w), lambda b, c: (rows(b, c), 1)),
                  pl.BlockSpec((ch, v_w), lambda b, c: (rows(b, c), 2)),
                  pl.BlockSpec((ch, LANE), lambda b, c: (rows(b, c), 0)),
                  pl.BlockSpec((ch, LANE), lambda b, c: (rows(b, c), 0)),
                  pl.BlockSpec((RET_HEADS, 1, LANE), lambda b, c: (0, 0, 0)),
                  pl.BlockSpec((RET_HEADS, 1, RET_DV), lambda b, c: (0, 0, 0))],
        out_specs=pl.BlockSpec((ch, v_w), lambda b, c: (rows(b, c), 0)),
        out_shape=jax.ShapeDtypeStruct((m, v_w), BF16),
        scratch_shapes=[pltpu.VMEM((RET_HEADS, RET_DK, RET_DV), F32), pltpu.VMEM((RET_HEADS, ch, ch), F32)],
        compiler_params=_params("parallel", "arbitrary"), name="retention",
    )(proj, proj, proj, proj, cos_r, sin_r, log_g, gn.reshape(RET_HEADS, 1, RET_DV).astype(F32))


def _cross_kernel(h_ref, g_ref, wq_ref, kv_ref, wo_ref, o_ref):
    h = h_ref[...]
    hn = _rms_rows(h, g_ref[...]).astype(BF16)
    q = (_dot(hn, wq_ref[...]) * Q_SCALE).astype(BF16)
    kv_cols = X_HEADS * HEAD_DIM
    outs = []
    for hd in range(X_HEADS):
        cols = slice(hd * HEAD_DIM, (hd + 1) * HEAD_DIM)
        s = _dot_nt(q[:, cols], kv_ref[:, cols])
        mx = jnp.max(s, axis=-1, keepdims=True)
        p = jnp.exp2(s - mx)
        l = jnp.sum(p, axis=-1, keepdims=True)
        p = p / l
        outs.append(_dot(p.astype(BF16), kv_ref[:, kv_cols + hd * HEAD_DIM:kv_cols + (hd + 1) * HEAD_DIM]))
    o = jnp.concatenate(outs, axis=-1).astype(BF16)
    o_ref[...] = h + _dot(o, wo_ref[...])


def _cross(h, g, wq, kv, wo, layer, batch, seq, n_mem, *, tm):
    m, d = h.shape
    per_b = seq // tm
    kv_cols = 2 * X_HEADS * HEAD_DIM
    return pl.pallas_call(
        _cross_kernel, grid=(m // tm,),
        in_specs=[pl.BlockSpec((tm, d), lambda i: (i, 0)),
                  pl.BlockSpec((1, d), lambda i: (0, 0)),
                  pl.BlockSpec((d, X_HEADS * HEAD_DIM), lambda i: (0, 0)),
                  pl.BlockSpec((n_mem, kv_cols), lambda i: (i // per_b, layer)),
                  pl.BlockSpec((X_HEADS * HEAD_DIM, d), lambda i: (0, 0))],
        out_specs=pl.BlockSpec((tm, d), lambda i: (i, 0)),
        out_shape=jax.ShapeDtypeStruct((m, d), F32),
        compiler_params=_params("parallel"), name="cross_attn",
    )(h, g.reshape(1, d), wq, kv, wo)


def _ffn_kernel(h_ref, halo_ref, g_ref, wg_ref, wv_ref, cwg_ref, cwv_ref, cbg_ref, cbv_ref, wd_ref, gf_ref,
                o_ref, hn_ref, acc_ref, *, per_b, final_norm):
    i = pl.program_id(0)
    f = pl.program_id(1)
    tm = h_ref.shape[0]
    pad = SUBLANE * 2

    @pl.when(f == 0)
    def _():
        g = g_ref[...]
        hn_ref[pad:, :] = _rms_rows(h_ref[...], g).astype(BF16)
        halo = _rms_rows(halo_ref[...], g)
        halo = jnp.where(i % per_b == 0, 0.0, halo)
        hn_ref[:pad, :] = halo.astype(BF16)
        acc_ref[...] = jnp.zeros_like(acc_ref)

    hn = hn_ref[...]

    def conv(u, w, b):
        return (b + w[0:1, :] * u[pad - 2:pad - 2 + tm] + w[1:2, :] * u[pad - 1:pad - 1 + tm]
                + w[2:3, :] * u[pad:pad + tm])

    gate = conv(_dot(hn, wg_ref[...]), cwg_ref[...], cbg_ref[...])
    val = conv(_dot(hn, wv_ref[...]), cwv_ref[...], cbv_ref[...])
    act = (gate * _sigmoid(gate) * val).astype(BF16)
    acc_ref[...] += _dot(act, wd_ref[...])

    @pl.when(f == pl.num_programs(1) - 1)
    def _():
        if not final_norm:
            o_ref[...] = h_ref[...] + acc_ref[...]
        else:
            slab = SUBLANE * 16
            assert tm % slab == 0

            def finish(r, carry):
                rows = pl.ds(pl.multiple_of(r * slab, slab), slab)
                o_ref[rows, :] = _rms_rows(h_ref[rows, :] + acc_ref[rows, :], gf_ref[...])
                return carry

            lax.fori_loop(0, tm // slab, finish, 0)


def _ffn(h, g, w_up, conv_w, conv_b, w_down, g_final, batch, seq, *, tm, tf):
    m, d = h.shape
    dff = w_down.shape[0]
    assert seq % tm == 0 and dff % tf == 0
    per_b = seq // tm
    nf = dff // tf
    pad = SUBLANE * 2
    halo_blocks = tm // pad
    cw = jnp.zeros((SUBLANE, 2 * dff), F32).at[:CONV_WIDTH].set(conv_w)
    cb = conv_b.reshape(1, 2 * dff)
    return pl.pallas_call(
        functools.partial(_ffn_kernel, per_b=per_b, final_norm=g_final is not None),
        grid=(m // tm, nf),
        in_specs=[pl.BlockSpec((tm, d), lambda i, f: (i, 0)),
                  pl.BlockSpec((pad, d), lambda i, f: (jnp.maximum(i * halo_blocks - 1, 0), 0)),
                  pl.BlockSpec((1, d), lambda i, f: (0, 0)),
                  pl.BlockSpec((d, tf), lambda i, f: (0, f)),
                  pl.BlockSpec((d, tf), lambda i, f: (0, nf + f)),
                  pl.BlockSpec((SUBLANE, tf), lambda i, f: (0, f)),
                  pl.BlockSpec((SUBLANE, tf), lambda i, f: (0, nf + f)),
                  pl.BlockSpec((1, tf), lambda i, f: (0, f)),
                  pl.BlockSpec((1, tf), lambda i, f: (0, nf + f)),
                  pl.BlockSpec((tf, d), lambda i, f: (f, 0)),
                  pl.BlockSpec((1, d), lambda i, f: (0, 0))],
        out_specs=pl.BlockSpec((tm, d), lambda i, f: (i, 0), pipeline_mode=pl.Buffered(1)),
        out_shape=jax.ShapeDtypeStruct((m, d), F32),
        scratch_shapes=[pltpu.VMEM((tm + pad, d), BF16), pltpu.VMEM((tm, d), F32)],
        compiler_params=_params("parallel", "arbitrary"), name="conv_ffn",
    )(h, h, g.reshape(1, d), w_up, w_up, cw, cw, cb, cb, w_down,
      (g if g_final is None else g_final).reshape(1, d))


def _row_tile(m, want):
    return want if m % want == 0 else m


def _mixer_ab(h, g, w_in, pe, w1, w2, w_out, tabs, rot, batch, seq):
    m, d = h.shape
    cos, sin = tabs[0], tabs[1]
    w_pad = jnp.zeros((d, AB_BLOCKS * LANE), BF16).at[:, :AB_COLS].set(w_in.astype(BF16))
    proj = _norm_matmul(h, g, w_pad, tm=_row_tile(m, 1024), tn=2048, head_major=True)
    o_moba = _moba(proj, cos, sin, rot, batch, seq)
    kvc = _compress(proj, pe, w1, w2, batch, seq)
    o_cmp, sel = _nsa_cmp(proj, kvc, batch, seq)
    o_nsa = _nsa_sw(proj, sel, o_cmp, cos, sin, rot, batch, seq)
    return _matmul_res([o_moba, o_nsa], w_out.astype(BF16), h, tm=_row_tile(m, 1024), tn=1024)


def _mixer_c(h, g, w_in, gn, w_out, tabs, batch, seq):
    m, d = h.shape
    proj = _norm_matmul(h, g, w_in.astype(BF16), tm=_row_tile(m, 1024), tn=2048)
    y = _retention(proj, tabs[2], tabs[3], gn, batch, seq)
    return _matmul_res([y], w_out.astype(BF16), h, tm=_row_tile(m, 1024), tn=1024)


def kernel(x, mem, positions, norm_mix, norm_cross, norm_ffn, norm_mem, norm_final, w_in_ab, cmp_pe_k, cmp_w1_k,
           cmp_w2_k, cmp_pe_v, cmp_w1_v, cmp_w2_v, w_out_ab, w_in_c, ret_gn, w_out_c, w_q_x, w_kv_x, w_o_x, w_up,
           conv_w, conv_b, w_down):
    batch, seq, d = x.shape
    n_mem = mem.shape[1]
    depth = norm_mix.shape[0]
    m = batch * seq
    tabs = _rope_tables(positions)
    rot = _rot_matrix()

    kv = _norm_matmul(mem.reshape(batch * n_mem, d), norm_mem, w_kv_x.astype(BF16),
                      tm=_row_tile(batch * n_mem, 512), tn=512)

    h = x.reshape(m, d)
    for l in range(depth):
        if l % 2 == 0:
            e = l // 2
            h = _mixer_ab(h, norm_mix[l], w_in_ab[e],
                          jnp.stack([cmp_pe_k[e], cmp_pe_v[e]]), jnp.stack([cmp_w1_k[e], cmp_w1_v[e]]),
                          jnp.stack([cmp_w2_k[e], cmp_w2_v[e]]), w_out_ab[e], tabs, rot, batch, seq)
        else:
            o = l // 2
            h = _mixer_c(h, norm_mix[l], w_in_c[o], ret_gn[o], w_out_c[o], tabs, batch, seq)
        h = _cross(h, norm_cross[l], w_q_x[l].astype(BF16), kv, w_o_x[l].astype(BF16), l, batch, seq, n_mem,
                   tm=_row_tile(seq, 512))
        h = _ffn(h, norm_ffn[l], w_up[l].astype(BF16), conv_w[l], conv_b[l], w_down[l].astype(BF16),
                 norm_final if l == depth - 1 else None, batch, seq, tm=_row_tile(seq, 1024), tf=512)
    return h.reshape(batch, seq, d)
```

```python
import functools

import numpy as np
import jax
import jax.numpy as jnp
from jax import lax
from jax.experimental import pallas as pl
from jax.experimental.pallas import tpu as pltpu

F32 = jnp.float32
BF16 = jnp.bfloat16

D_MODEL = 2048
DEPTH = 4
HEAD_DIM = 128
ROPE_THETA = 500000.0
ROPE_DIM = HEAD_DIM // 4
X_HEADS = 4
MOBA_HEADS = 8
MOBA_BLOCK = 256
MOBA_TOPK = 3
NSA_HEADS = 8
NSA_GROUPS = 2
NSA_REP = NSA_HEADS // NSA_GROUPS
NSA_CMP_LEN = 32
NSA_CMP_STRIDE = 16
NSA_SEL_LEN = 64
NSA_TOPK = 16
NSA_WINDOW = 512
RET_HEADS = 8
RET_DK = 256
RET_DV = 512
RET_THETA = 10000.0
D_FF = 5632
CONV_WIDTH = 3
RMS_EPS = 1e-6
NEG = -1e30
FORCE = 1e9
LOG2E = 1.4426950408889634
Q_SCALE = HEAD_DIM ** -0.5 * LOG2E

LANE = 128
SUBLANE = 8
VMEM_LIMIT = 56 * 2 ** 20

AB_SIZES = (MOBA_HEADS * HEAD_DIM,) * 3 + (NSA_HEADS * HEAD_DIM,) + (NSA_GROUPS * HEAD_DIM,) * 6 + (NSA_HEADS * 3,)
AB_COLS = sum(AB_SIZES)
AB_BLOCKS = 48
BLK_MQ, BLK_MK, BLK_MV, BLK_NQ = 0, 8, 16, 24
BLK_NKC, BLK_NVC, BLK_NKS, BLK_NVS, BLK_NKW, BLK_NVW, BLK_NG = 32, 34, 36, 38, 40, 42, 44

ATT_TQ = 256
ATT_TK = 256
CMP_TQ = 1024
RET_CHUNK = 256
MOBA_GROUP = 4
MOBA_HB = 4
SEL_GROUP = 4

_NT = (((1,), (1,)), ((), ()))
_TN = (((0,), (0,)), ((), ()))


def _params(*sem):
    return pltpu.CompilerParams(dimension_semantics=sem, vmem_limit_bytes=VMEM_LIMIT)


def _dot(a, b):
    return jnp.dot(a, b, preferred_element_type=F32)


def _dot_nt(a, b):
    return lax.dot_general(a, b, _NT, preferred_element_type=F32)


def _sigmoid(x):
    return 1.0 / (1.0 + jnp.exp(-x))


def _rms_rows(x, g):
    ms = jnp.mean(x * x, axis=-1, keepdims=True)
    return x * lax.rsqrt(ms + RMS_EPS) * g


def _tables_kernel(pos_ref, inv_rope_ref, inv_ret_ref, c_ref, s_ref, cr_ref, sr_ref):
    pos = pos_ref[...]
    lane = lax.broadcasted_iota(jnp.int32, pos.shape, 1)
    rot = lane < ROPE_DIM
    ang = pos * inv_rope_ref[...]
    c_ref[...] = jnp.where(rot, jnp.cos(ang), 1.0)
    s_ref[...] = jnp.where(rot, jnp.sin(ang), 0.0)
    ang_r = pos * inv_ret_ref[...]
    cr_ref[...] = jnp.cos(ang_r)
    sr_ref[...] = jnp.sin(ang_r)


def _rope_tables(positions):
    m = positions.size
    posb = jnp.broadcast_to(positions.reshape(m, 1).astype(F32), (m, LANE))
    half = ROPE_DIM // 2
    inv = jnp.float32(ROPE_THETA) ** (-jnp.arange(half, dtype=F32) / half)
    inv_rope = jnp.concatenate([inv, inv, jnp.zeros((LANE - ROPE_DIM,), F32)]).reshape(1, LANE)
    half_r = RET_DK // 2
    inv_ret = (jnp.float32(RET_THETA) ** (-jnp.arange(half_r, dtype=F32) / half_r)).reshape(1, LANE)
    tm = 1024 if m % 1024 == 0 else m
    row = pl.BlockSpec((tm, LANE), lambda i: (i, 0))
    vec = pl.BlockSpec((1, LANE), lambda i: (0, 0))
    sds = jax.ShapeDtypeStruct((m, LANE), F32)
    return pl.pallas_call(
        _tables_kernel, grid=(m // tm,), in_specs=[row, vec, vec], out_specs=[row] * 4,
        out_shape=[sds] * 4, compiler_params=_params("parallel"), name="rope_tables",
    )(posb, inv_rope, inv_ret)


def _rot_matrix():
    half = ROPE_DIM // 2
    p = np.zeros((HEAD_DIM, HEAD_DIM), np.float32)
    for l in range(half):
        p[l + half, l] = -1.0
        p[l, l + half] = 1.0
    return jnp.asarray(p, BF16)


def _rope_partial(x_bf16, c, s, p_ref):
    return x_bf16.astype(F32) * c + _dot(x_bf16, p_ref[...]) * s


def _norm_matmul_kernel(x_ref, g_ref, w_ref, o_ref, xn_ref, *, head_major):
    @pl.when(pl.program_id(1) == 0)
    def _():
        xn_ref[...] = _rms_rows(x_ref[...], g_ref[...]).astype(BF16)

    acc = _dot(xn_ref[...], w_ref[...])
    if head_major:
        for c in range(o_ref.shape[0]):
            o_ref[c] = acc[:, c * LANE:(c + 1) * LANE].astype(o_ref.dtype)
    else:
        o_ref[...] = acc.astype(o_ref.dtype)


def _norm_matmul(x, g, w, *, tm, tn, head_major=False):
    stack = w.shape[0] if w.ndim == 3 else 1
    m, k = x.shape
    per_w = w.shape[-1] // tn
    n = stack * w.shape[-1]
    assert m % tm == 0 and w.shape[-1] % tn == 0 and tn % LANE == 0
    if head_major:
        out_shape = jax.ShapeDtypeStruct((n // LANE, m, LANE), BF16)
        out_spec = pl.BlockSpec((tn // LANE, tm, LANE), lambda i, j: (j, i, 0))
    else:
        out_shape = jax.ShapeDtypeStruct((m, n), BF16)
        out_spec = pl.BlockSpec((tm, tn), lambda i, j: (i, j))
    return pl.pallas_call(
        functools.partial(_norm_matmul_kernel, head_major=head_major),
        grid=(m // tm, n // tn),
        in_specs=[pl.BlockSpec((tm, k), lambda i, j: (i, 0)),
                  pl.BlockSpec((1, k), lambda i, j: (0, 0)),
                  pl.BlockSpec((k, tn), lambda i, j: (0, j)) if w.ndim == 2 else
                  pl.BlockSpec((None, k, tn), lambda i, j: (j // per_w, 0, j % per_w))],
        out_specs=out_spec, out_shape=out_shape,
        scratch_shapes=[pltpu.VMEM((tm, k), BF16)],
        compiler_params=_params("parallel", "arbitrary"), name="norm_matmul",
    )(x, g.reshape(1, k), w)


def _matmul_res_kernel(*refs):
    r_ref, o_ref = refs[-2], refs[-1]
    acc = r_ref[...]
    for x_ref, w_ref in zip(refs[:-2:2], refs[1:-2:2]):
        acc = acc + _dot(x_ref[...], w_ref[...])
    o_ref[...] = acc


def _matmul_res(xs, w, res, *, tm, tn):
    m, n = res.shape
    k = xs[0].shape[1]
    assert m % tm == 0 and n % tn == 0 and all(x.shape[1] == k for x in xs) and w.shape[0] == k * len(xs)
    in_specs, args = [], []
    for idx, x in enumerate(xs):
        in_specs += [pl.BlockSpec((tm, k), lambda i, j: (i, 0)),
                     pl.BlockSpec((k, tn), lambda i, j, idx=idx: (idx, j))]
        args += [x, w]
    return pl.pallas_call(
        _matmul_res_kernel, grid=(m // tm, n // tn),
        in_specs=in_specs + [pl.BlockSpec((tm, tn), lambda i, j: (i, j))],
        out_specs=pl.BlockSpec((tm, tn), lambda i, j: (i, j)),
        out_shape=jax.ShapeDtypeStruct((m, n), F32),
        compiler_params=_params("parallel", "arbitrary"), name="matmul_res",
    )(*args, res)


def _topk_rows(x, n_valid, k, n_rows):
    rows, nq = x.shape
    ridx = lax.broadcasted_iota(jnp.int32, (rows, nq), 0)
    x = jnp.where(ridx < n_valid, x, NEG)
    x = jnp.where(ridx < n_rows, x, -jnp.inf)
    rank = jnp.zeros((rows, nq), F32)
    for i in range(n_rows):
        xi = x[i:i + 1, :]
        rank = rank + jnp.where(ridx > i, jnp.where(xi >= x, 1.0, 0.0), jnp.where(xi > x, 1.0, 0.0))
    marks = jnp.where(rank < float(k), jnp.where(x > NEG / 2, 1.0, 0.0), 0.0)
    if rows < LANE:
        marks = jnp.concatenate([marks, jnp.zeros((LANE - rows, nq), F32)], axis=0)
    return marks.T


def _moba_kernel(q_ref, k_ref, v_ref, c_ref, s_ref, p_ref, o_ref, krot_ref, kmean_ref, vext_ref, *, nb):
    qi = pl.program_id(2)
    blk = MOBA_BLOCK
    nb_pad = -(-nb // SUBLANE) * SUBLANE
    grp = MOBA_GROUP
    heads = range(MOBA_HB)

    @pl.when(qi == 0)
    def _():
        kmean_ref[...] = jnp.zeros_like(kmean_ref)
        for hh in heads:
            for j in range(nb):
                rows = slice(j * blk, (j + 1) * blk)
                kr = _rope_partial(k_ref[hh, rows, :], c_ref[rows, :], s_ref[rows, :], p_ref)
                krot_ref[hh, rows, :] = kr.astype(BF16)
                kmean_ref[hh, j:j + 1, :] = jnp.mean(kr, axis=0, keepdims=True)
            vext_ref[hh, :, :HEAD_DIM] = v_ref[hh]
            vext_ref[hh, :, HEAD_DIM:] = jnp.ones((v_ref.shape[1], HEAD_DIM), BF16)

    r0 = pl.multiple_of(qi * blk, blk)
    cq = c_ref[pl.ds(r0, blk), :]
    sq = s_ref[pl.ds(r0, blk), :]
    lane = lax.broadcasted_iota(jnp.int32, (blk, LANE), 1)
    row = lax.broadcasted_iota(jnp.int32, (blk, blk), 0)
    col = lax.broadcasted_iota(jnp.int32, (blk, blk), 1)

    qbs, sels, state = [], [], []
    for hh in heads:
        q = _rope_partial(q_ref[hh], cq, sq, p_ref)
        qb = (q * Q_SCALE).astype(BF16)
        gate_t = lax.dot_general(kmean_ref[hh], q, _NT, precision=lax.Precision.HIGHEST,
                                 preferred_element_type=F32)[:nb_pad, :]
        sels.append(_topk_rows(gate_t, qi, min(MOBA_TOPK, nb), nb))
        qbs.append(qb)
        s = jnp.where(col <= row, _dot_nt(qb, krot_ref[hh, pl.ds(r0, blk), :]), NEG)
        m = jnp.max(s, axis=-1, keepdims=True)
        p = jnp.exp2((s - m).astype(BF16))
        state += [m, _dot(p, vext_ref[hh, pl.ds(r0, blk), :])]

    def body(jg, carry):
        off = pl.multiple_of(jg * (grp * blk), grp * blk)
        out = []
        for hh in heads:
            m, acc = carry[2 * hh:2 * hh + 2]
            s = _dot_nt(qbs[hh], krot_ref[hh, pl.ds(off, grp * blk), :])
            parts = []
            for t in range(grp):
                chosen = jnp.sum(jnp.where(lane == jg * grp + t, sels[hh], 0.0), axis=-1, keepdims=True)
                parts.append(jnp.where(chosen > 0.5, s[:, t * blk:(t + 1) * blk], NEG))
            s = jnp.concatenate(parts, axis=-1)
            m_new = jnp.maximum(m, jnp.max(s, axis=-1, keepdims=True))
            p = jnp.exp2((s - m_new).astype(BF16))
            out += [m_new, jnp.exp2(m - m_new) * acc + _dot(p, vext_ref[hh, pl.ds(off, grp * blk), :])]
        return tuple(out)

    state = lax.fori_loop(0, (qi + grp - 1) // grp, body, tuple(state))
    for hh in heads:
        acc = state[2 * hh + 1]
        o_ref[:, hh * HEAD_DIM:(hh + 1) * HEAD_DIM] = (acc[:, :HEAD_DIM] / acc[:, HEAD_DIM:HEAD_DIM + 1]).astype(o_ref.dtype)


def _moba(proj, cos, sin, rot, batch, seq):
    nb = seq // MOBA_BLOCK
    hb = MOBA_HB
    assert seq % (MOBA_BLOCK * MOBA_GROUP) == 0 and nb <= LANE
    assert MOBA_HEADS % hb == 0 and BLK_MQ % hb == 0 and BLK_MK % hb == 0 and BLK_MV % hb == 0
    m = batch * seq
    head = lambda off: pl.BlockSpec((hb, seq, HEAD_DIM), lambda b, h, i: (off // hb + h, b, 0))
    table = pl.BlockSpec((seq, LANE), lambda b, h, i: (b, 0))
    return pl.pallas_call(
        functools.partial(_moba_kernel, nb=nb),
        grid=(batch, MOBA_HEADS // hb, nb),
        in_specs=[pl.BlockSpec((hb, MOBA_BLOCK, HEAD_DIM), lambda b, h, i: (BLK_MQ // hb + h, b * nb + i, 0)),
                  head(BLK_MK), head(BLK_MV), table, table,
                  pl.BlockSpec((HEAD_DIM, HEAD_DIM), lambda b, h, i: (0, 0))],
        out_specs=pl.BlockSpec((MOBA_BLOCK, hb * HEAD_DIM), lambda b, h, i: (b * nb + i, h)),
        out_shape=jax.ShapeDtypeStruct((m, MOBA_HEADS * HEAD_DIM), BF16),
        scratch_shapes=[pltpu.VMEM((hb, seq, HEAD_DIM), BF16), pltpu.VMEM((hb, LANE, HEAD_DIM), F32),
                        pltpu.VMEM((hb, seq, 2 * HEAD_DIM), BF16)],
        compiler_params=_params("parallel", "parallel", "arbitrary"), name="moba",
    )(proj, proj, proj, cos, sin, rot)


def _compress_kernel(x_ref, pe_ref, w1_ref, w2_ref, o_ref, xf_ref):
    seq = x_ref.shape[0]
    n_sub = seq // NSA_CMP_STRIDE
    xf_ref[:seq, :] = x_ref[...].astype(F32)
    xf_ref[seq:, :] = jnp.zeros((NSA_CMP_STRIDE, HEAD_DIM), F32)
    hid = _dot(pe_ref[...], w1_ref[...])[0:1, :]
    for r in range(NSA_CMP_LEN):
        rows = xf_ref[pl.ds(r, n_sub, stride=NSA_CMP_STRIDE), :]
        hid = hid + _dot(rows.astype(BF16), w1_ref[r * HEAD_DIM:(r + 1) * HEAD_DIM, :])
    hid = hid * _sigmoid(hid)
    o_ref[...] = _dot(hid.astype(BF16), w2_ref[...]).astype(o_ref.dtype)


def _compress(proj, pe, w1, w2, batch, seq):
    n_sub = seq // NSA_CMP_STRIDE
    flat = NSA_CMP_LEN * HEAD_DIM
    pe_flat = jnp.zeros((2, SUBLANE * 2, flat), BF16).at[:, 0].set(pe.reshape(2, flat).astype(BF16))
    return pl.pallas_call(
        _compress_kernel, grid=(2, batch, NSA_GROUPS),
        in_specs=[pl.BlockSpec((None, seq, HEAD_DIM), lambda t, b, g: (BLK_NKC + 2 * t + g, b, 0)),
                  pl.BlockSpec((None, SUBLANE * 2, flat), lambda t, b, g: (t, 0, 0)),
                  pl.BlockSpec((None, flat, HEAD_DIM), lambda t, b, g: (t, 0, 0)),
                  pl.BlockSpec((None, HEAD_DIM, HEAD_DIM), lambda t, b, g: (t, 0, 0))],
        out_specs=pl.BlockSpec((None, None, None, n_sub, HEAD_DIM), lambda t, b, g: (t, b, g, 0, 0)),
        out_shape=jax.ShapeDtypeStruct((2, batch, NSA_GROUPS, n_sub, HEAD_DIM), BF16),
        scratch_shapes=[pltpu.VMEM((seq + NSA_CMP_STRIDE, HEAD_DIM), F32)],
        compiler_params=_params("parallel", "parallel", "parallel"), name="nsa_compress",
    )(proj, pe_flat, w1.astype(BF16), w2.astype(BF16))


def _nsa_cmp_kernel(q_ref, kc_ref, vc_ref, cov_ref, oc_ref, sel_ref, *, n_sel):
    qi = pl.program_id(2)
    tq = CMP_TQ
    n_cmp = kc_ref.shape[0]
    scale = HEAD_DIM ** -0.5
    q_pos = qi * tq + lax.broadcasted_iota(jnp.int32, (tq, n_cmp), 0)
    n_idx = lax.broadcasted_iota(jnp.int32, (tq, n_cmp), 1)
    ok = n_idx * NSA_CMP_STRIDE + (NSA_CMP_LEN - 1) <= q_pos
    kc = kc_ref[...]
    vc = vc_ref[...]
    p_sum = jnp.zeros((tq, n_cmp), F32)
    for r in range(NSA_REP):
        s = jnp.where(ok, _dot_nt(q_ref[r], kc) * scale, NEG)
        mx = jnp.max(s, axis=-1, keepdims=True)
        e = jnp.where(ok, jnp.exp(s - mx), 0.0)
        l = jnp.sum(e, axis=-1, keepdims=True)
        p = e * jnp.where(l > 0.0, 1.0 / l, 0.0)
        oc_ref[r] = _dot(p.astype(BF16), vc).astype(oc_ref.dtype)
        p_sum = p_sum + p

    n_rows = -(-n_sel // SUBLANE) * SUBLANE
    imp = lax.dot_general(cov_ref[...], p_sum, _NT, precision=lax.Precision.HIGHEST,
                          preferred_element_type=F32)[:n_rows, :]
    ridx = lax.broadcasted_iota(jnp.int32, (n_rows, tq), 0)
    blk = (qi * tq + lax.broadcasted_iota(jnp.int32, (n_rows, tq), 1)) // NSA_SEL_LEN
    forced = (ridx == 0) | (ridx == blk) | (ridx == blk - 1)
    x = jnp.where(forced, FORCE, imp)
    sel_ref[...] = _topk_rows(x, blk[0:1, :] + 1, min(NSA_TOPK, n_sel), n_sel).astype(sel_ref.dtype)


def _cover_matrix(n_cmp_pad, n_cmp, n_sel):
    c_start = np.arange(n_cmp_pad) * NSA_CMP_STRIDE
    s_start = np.arange(LANE) * NSA_SEL_LEN
    cover = (c_start[:, None] < s_start[None, :] + NSA_SEL_LEN) & (c_start[:, None] + NSA_CMP_LEN > s_start[None, :])
    cover &= (np.arange(n_cmp_pad) < n_cmp)[:, None] & (np.arange(LANE) < n_sel)[None, :]
    return jnp.asarray(cover.T, F32)


def _nsa_cmp(proj, kvc, batch, seq):
    nq = seq // CMP_TQ
    n_sub = seq // NSA_CMP_STRIDE
    n_sel = seq // NSA_SEL_LEN
    assert n_sel <= LANE
    m = batch * seq
    cover = _cover_matrix(n_sub, n_sub - NSA_CMP_LEN // NSA_CMP_STRIDE + 1, n_sel)
    cmp_spec = lambda t: pl.BlockSpec((None, None, None, n_sub, HEAD_DIM), lambda b, g, i: (t, b, g, 0, 0))
    return pl.pallas_call(
        functools.partial(_nsa_cmp_kernel, n_sel=n_sel),
        grid=(batch, NSA_GROUPS, nq),
        in_specs=[pl.BlockSpec((NSA_REP, CMP_TQ, HEAD_DIM), lambda b, g, i: (BLK_NQ // NSA_REP + g, b * nq + i, 0)),
                  cmp_spec(0), cmp_spec(1),
                  pl.BlockSpec((LANE, n_sub), lambda b, g, i: (0, 0))],
        out_specs=[pl.BlockSpec((NSA_REP, CMP_TQ, HEAD_DIM), lambda b, g, i: (g, b * nq + i, 0)),
                   pl.BlockSpec((CMP_TQ, LANE), lambda b, g, i: (b * nq + i, g))],
        out_shape=[jax.ShapeDtypeStruct((NSA_HEADS, m, HEAD_DIM), BF16),
                   jax.ShapeDtypeStruct((m, NSA_GROUPS * LANE), BF16)],
        compiler_params=_params("parallel", "parallel", "parallel"), name="nsa_cmp",
    )(proj, kvc, kvc, cover)


def _nsa_sw_kernel(q_ref, ks_ref, vs_ref, kw_ref, vw_ref, sel_ref, oc_ref, gl_ref, c_ref, s_ref, p_ref,
                   o_ref, ksr_ref, kwr_ref, vwp_ref, vse_ref, qr_ref, m_ref, acc_ref, ow_ref, *, seq):
    grp = pl.program_id(1)
    qi = pl.program_id(2)
    tq, tk = ATT_TQ, ATT_TK
    rep = NSA_REP
    win = NSA_WINDOW
    gk = SEL_GROUP * tk

    @pl.when(qi == 0)
    def _():
        kwr_ref[:win, :] = jnp.zeros((win, HEAD_DIM), BF16)
        vwp_ref[:win, :] = jnp.zeros((win, 2 * HEAD_DIM), BF16)
        vwp_ref[win:, HEAD_DIM:] = jnp.ones((seq, HEAD_DIM), BF16)
        vse_ref[:, :HEAD_DIM] = vs_ref[...]
        vse_ref[:, HEAD_DIM:] = jnp.ones((seq, HEAD_DIM), BF16)
        for j in range(seq // tk):
            rows = slice(j * tk, (j + 1) * tk)
            shifted = slice(win + j * tk, win + (j + 1) * tk)
            c, s = c_ref[rows, :], s_ref[rows, :]
            ksr_ref[rows, :] = _rope_partial(ks_ref[rows, :], c, s, p_ref).astype(BF16)
            kwr_ref[shifted, :] = _rope_partial(kw_ref[rows, :], c, s, p_ref).astype(BF16)
            vwp_ref[shifted, :HEAD_DIM] = vw_ref[rows, :]

    r0 = pl.multiple_of(qi * tq, tq)
    cq = c_ref[pl.ds(r0, tq), :]
    sq = s_ref[pl.ds(r0, tq), :]
    for r in range(rep):
        qr_ref[r] = (_rope_partial(q_ref[r], cq, sq, p_ref) * Q_SCALE).astype(BF16)

    q_all = qr_ref[...].reshape(rep * tq, HEAD_DIM)

    row_w = lax.broadcasted_iota(jnp.int32, (tq, win + tq), 0)
    col_w = lax.broadcasted_iota(jnp.int32, (tq, win + tq), 1)
    in_win = (col_w > row_w) & (col_w <= row_w + win) & (col_w >= win - r0)
    s = _dot_nt(q_all, kwr_ref[pl.ds(r0, win + tq), :]).reshape(rep, tq, win + tq)
    s = jnp.where(in_win[None], s, NEG)
    p = jnp.exp2((s - jnp.max(s, axis=-1, keepdims=True)).astype(BF16))
    o_w = _dot(p.reshape(rep * tq, win + tq), vwp_ref[pl.ds(r0, win + tq), :]).reshape(rep, tq, 2 * HEAD_DIM)
    ow_ref[...] = o_w[..., :HEAD_DIM] / o_w[..., HEAD_DIM:HEAD_DIM + 1]

    sel = sel_ref[...]
    row = lax.broadcasted_iota(jnp.int32, (tq, gk), 0)
    col = lax.broadcasted_iota(jnp.int32, (tq, gk), 1)

    def chosen_keys(jg):
        blk_row = lax.broadcasted_iota(jnp.int32, (LANE, gk), 0)
        blk_col = jg * (gk // NSA_SEL_LEN) + lax.broadcasted_iota(jnp.int32, (LANE, gk), 1) // NSA_SEL_LEN
        return _dot(sel, jnp.where(blk_row == blk_col, 1.0, 0.0).astype(BF16))

    def attend(off, valid, first):
        s = _dot_nt(q_all, ksr_ref[pl.ds(off, gk), :]).reshape(rep, tq, gk)
        s = jnp.where(valid[None], s, NEG)
        m_new = jnp.max(s, axis=-1, keepdims=True)
        if not first:
            m_old = m_ref[...]
            m_new = jnp.maximum(m_old, m_new)
        p = jnp.exp2((s - m_new).astype(BF16))
        v_grp = vse_ref[pl.ds(off, gk), :]
        for r in range(rep):
            pv = _dot(p[r], v_grp)
            if first:
                acc_ref[r] = pv
            else:
                acc_ref[r] = jnp.exp2(m_old[r] - m_new[r]) * acc_ref[r] + pv
        m_ref[...] = m_new

    gd = qi // SEL_GROUP
    off_d = pl.multiple_of(gd * gk, gk)
    causal = col + off_d <= row + r0
    attend(off_d, jnp.where(causal, chosen_keys(gd), 0.0) > 0.5, True)

    def sel_body(jg, carry):
        attend(pl.multiple_of(jg * gk, gk), chosen_keys(jg) > 0.5, False)
        return carry

    lax.fori_loop(0, gd, sel_body, 0)

    gates = _sigmoid(gl_ref[...].astype(F32))
    lane = lax.broadcasted_iota(jnp.int32, (tq, LANE), 1)

    def gate_col(c):
        return jnp.sum(jnp.where(lane == c, gates, 0.0), axis=-1, keepdims=True)

    for r in range(rep):
        base = (grp * rep + r) * 3
        o_s = acc_ref[r, :, :HEAD_DIM] / acc_ref[r, :, HEAD_DIM:HEAD_DIM + 1]
        o = gate_col(base) * oc_ref[r].astype(F32) + gate_col(base + 1) * o_s + gate_col(base + 2) * ow_ref[r]
        o_ref[:, r * HEAD_DIM:(r + 1) * HEAD_DIM] = o.astype(o_ref.dtype)


def _nsa_sw(proj, sel, o_cmp, cos, sin, rot, batch, seq):
    nq = seq // ATT_TQ
    assert seq % (SEL_GROUP * ATT_TK) == 0 and NSA_WINDOW % ATT_TK == 0 and ATT_TQ == ATT_TK
    m = batch * seq
    rep = NSA_REP
    head = lambda off: pl.BlockSpec((None, seq, HEAD_DIM), lambda b, g, i: (off + g, b, 0))
    table = pl.BlockSpec((seq, LANE), lambda b, g, i: (b, 0))
    q_like = lambda off: pl.BlockSpec((rep, ATT_TQ, HEAD_DIM), lambda b, g, i: (off + g, b * nq + i, 0))
    return pl.pallas_call(
        functools.partial(_nsa_sw_kernel, seq=seq),
        grid=(batch, NSA_GROUPS, nq),
        in_specs=[q_like(BLK_NQ // rep), head(BLK_NKS), head(BLK_NVS), head(BLK_NKW), head(BLK_NVW),
                  pl.BlockSpec((ATT_TQ, LANE), lambda b, g, i: (b * nq + i, g)),
                  q_like(0),
                  pl.BlockSpec((None, ATT_TQ, LANE), lambda b, g, i: (BLK_NG, b * nq + i, 0)),
                  table, table,
                  pl.BlockSpec((HEAD_DIM, HEAD_DIM), lambda b, g, i: (0, 0))],
        out_specs=pl.BlockSpec((ATT_TQ, rep * HEAD_DIM), lambda b, g, i: (b * nq + i, g)),
        out_shape=jax.ShapeDtypeStruct((m, NSA_HEADS * HEAD_DIM), BF16),
        scratch_shapes=[pltpu.VMEM((seq, HEAD_DIM), BF16),
                        pltpu.VMEM((seq + NSA_WINDOW, HEAD_DIM), BF16),
                        pltpu.VMEM((seq + NSA_WINDOW, 2 * HEAD_DIM), BF16),
                        pltpu.VMEM((seq, 2 * HEAD_DIM), BF16),
                        pltpu.VMEM((rep, ATT_TQ, HEAD_DIM), BF16),
                        pltpu.VMEM((rep, ATT_TQ, 1), F32),
                        pltpu.VMEM((rep, ATT_TQ, 2 * HEAD_DIM), F32), pltpu.VMEM((rep, ATT_TQ, HEAD_DIM), F32)],
        compiler_params=_params("parallel", "parallel", "arbitrary"), name="nsa_sel_win",
    )(proj, proj, proj, proj, proj, sel, o_cmp, proj, cos, sin, rot)


def _rope_full(x_bf16, c, s):
    half = x_bf16.shape[-1] // 2
    x = x_bf16.astype(F32)
    x1, x2 = x[:, :half], x[:, half:]
    return jnp.concatenate([x1 * c - x2 * s, x1 * s + x2 * c], axis=-1)


def _retention_kernel(q_ref, k_ref, v_ref, g_ref, c_ref, s_ref, lg_ref, gn_ref, o_ref, state_ref, decay_ref):
    ci = pl.program_id(1)
    ch = q_ref.shape[0]

    @pl.when(ci == 0)
    def _():
        state_ref[...] = jnp.zeros_like(state_ref)
        n_row = lax.broadcasted_iota(jnp.int32, (ch, ch), 0)
        n_col = lax.broadcasted_iota(jnp.int32, (ch, ch), 1)
        diff = (n_row - n_col).astype(F32)
        for h in range(RET_HEADS):
            decay_ref[h] = jnp.where(diff >= 0.0, jnp.exp(jnp.maximum(diff, 0.0) * lg_ref[h][:, 0:1]), 0.0)

    c, s = c_ref[...], s_ref[...]
    n_vec = lax.broadcasted_iota(jnp.int32, (ch, 1), 0).astype(F32)

    for h in range(RET_HEADS):
        log_g = lg_ref[h][:, 0:1]
        qk_cols = slice(h * RET_DK, (h + 1) * RET_DK)
        v_cols = slice(h * RET_DV, (h + 1) * RET_DV)
        q = _rope_full(q_ref[:, qk_cols], c, s)
        k = _rope_full(k_ref[:, qk_cols], c, s) * (RET_DK ** -0.5)
        v = v_ref[:, v_cols]

        q_dec = jnp.exp((n_vec + 1.0) * log_g)
        k_dec = jnp.exp((ch - 1.0 - n_vec) * log_g)
        c_dec = jnp.exp(ch * log_g)

        qb = q.astype(BF16)
        state = state_ref[h]
        scores = _dot_nt(qb, k.astype(BF16)) * decay_ref[h]
        y = _dot(scores.astype(BF16), v) + _dot(qb, state.astype(BF16)) * q_dec
        state_ref[h] = state * c_dec + lax.dot_general((k * k_dec).astype(BF16), v, _TN,
                                                       preferred_element_type=F32)

        mu = jnp.mean(y, axis=-1, keepdims=True)
        yc = y - mu
        var = jnp.mean(yc * yc, axis=-1, keepdims=True)
        yn = yc * lax.rsqrt(var + RMS_EPS) * gn_ref[h]
        gate = g_ref[:, v_cols].astype(F32)
        o_ref[:, v_cols] = (gate * _sigmoid(gate) * yn).astype(o_ref.dtype)


def _retention(proj, cos_r, sin_r, gn, batch, seq):
    ch = RET_CHUNK
    nch = seq // ch
    assert seq % ch == 0
    m = batch * seq
    log_g = jnp.log(1.0 - jnp.exp2(-5.0 - jnp.arange(RET_HEADS, dtype=F32)))
    log_g = jnp.broadcast_to(log_g[:, None, None], (RET_HEADS, 1, LANE))
    qk_w = RET_HEADS * RET_DK
    v_w = RET_HEADS * RET_DV
    assert v_w == 2 * qk_w
    rows = lambda b, c: b * nch + c
    return pl.pallas_call(
        _retention_kernel, grid=(batch, nch),
        in_specs=[pl.BlockSpec((ch, qk_w), lambda b, c: (rows(b, c), 0)),
                  pl.BlockSpec((ch, qk_w), lambda b, c: (rows(b, c), 1)),
                  pl.BlockSpec((ch, v_w), lambda b, c: (rows(b, c), 1)),
                  pl.BlockSpec((ch, v_w), lambda b, c: (rows(b, c), 2)),
                  pl.BlockSpec((ch, LANE), lambda b, c: (rows(b, c), 0)),
                  pl.BlockSpec((ch, LANE), lambda b, c: (rows(b, c), 0)),
                  pl.BlockSpec((RET_HEADS, 1, LANE), lambda b, c: (0, 0, 0)),
                  pl.BlockSpec((RET_HEADS, 1, RET_DV), lambda b, c: (0, 0, 0))],
        out_specs=pl.BlockSpec((ch, v_w), lambda b, c: (rows(b, c), 0)),
        out_shape=jax.ShapeDtypeStruct((m, v_w), BF16),
        scratch_shapes=[pltpu.VMEM((RET_HEADS, RET_DK, RET_DV), F32), pltpu.VMEM((RET_HEADS, ch, ch), F32)],
        compiler_params=_params("parallel", "arbitrary"), name="retention",
    )(proj, proj, proj, proj, cos_r, sin_r, log_g, gn.reshape(RET_HEADS, 1, RET_DV).astype(F32))


def _cross_kernel(h_ref, g_ref, wq_ref, kv_ref, wo_ref, o_ref):
    h = h_ref[...]
    hn = _rms_rows(h, g_ref[...]).astype(BF16)
    q = (_dot(hn, wq_ref[...]) * Q_SCALE).astype(BF16)
    kv_cols = X_HEADS * HEAD_DIM
    outs = []
    for hd in range(X_HEADS):
        cols = slice(hd * HEAD_DIM, (hd + 1) * HEAD_DIM)
        s = _dot_nt(q[:, cols], kv_ref[:, cols])
        mx = jnp.max(s, axis=-1, keepdims=True)
        p = jnp.exp2(s - mx)
        l = jnp.sum(p, axis=-1, keepdims=True)
        p = p / l
        outs.append(_dot(p.astype(BF16), kv_ref[:, kv_cols + hd * HEAD_DIM:kv_cols + (hd + 1) * HEAD_DIM]))
    o = jnp.concatenate(outs, axis=-1).astype(BF16)
    o_ref[...] = h + _dot(o, wo_ref[...])


def _cross(h, g, wq, kv, wo, layer, batch, seq, n_mem, *, tm):
    m, d = h.shape
    per_b = seq // tm
    kv_cols = 2 * X_HEADS * HEAD_DIM
    return pl.pallas_call(
        _cross_kernel, grid=(m // tm,),
        in_specs=[pl.BlockSpec((tm, d), lambda i: (i, 0)),
                  pl.BlockSpec((1, d), lambda i: (0, 0)),
                  pl.BlockSpec((d, X_HEADS * HEAD_DIM), lambda i: (0, 0)),
                  pl.BlockSpec((n_mem, kv_cols), lambda i: (i // per_b, layer)),
                  pl.BlockSpec((X_HEADS * HEAD_DIM, d), lambda i: (0, 0))],
        out_specs=pl.BlockSpec((tm, d), lambda i: (i, 0)),
        out_shape=jax.ShapeDtypeStruct((m, d), F32),
        compiler_params=_params("parallel"), name="cross_attn",
    )(h, g.reshape(1, d), wq, kv, wo)


def _ffn_kernel(h_ref, halo_ref, g_ref, wg_ref, wv_ref, cwg_ref, cwv_ref, cbg_ref, cbv_ref, wd_ref, gf_ref,
                o_ref, hn_ref, acc_ref, *, per_b, final_norm):
    i = pl.program_id(0)
    f = pl.program_id(1)
    tm = h_ref.shape[0]
    pad = SUBLANE * 2

    @pl.when(f == 0)
    def _():
        g = g_ref[...]
        hn_ref[pad:, :] = _rms_rows(h_ref[...], g).astype(BF16)
        halo = _rms_rows(halo_ref[...], g)
        halo = jnp.where(i % per_b == 0, 0.0, halo)
        hn_ref[:pad, :] = halo.astype(BF16)
        acc_ref[...] = jnp.zeros_like(acc_ref)

    hn = hn_ref[...]

    def conv(u, w, b):
        return (b + w[0:1, :] * u[pad - 2:pad - 2 + tm] + w[1:2, :] * u[pad - 1:pad - 1 + tm]
                + w[2:3, :] * u[pad:pad + tm])

    gate = conv(_dot(hn, wg_ref[...]), cwg_ref[...], cbg_ref[...])
    val = conv(_dot(hn, wv_ref[...]), cwv_ref[...], cbv_ref[...])
    act = (gate * _sigmoid(gate) * val).astype(BF16)
    acc_ref[...] += _dot(act, wd_ref[...])

    @pl.when(f == pl.num_programs(1) - 1)
    def _():
        if not final_norm:
            o_ref[...] = h_ref[...] + acc_ref[...]
        else:
            slab = SUBLANE * 16
            assert tm % slab == 0

            def finish(r, carry):
                rows = pl.ds(pl.multiple_of(r * slab, slab), slab)
                o_ref[rows, :] = _rms_rows(h_ref[rows, :] + acc_ref[rows, :], gf_ref[...])
                return carry

            lax.fori_loop(0, tm // slab, finish, 0)


def _ffn(h, g, w_up, conv_w, conv_b, w_down, g_final, batch, seq, *, tm, tf):
    m, d = h.shape
    dff = w_down.shape[0]
    assert seq % tm == 0 and dff % tf == 0
    per_b = seq // tm
    nf = dff // tf
    pad = SUBLANE * 2
    halo_blocks = tm // pad
    cw = jnp.zeros((SUBLANE, 2 * dff), F32).at[:CONV_WIDTH].set(conv_w)
    cb = conv_b.reshape(1, 2 * dff)
    return pl.pallas_call(
        functools.partial(_ffn_kernel, per_b=per_b, final_norm=g_final is not None),
        grid=(m // tm, nf),
        in_specs=[pl.BlockSpec((tm, d), lambda i, f: (i, 0)),
                  pl.BlockSpec((pad, d), lambda i, f: (jnp.maximum(i * halo_blocks - 1, 0), 0)),
                  pl.BlockSpec((1, d), lambda i, f: (0, 0)),
                  pl.BlockSpec((d, tf), lambda i, f: (0, f)),
                  pl.BlockSpec((d, tf), lambda i, f: (0, nf + f)),
                  pl.BlockSpec((SUBLANE, tf), lambda i, f: (0, f)),
                  pl.BlockSpec((SUBLANE, tf), lambda i, f: (0, nf + f)),
                  pl.BlockSpec((1, tf), lambda i, f: (0, f)),
                  pl.BlockSpec((1, tf), lambda i, f: (0, nf + f)),
                  pl.BlockSpec((tf, d), lambda i, f: (f, 0)),
                  pl.BlockSpec((1, d), lambda i, f: (0, 0))],
        out_specs=pl.BlockSpec((tm, d), lambda i, f: (i, 0), pipeline_mode=pl.Buffered(1)),
        out_shape=jax.ShapeDtypeStruct((m, d), F32),
        scratch_shapes=[pltpu.VMEM((tm + pad, d), BF16), pltpu.VMEM((tm, d), F32)],
        compiler_params=_params("parallel", "arbitrary"), name="conv_ffn",
    )(h, h, g.reshape(1, d), w_up, w_up, cw, cw, cb, cb, w_down,
      (g if g_final is None else g_final).reshape(1, d))


def _row_tile(m, want):
    return want if m % want == 0 else m


def _mixer_ab(h, g, w_in, pe, w1, w2, w_out, tabs, rot, batch, seq):
    m, d = h.shape
    cos, sin = tabs[0], tabs[1]
    w_pad = jnp.zeros((d, AB_BLOCKS * LANE), BF16).at[:, :AB_COLS].set(w_in.astype(BF16))
    proj = _norm_matmul(h, g, w_pad, tm=_row_tile(m, 1024), tn=2048, head_major=True)
    o_moba = _moba(proj, cos, sin, rot, batch, seq)
    kvc = _compress(proj, pe, w1, w2, batch, seq)
    o_cmp, sel = _nsa_cmp(proj, kvc, batch, seq)
    o_nsa = _nsa_sw(proj, sel, o_cmp, cos, sin, rot, batch, seq)
    return _matmul_res([o_moba, o_nsa], w_out.astype(BF16), h, tm=_row_tile(m, 1024), tn=1024)


def _mixer_c(h, g, w_in, gn, w_out, tabs, batch, seq):
    m, d = h.shape
    proj = _norm_matmul(h, g, w_in.astype(BF16), tm=_row_tile(m, 1024), tn=2048)
    y = _retention(proj, tabs[2], tabs[3], gn, batch, seq)
    return _matmul_res([y], w_out.astype(BF16), h, tm=_row_tile(m, 1024), tn=1024)


def kernel(x, mem, positions, norm_mix, norm_cross, norm_ffn, norm_mem, norm_final, w_in_ab, cmp_pe_k, cmp_w1_k,
           cmp_w2_k, cmp_pe_v, cmp_w1_v, cmp_w2_v, w_out_ab, w_in_c, ret_gn, w_out_c, w_q_x, w_kv_x, w_o_x, w_up,
           conv_w, conv_b, w_down):
    batch, seq, d = x.shape
    n_mem = mem.shape[1]
    depth = norm_mix.shape[0]
    m = batch * seq
    tabs = _rope_tables(positions)
    rot = _rot_matrix()

    kv = _norm_matmul(mem.reshape(batch * n_mem, d), norm_mem, w_kv_x.astype(BF16),
                      tm=_row_tile(batch * n_mem, 512), tn=512)

    h = x.reshape(m, d)
    for l in range(depth):
        if l % 2 == 0:
            e = l // 2
            h = _mixer_ab(h, norm_mix[l], w_in_ab[e],
                          jnp.stack([cmp_pe_k[e], cmp_pe_v[e]]), jnp.stack([cmp_w1_k[e], cmp_w1_v[e]]),
                          jnp.stack([cmp_w2_k[e], cmp_w2_v[e]]), w_out_ab[e], tabs, rot, batch, seq)
        else:
            o = l // 2
            h = _mixer_c(h, norm_mix[l], w_in_c[o], ret_gn[o], w_out_c[o], tabs, batch, seq)
        h = _cross(h, norm_cross[l], w_q_x[l].astype(BF16), kv, w_o_x[l].astype(BF16), l, batch, seq, n_mem,
                   tm=_row_tile(seq, 512))
        h = _ffn(h, norm_ffn[l], w_up[l].astype(BF16), conv_w[l], conv_b[l], w_down[l].astype(BF16),
                 norm_final if l == depth - 1 else None, batch, seq, tm=_row_tile(seq, 1024), tf=512)
    return h.reshape(batch, seq, d)
```

```python
import functools

import numpy as np
import jax
import jax.numpy as jnp
from jax import lax
from jax.experimental import pallas as pl
from jax.experimental.pallas import tpu as pltpu

F32 = jnp.float32
BF16 = jnp.bfloat16

D_MODEL = 2048
DEPTH = 4
HEAD_DIM = 128
ROPE_THETA = 500000.0
ROPE_DIM = HEAD_DIM // 4
X_HEADS = 4
MOBA_HEADS = 8
MOBA_BLOCK = 256
MOBA_TOPK = 3
NSA_HEADS = 8
NSA_GROUPS = 2
NSA_REP = NSA_HEADS // NSA_GROUPS
NSA_CMP_LEN = 32
NSA_CMP_STRIDE = 16
NSA_SEL_LEN = 64
NSA_TOPK = 16
NSA_WINDOW = 512
RET_HEADS = 8
RET_DK = 256
RET_DV = 512
RET_THETA = 10000.0
D_FF = 5632
CONV_WIDTH = 3
RMS_EPS = 1e-6
NEG = -1e30
FORCE = 1e9
LOG2E = 1.4426950408889634
Q_SCALE = HEAD_DIM ** -0.5 * LOG2E

LANE = 128
SUBLANE = 8
VMEM_LIMIT = 56 * 2 ** 20

AB_SIZES = (MOBA_HEADS * HEAD_DIM,) * 3 + (NSA_HEADS * HEAD_DIM,) + (NSA_GROUPS * HEAD_DIM,) * 6 + (NSA_HEADS * 3,)
AB_COLS = sum(AB_SIZES)
AB_BLOCKS = 48
BLK_MQ, BLK_MK, BLK_MV, BLK_NQ = 0, 8, 16, 24
BLK_NKC, BLK_NVC, BLK_NKS, BLK_NVS, BLK_NKW, BLK_NVW, BLK_NG = 32, 34, 36, 38, 40, 42, 44

ATT_TK = 256
CMP_TQ = 1024
NSA_TQ = 256
RET_CHUNK = 256
MOBA_GROUP = 4
MOBA_HB = 4
SEL_GROUP = 4

_NT = (((1,), (1,)), ((), ()))
_TN = (((0,), (0,)), ((), ()))


def _params(*sem):
    return pltpu.CompilerParams(dimension_semantics=sem, vmem_limit_bytes=VMEM_LIMIT)


def _dot(a, b):
    return jnp.dot(a, b, preferred_element_type=F32)


def _dot_nt(a, b):
    return lax.dot_general(a, b, _NT, preferred_element_type=F32)


def _sigmoid(x):
    return 1.0 / (1.0 + jnp.exp(-x))


def _rms_rows(x, g):
    ms = jnp.mean(x * x, axis=-1, keepdims=True)
    return x * lax.rsqrt(ms + RMS_EPS) * g


def _tables_kernel(pos_ref, inv_rope_ref, inv_ret_ref, c_ref, s_ref, cr_ref, sr_ref):
    pos = pos_ref[...]
    lane = lax.broadcasted_iota(jnp.int32, pos.shape, 1)
    rot = lane < ROPE_DIM
    ang = pos * inv_rope_ref[...]
    c_ref[...] = jnp.where(rot, jnp.cos(ang), 1.0)
    s_ref[...] = jnp.where(rot, jnp.sin(ang), 0.0)
    ang_r = pos * inv_ret_ref[...]
    cr_ref[...] = jnp.cos(ang_r)
    sr_ref[...] = jnp.sin(ang_r)


def _rope_tables(positions):
    m = positions.size
    posb = jnp.broadcast_to(positions.reshape(m, 1).astype(F32), (m, LANE))
    half = ROPE_DIM // 2
    inv = jnp.float32(ROPE_THETA) ** (-jnp.arange(half, dtype=F32) / half)
    inv_rope = jnp.concatenate([inv, inv, jnp.zeros((LANE - ROPE_DIM,), F32)]).reshape(1, LANE)
    half_r = RET_DK // 2
    inv_ret = (jnp.float32(RET_THETA) ** (-jnp.arange(half_r, dtype=F32) / half_r)).reshape(1, LANE)
    tm = 1024 if m % 1024 == 0 else m
    row = pl.BlockSpec((tm, LANE), lambda i: (i, 0))
    vec = pl.BlockSpec((1, LANE), lambda i: (0, 0))
    sds = jax.ShapeDtypeStruct((m, LANE), F32)
    return pl.pallas_call(
        _tables_kernel, grid=(m // tm,), in_specs=[row, vec, vec], out_specs=[row] * 4,
        out_shape=[sds] * 4, compiler_params=_params("parallel"), name="rope_tables",
    )(posb, inv_rope, inv_ret)


def _rot_matrix():
    half = ROPE_DIM // 2
    p = np.zeros((HEAD_DIM, HEAD_DIM), np.float32)
    for l in range(half):
        p[l + half, l] = -1.0
        p[l, l + half] = 1.0
    return jnp.asarray(p, BF16)


def _rope_partial(x_bf16, c, s, p_ref):
    return x_bf16.astype(F32) * c + _dot(x_bf16, p_ref[...]) * s


def _norm_matmul_kernel(x_ref, g_ref, w_ref, o_ref, xn_ref, *, head_major):
    @pl.when(pl.program_id(1) == 0)
    def _():
        xn_ref[...] = _rms_rows(x_ref[...], g_ref[...]).astype(BF16)

    acc = _dot(xn_ref[...], w_ref[...])
    if head_major:
        for c in range(o_ref.shape[0]):
            o_ref[c] = acc[:, c * LANE:(c + 1) * LANE].astype(o_ref.dtype)
    else:
        o_ref[...] = acc.astype(o_ref.dtype)


def _norm_matmul(x, g, w, *, tm, tn, head_major=False):
    stack = w.shape[0] if w.ndim == 3 else 1
    m, k = x.shape
    per_w = w.shape[-1] // tn
    n = stack * w.shape[-1]
    assert m % tm == 0 and w.shape[-1] % tn == 0 and tn % LANE == 0
    if head_major:
        out_shape = jax.ShapeDtypeStruct((n // LANE, m, LANE), BF16)
        out_spec = pl.BlockSpec((tn // LANE, tm, LANE), lambda i, j: (j, i, 0))
    else:
        out_shape = jax.ShapeDtypeStruct((m, n), BF16)
        out_spec = pl.BlockSpec((tm, tn), lambda i, j: (i, j))
    return pl.pallas_call(
        functools.partial(_norm_matmul_kernel, head_major=head_major),
        grid=(m // tm, n // tn),
        in_specs=[pl.BlockSpec((tm, k), lambda i, j: (i, 0)),
                  pl.BlockSpec((1, k), lambda i, j: (0, 0)),
                  pl.BlockSpec((k, tn), lambda i, j: (0, j)) if w.ndim == 2 else
                  pl.BlockSpec((None, k, tn), lambda i, j: (j // per_w, 0, j % per_w))],
        out_specs=out_spec, out_shape=out_shape,
        scratch_shapes=[pltpu.VMEM((tm, k), BF16)],
        compiler_params=_params("parallel", "arbitrary"), name="norm_matmul",
    )(x, g.reshape(1, k), w)


def _matmul_res_kernel(*refs):
    r_ref, o_ref = refs[-2], refs[-1]
    acc = r_ref[...]
    for x_ref, w_ref in zip(refs[:-2:2], refs[1:-2:2]):
        acc = acc + _dot(x_ref[...], w_ref[...])
    o_ref[...] = acc


def _matmul_res(xs, w, res, *, tm, tn):
    m, n = res.shape
    k = xs[0].shape[1]
    assert m % tm == 0 and n % tn == 0 and all(x.shape[1] == k for x in xs) and w.shape[0] == k * len(xs)
    in_specs, args = [], []
    for idx, x in enumerate(xs):
        in_specs += [pl.BlockSpec((tm, k), lambda i, j: (i, 0)),
                     pl.BlockSpec((k, tn), lambda i, j, idx=idx: (idx, j))]
        args += [x, w]
    return pl.pallas_call(
        _matmul_res_kernel, grid=(m // tm, n // tn),
        in_specs=in_specs + [pl.BlockSpec((tm, tn), lambda i, j: (i, j))],
        out_specs=pl.BlockSpec((tm, tn), lambda i, j: (i, j)),
        out_shape=jax.ShapeDtypeStruct((m, n), F32),
        compiler_params=_params("parallel", "arbitrary"), name="matmul_res",
    )(*args, res)


def _topk_rows(x, n_valid, k, n_rows):
    rows, nq = x.shape
    ridx = lax.broadcasted_iota(jnp.int32, (rows, nq), 0)
    x = jnp.where(ridx < n_valid, x, NEG)
    x = jnp.where(ridx < n_rows, x, -jnp.inf)
    rank = jnp.zeros((rows, nq), F32)
    for i in range(n_rows):
        xi = x[i:i + 1, :]
        rank = rank + jnp.where(ridx > i, jnp.where(xi >= x, 1.0, 0.0), jnp.where(xi > x, 1.0, 0.0))
    marks = jnp.where(rank < float(k), jnp.where(x > NEG / 2, 1.0, 0.0), 0.0)
    if rows < LANE:
        marks = jnp.concatenate([marks, jnp.zeros((LANE - rows, nq), F32)], axis=0)
    return marks.T


def _moba_kernel(q_ref, k_ref, v_ref, c_ref, s_ref, p_ref, o_ref, krot_ref, kmean_ref, vext_ref, *, nb):
    qi = pl.program_id(2)
    blk = MOBA_BLOCK
    nb_pad = -(-nb // SUBLANE) * SUBLANE
    grp = MOBA_GROUP
    heads = range(MOBA_HB)

    @pl.when(qi == 0)
    def _():
        kmean_ref[...] = jnp.zeros_like(kmean_ref)
        for hh in heads:
            for j in range(nb):
                rows = slice(j * blk, (j + 1) * blk)
                kr = _rope_partial(k_ref[hh, rows, :], c_ref[rows, :], s_ref[rows, :], p_ref)
                krot_ref[hh, rows, :] = kr.astype(BF16)
                kmean_ref[hh, j:j + 1, :] = jnp.mean(kr, axis=0, keepdims=True)
            vext_ref[hh, :, :HEAD_DIM] = v_ref[hh]
            vext_ref[hh, :, HEAD_DIM:] = jnp.ones((v_ref.shape[1], HEAD_DIM), BF16)

    r0 = pl.multiple_of(qi * blk, blk)
    cq = c_ref[pl.ds(r0, blk), :]
    sq = s_ref[pl.ds(r0, blk), :]
    lane = lax.broadcasted_iota(jnp.int32, (blk, LANE), 1)
    row = lax.broadcasted_iota(jnp.int32, (blk, blk), 0)
    col = lax.broadcasted_iota(jnp.int32, (blk, blk), 1)

    qbs, sels, state = [], [], []
    for hh in heads:
        q = _rope_partial(q_ref[hh], cq, sq, p_ref)
        qb = (q * Q_SCALE).astype(BF16)
        gate_t = lax.dot_general(kmean_ref[hh], q, _NT, precision=lax.Precision.HIGHEST,
                                 preferred_element_type=F32)[:nb_pad, :]
        sels.append(_topk_rows(gate_t, qi, min(MOBA_TOPK, nb), nb))
        qbs.append(qb)
        s = jnp.where(col <= row, _dot_nt(qb, krot_ref[hh, pl.ds(r0, blk), :]), NEG)
        m = jnp.max(s, axis=-1, keepdims=True)
        p = jnp.exp2((s - m).astype(BF16))
        state += [m, _dot(p, vext_ref[hh, pl.ds(r0, blk), :])]

    def body(jg, carry):
        off = pl.multiple_of(jg * (grp * blk), grp * blk)
        out = []
        for hh in heads:
            m, acc = carry[2 * hh:2 * hh + 2]
            s = _dot_nt(qbs[hh], krot_ref[hh, pl.ds(off, grp * blk), :])
            parts = []
            for t in range(grp):
                chosen = jnp.sum(jnp.where(lane == jg * grp + t, sels[hh], 0.0), axis=-1, keepdims=True)
                parts.append(jnp.where(chosen > 0.5, s[:, t * blk:(t + 1) * blk], NEG))
            s = jnp.concatenate(parts, axis=-1)
            m_new = jnp.maximum(m, jnp.max(s, axis=-1, keepdims=True))
            p = jnp.exp2((s - m_new).astype(BF16))
            out += [m_new, jnp.exp2(m - m_new) * acc + _dot(p, vext_ref[hh, pl.ds(off, grp * blk), :])]
        return tuple(out)

    state = lax.fori_loop(0, (qi + grp - 1) // grp, body, tuple(state))
    for hh in heads:
        acc = state[2 * hh + 1]
        o_ref[:, hh * HEAD_DIM:(hh + 1) * HEAD_DIM] = (acc[:, :HEAD_DIM] / acc[:, HEAD_DIM:HEAD_DIM + 1]).astype(o_ref.dtype)


def _moba(proj, cos, sin, rot, batch, seq):
    nb = seq // MOBA_BLOCK
    hb = MOBA_HB
    assert seq % (MOBA_BLOCK * MOBA_GROUP) == 0 and nb <= LANE
    assert MOBA_HEADS % hb == 0 and BLK_MQ % hb == 0 and BLK_MK % hb == 0 and BLK_MV % hb == 0
    m = batch * seq
    head = lambda off: pl.BlockSpec((hb, seq, HEAD_DIM), lambda b, h, i: (off // hb + h, b, 0))
    table = pl.BlockSpec((seq, LANE), lambda b, h, i: (b, 0))
    return pl.pallas_call(
        functools.partial(_moba_kernel, nb=nb),
        grid=(batch, MOBA_HEADS // hb, nb),
        in_specs=[pl.BlockSpec((hb, MOBA_BLOCK, HEAD_DIM), lambda b, h, i: (BLK_MQ // hb + h, b * nb + i, 0)),
                  head(BLK_MK), head(BLK_MV), table, table,
                  pl.BlockSpec((HEAD_DIM, HEAD_DIM), lambda b, h, i: (0, 0))],
        out_specs=pl.BlockSpec((MOBA_BLOCK, hb * HEAD_DIM), lambda b, h, i: (b * nb + i, h)),
        out_shape=jax.ShapeDtypeStruct((m, MOBA_HEADS * HEAD_DIM), BF16),
        scratch_shapes=[pltpu.VMEM((hb, seq, HEAD_DIM), BF16), pltpu.VMEM((hb, LANE, HEAD_DIM), F32),
                        pltpu.VMEM((hb, seq, 2 * HEAD_DIM), BF16)],
        compiler_params=_params("parallel", "parallel", "arbitrary"), name="moba",
    )(proj, proj, proj, cos, sin, rot)


def _compress_kernel(x_ref, pe_ref, w1_ref, w2_ref, o_ref, xf_ref):
    seq = x_ref.shape[0]
    n_sub = seq // NSA_CMP_STRIDE
    xf_ref[:seq, :] = x_ref[...].astype(F32)
    xf_ref[seq:, :] = jnp.zeros((NSA_CMP_STRIDE, HEAD_DIM), F32)
    hid = _dot(pe_ref[...], w1_ref[...])[0:1, :]
    for r in range(NSA_CMP_LEN):
        rows = xf_ref[pl.ds(r, n_sub, stride=NSA_CMP_STRIDE), :]
        hid = hid + _dot(rows.astype(BF16), w1_ref[r * HEAD_DIM:(r + 1) * HEAD_DIM, :])
    hid = hid * _sigmoid(hid)
    o_ref[...] = _dot(hid.astype(BF16), w2_ref[...]).astype(o_ref.dtype)


def _compress(proj, pe, w1, w2, batch, seq):
    n_sub = seq // NSA_CMP_STRIDE
    flat = NSA_CMP_LEN * HEAD_DIM
    pe_flat = jnp.zeros((2, SUBLANE * 2, flat), BF16).at[:, 0].set(pe.reshape(2, flat).astype(BF16))
    return pl.pallas_call(
        _compress_kernel, grid=(2, batch, NSA_GROUPS),
        in_specs=[pl.BlockSpec((None, seq, HEAD_DIM), lambda t, b, g: (BLK_NKC + 2 * t + g, b, 0)),
                  pl.BlockSpec((None, SUBLANE * 2, flat), lambda t, b, g: (t, 0, 0)),
                  pl.BlockSpec((None, flat, HEAD_DIM), lambda t, b, g: (t, 0, 0)),
                  pl.BlockSpec((None, HEAD_DIM, HEAD_DIM), lambda t, b, g: (t, 0, 0))],
        out_specs=pl.BlockSpec((None, None, None, n_sub, HEAD_DIM), lambda t, b, g: (t, b, g, 0, 0)),
        out_shape=jax.ShapeDtypeStruct((2, batch, NSA_GROUPS, n_sub, HEAD_DIM), BF16),
        scratch_shapes=[pltpu.VMEM((seq + NSA_CMP_STRIDE, HEAD_DIM), F32)],
        compiler_params=_params("parallel", "parallel", "parallel"), name="nsa_compress",
    )(proj, pe_flat, w1.astype(BF16), w2.astype(BF16))


def _nsa_cmp_kernel(q_ref, kc_ref, vc_ref, cov_ref, oc_ref, sel_ref, *, n_sel):
    qi = pl.program_id(2)
    tq = CMP_TQ
    n_cmp = kc_ref.shape[0]
    scale = HEAD_DIM ** -0.5
    q_pos = qi * tq + lax.broadcasted_iota(jnp.int32, (tq, n_cmp), 0)
    n_idx = lax.broadcasted_iota(jnp.int32, (tq, n_cmp), 1)
    ok = n_idx * NSA_CMP_STRIDE + (NSA_CMP_LEN - 1) <= q_pos
    kc = kc_ref[...]
    vc = vc_ref[...]
    p_sum = jnp.zeros((tq, n_cmp), F32)
    for r in range(NSA_REP):
        s = jnp.where(ok, _dot_nt(q_ref[r], kc) * scale, NEG)
        mx = jnp.max(s, axis=-1, keepdims=True)
        e = jnp.where(ok, jnp.exp(s - mx), 0.0)
        l = jnp.sum(e, axis=-1, keepdims=True)
        p = e * jnp.where(l > 0.0, 1.0 / l, 0.0)
        oc_ref[r] = _dot(p.astype(BF16), vc).astype(oc_ref.dtype)
        p_sum = p_sum + p

    n_rows = -(-n_sel // SUBLANE) * SUBLANE
    imp = lax.dot_general(cov_ref[...], p_sum, _NT, precision=lax.Precision.HIGHEST,
                          preferred_element_type=F32)[:n_rows, :]
    ridx = lax.broadcasted_iota(jnp.int32, (n_rows, tq), 0)
    blk = (qi * tq + lax.broadcasted_iota(jnp.int32, (n_rows, tq), 1)) // NSA_SEL_LEN
    forced = (ridx == 0) | (ridx == blk) | (ridx == blk - 1)
    x = jnp.where(forced, FORCE, imp)
    sel_ref[...] = _topk_rows(x, blk[0:1, :] + 1, min(NSA_TOPK, n_sel), n_sel).astype(sel_ref.dtype)


def _cover_matrix(n_cmp_pad, n_cmp, n_sel):
    c_start = np.arange(n_cmp_pad) * NSA_CMP_STRIDE
    s_start = np.arange(LANE) * NSA_SEL_LEN
    cover = (c_start[:, None] < s_start[None, :] + NSA_SEL_LEN) & (c_start[:, None] + NSA_CMP_LEN > s_start[None, :])
    cover &= (np.arange(n_cmp_pad) < n_cmp)[:, None] & (np.arange(LANE) < n_sel)[None, :]
    return jnp.asarray(cover.T, F32)


def _nsa_cmp(proj, kvc, batch, seq):
    nq = seq // CMP_TQ
    n_sub = seq // NSA_CMP_STRIDE
    n_sel = seq // NSA_SEL_LEN
    assert n_sel <= LANE
    m = batch * seq
    cover = _cover_matrix(n_sub, n_sub - NSA_CMP_LEN // NSA_CMP_STRIDE + 1, n_sel)
    cmp_spec = lambda t: pl.BlockSpec((None, None, None, n_sub, HEAD_DIM), lambda b, g, i: (t, b, g, 0, 0))
    return pl.pallas_call(
        functools.partial(_nsa_cmp_kernel, n_sel=n_sel),
        grid=(batch, NSA_GROUPS, nq),
        in_specs=[pl.BlockSpec((NSA_REP, CMP_TQ, HEAD_DIM), lambda b, g, i: (BLK_NQ // NSA_REP + g, b * nq + i, 0)),
                  cmp_spec(0), cmp_spec(1),
                  pl.BlockSpec((LANE, n_sub), lambda b, g, i: (0, 0))],
        out_specs=[pl.BlockSpec((NSA_REP, CMP_TQ, HEAD_DIM), lambda b, g, i: (g, b * nq + i, 0)),
                   pl.BlockSpec((CMP_TQ, LANE), lambda b, g, i: (b * nq + i, g))],
        out_shape=[jax.ShapeDtypeStruct((NSA_HEADS, m, HEAD_DIM), BF16),
                   jax.ShapeDtypeStruct((m, NSA_GROUPS * LANE), BF16)],
        compiler_params=_params("parallel", "parallel", "parallel"), name="nsa_cmp",
    )(proj, kvc, kvc, cover)


def _nsa_sw_kernel(q_ref, ks_ref, vs_ref, kw_ref, vw_ref, sel_ref, oc_ref, gl_ref, c_ref, s_ref, p_ref,
                   o_ref, ksr_ref, kwr_ref, vwp_ref, vse_ref, qr_ref, m_ref, acc_ref, ow_ref, *, seq):
    grp = pl.program_id(1)
    qi = pl.program_id(2)
    tq, tk = NSA_TQ, ATT_TK
    rep = NSA_REP
    win = NSA_WINDOW
    gk = SEL_GROUP * tk

    @pl.when(qi == 0)
    def _():
        kwr_ref[:win, :] = jnp.zeros((win, HEAD_DIM), BF16)
        vwp_ref[:win, :] = jnp.zeros((win, 2 * HEAD_DIM), BF16)
        vwp_ref[win:, HEAD_DIM:] = jnp.ones((seq, HEAD_DIM), BF16)
        vse_ref[:, :HEAD_DIM] = vs_ref[...]
        vse_ref[:, HEAD_DIM:] = jnp.ones((seq, HEAD_DIM), BF16)
        for j in range(seq // tk):
            rows = slice(j * tk, (j + 1) * tk)
            shifted = slice(win + j * tk, win + (j + 1) * tk)
            c, s = c_ref[rows, :], s_ref[rows, :]
            ksr_ref[rows, :] = _rope_partial(ks_ref[rows, :], c, s, p_ref).astype(BF16)
            kwr_ref[shifted, :] = _rope_partial(kw_ref[rows, :], c, s, p_ref).astype(BF16)
            vwp_ref[shifted, :HEAD_DIM] = vw_ref[rows, :]

    r0 = pl.multiple_of(qi * tq, tq)
    cq = c_ref[pl.ds(r0, tq), :]
    sq = s_ref[pl.ds(r0, tq), :]
    for r in range(rep):
        qr_ref[r] = (_rope_partial(q_ref[r], cq, sq, p_ref) * Q_SCALE).astype(BF16)

    q_all = qr_ref[...].reshape(rep * tq, HEAD_DIM)

    row_w = lax.broadcasted_iota(jnp.int32, (tq, win + tq), 0)
    col_w = lax.broadcasted_iota(jnp.int32, (tq, win + tq), 1)
    in_win = (col_w > row_w) & (col_w <= row_w + win) & (col_w >= win - r0)
    s = _dot_nt(q_all, kwr_ref[pl.ds(r0, win + tq), :]).reshape(rep, tq, win + tq)
    s = jnp.where(in_win[None], s, NEG)
    p = jnp.exp2((s - jnp.max(s, axis=-1, keepdims=True)).astype(BF16))
    o_w = _dot(p.reshape(rep * tq, win + tq), vwp_ref[pl.ds(r0, win + tq), :]).reshape(rep, tq, 2 * HEAD_DIM)
    ow_ref[...] = o_w[..., :HEAD_DIM] / o_w[..., HEAD_DIM:HEAD_DIM + 1]

    sel = sel_ref[...]
    row = lax.broadcasted_iota(jnp.int32, (tq, gk), 0)
    col = lax.broadcasted_iota(jnp.int32, (tq, gk), 1)

    def chosen_keys(jg):
        blk_row = lax.broadcasted_iota(jnp.int32, (LANE, gk), 0)
        blk_col = jg * (gk // NSA_SEL_LEN) + lax.broadcasted_iota(jnp.int32, (LANE, gk), 1) // NSA_SEL_LEN
        return _dot(sel, jnp.where(blk_row == blk_col, 1.0, 0.0).astype(BF16))

    def attend(off, valid, first):
        s = _dot_nt(q_all, ksr_ref[pl.ds(off, gk), :]).reshape(rep, tq, gk)
        s = jnp.where(valid[None], s, NEG)
        m_new = jnp.max(s, axis=-1, keepdims=True)
        if not first:
            m_old = m_ref[...]
            m_new = jnp.maximum(m_old, m_new)
        p = jnp.exp2((s - m_new).astype(BF16))
        v_grp = vse_ref[pl.ds(off, gk), :]
        for r in range(rep):
            pv = _dot(p[r], v_grp)
            if first:
                acc_ref[r] = pv
            else:
                acc_ref[r] = jnp.exp2(m_old[r] - m_new[r]) * acc_ref[r] + pv
        m_ref[...] = m_new

    gd = qi // (gk // tq)
    off_d = pl.multiple_of(gd * gk, gk)
    causal = col + off_d <= row + r0
    attend(off_d, jnp.where(causal, chosen_keys(gd), 0.0) > 0.5, True)

    def sel_body(jg, carry):
        attend(pl.multiple_of(jg * gk, gk), chosen_keys(jg) > 0.5, False)
        return carry

    lax.fori_loop(0, gd, sel_body, 0)

    gates = _sigmoid(gl_ref[...].astype(F32))
    lane = lax.broadcasted_iota(jnp.int32, (tq, LANE), 1)

    def gate_col(c):
        return jnp.sum(jnp.where(lane == c, gates, 0.0), axis=-1, keepdims=True)

    for r in range(rep):
        base = (grp * rep + r) * 3
        o_s = acc_ref[r, :, :HEAD_DIM] / acc_ref[r, :, HEAD_DIM:HEAD_DIM + 1]
        o = gate_col(base) * oc_ref[r].astype(F32) + gate_col(base + 1) * o_s + gate_col(base + 2) * ow_ref[r]
        o_ref[:, r * HEAD_DIM:(r + 1) * HEAD_DIM] = o.astype(o_ref.dtype)


def _nsa_sw(proj, sel, o_cmp, cos, sin, rot, batch, seq):
    nq = seq // NSA_TQ
    assert seq % (SEL_GROUP * ATT_TK) == 0 and (SEL_GROUP * ATT_TK) % NSA_TQ == 0 and NSA_TQ % ATT_TK == 0
    m = batch * seq
    rep = NSA_REP
    head = lambda off: pl.BlockSpec((None, seq, HEAD_DIM), lambda b, g, i: (off + g, b, 0))
    table = pl.BlockSpec((seq, LANE), lambda b, g, i: (b, 0))
    q_like = lambda off: pl.BlockSpec((rep, NSA_TQ, HEAD_DIM), lambda b, g, i: (off + g, b * nq + i, 0))
    return pl.pallas_call(
        functools.partial(_nsa_sw_kernel, seq=seq),
        grid=(batch, NSA_GROUPS, nq),
        in_specs=[q_like(BLK_NQ // rep), head(BLK_NKS), head(BLK_NVS), head(BLK_NKW), head(BLK_NVW),
                  pl.BlockSpec((NSA_TQ, LANE), lambda b, g, i: (b * nq + i, g)),
                  q_like(0),
                  pl.BlockSpec((None, NSA_TQ, LANE), lambda b, g, i: (BLK_NG, b * nq + i, 0)),
                  table, table,
                  pl.BlockSpec((HEAD_DIM, HEAD_DIM), lambda b, g, i: (0, 0))],
        out_specs=pl.BlockSpec((NSA_TQ, rep * HEAD_DIM), lambda b, g, i: (b * nq + i, g)),
        out_shape=jax.ShapeDtypeStruct((m, NSA_HEADS * HEAD_DIM), BF16),
        scratch_shapes=[pltpu.VMEM((seq, HEAD_DIM), BF16),
                        pltpu.VMEM((seq + NSA_WINDOW, HEAD_DIM), BF16),
                        pltpu.VMEM((seq + NSA_WINDOW, 2 * HEAD_DIM), BF16),
                        pltpu.VMEM((seq, 2 * HEAD_DIM), BF16),
                        pltpu.VMEM((rep, NSA_TQ, HEAD_DIM), BF16),
                        pltpu.VMEM((rep, NSA_TQ, 1), F32),
                        pltpu.VMEM((rep, NSA_TQ, 2 * HEAD_DIM), F32), pltpu.VMEM((rep, NSA_TQ, HEAD_DIM), F32)],
        compiler_params=_params("parallel", "parallel", "arbitrary"), name="nsa_sel_win",
    )(proj, proj, proj, proj, proj, sel, o_cmp, proj, cos, sin, rot)


def _rope_full(x_bf16, c, s):
    half = x_bf16.shape[-1] // 2
    x = x_bf16.astype(F32)
    x1, x2 = x[:, :half], x[:, half:]
    return jnp.concatenate([x1 * c - x2 * s, x1 * s + x2 * c], axis=-1)


def _retention_kernel(q_ref, k_ref, v_ref, g_ref, c_ref, s_ref, lg_ref, gn_ref, o_ref, state_ref, decay_ref):
    ci = pl.program_id(1)
    ch = q_ref.shape[0]

    @pl.when(ci == 0)
    def _():
        state_ref[...] = jnp.zeros_like(state_ref)
        n_row = lax.broadcasted_iota(jnp.int32, (ch, ch), 0)
        n_col = lax.broadcasted_iota(jnp.int32, (ch, ch), 1)
        diff = (n_row - n_col).astype(F32)
        for h in range(RET_HEADS):
            decay_ref[h] = jnp.where(diff >= 0.0, jnp.exp(jnp.maximum(diff, 0.0) * lg_ref[h][:, 0:1]), 0.0)

    c, s = c_ref[...], s_ref[...]
    n_vec = lax.broadcasted_iota(jnp.int32, (ch, 1), 0).astype(F32)

    for h in range(RET_HEADS):
        log_g = lg_ref[h][:, 0:1]
        qk_cols = slice(h * RET_DK, (h + 1) * RET_DK)
        v_cols = slice(h * RET_DV, (h + 1) * RET_DV)
        q = _rope_full(q_ref[:, qk_cols], c, s)
        k = _rope_full(k_ref[:, qk_cols], c, s) * (RET_DK ** -0.5)
        v = v_ref[:, v_cols]

        q_dec = jnp.exp((n_vec + 1.0) * log_g)
        k_dec = jnp.exp((ch - 1.0 - n_vec) * log_g)
        c_dec = jnp.exp(ch * log_g)

        qb = q.astype(BF16)
        state = state_ref[h]
        scores = _dot_nt(qb, k.astype(BF16)) * decay_ref[h]
        y = _dot(scores.astype(BF16), v) + _dot(qb, state.astype(BF16)) * q_dec
        state_ref[h] = state * c_dec + lax.dot_general((k * k_dec).astype(BF16), v, _TN,
                                                       preferred_element_type=F32)

        mu = jnp.mean(y, axis=-1, keepdims=True)
        yc = y - mu
        var = jnp.mean(yc * yc, axis=-1, keepdims=True)
        yn = yc * lax.rsqrt(var + RMS_EPS) * gn_ref[h]
        gate = g_ref[:, v_cols].astype(F32)
        o_ref[:, v_cols] = (gate * _sigmoid(gate) * yn).astype(o_ref.dtype)


def _retention(proj, cos_r, sin_r, gn, batch, seq):
    ch = RET_CHUNK
    nch = seq // ch
    assert seq % ch == 0
    m = batch * seq
    log_g = jnp.log(1.0 - jnp.exp2(-5.0 - jnp.arange(RET_HEADS, dtype=F32)))
    log_g = jnp.broadcast_to(log_g[:, None, None], (RET_HEADS, 1, LANE))
    qk_w = RET_HEADS * RET_DK
    v_w = RET_HEADS * RET_DV
    assert v_w == 2 * qk_w
    rows = lambda b, c: b * nch + c
    return pl.pallas_call(
        _retention_kernel, grid=(batch, nch),
        in_specs=[pl.BlockSpec((ch, qk_w), lambda b, c: (rows(b, c), 0)),
                  pl.BlockSpec((ch, qk_w), lambda b, c: (rows(b, c), 1)),
                  pl.BlockSpec((ch, v_w), lambda b, c: (rows(b, c), 1)),
                  pl.BlockSpec((ch, v_w), lambda b, c: (rows(b, c), 2)),
                  pl.BlockSpec((ch, LANE), lambda b, c: (rows(b, c), 0)),
                  pl.BlockSpec((ch, LANE), lambda b, c: (rows(b, c), 0)),
                  pl.BlockSpec((RET_HEADS, 1, LANE), lambda b, c: (0, 0, 0)),
                  pl.BlockSpec((RET_HEADS, 1, RET_DV), lambda b, c: (0, 0, 0))],
        out_specs=pl.BlockSpec((ch, v_w), lambda b, c: (rows(b, c), 0)),
        out_shape=jax.ShapeDtypeStruct((m, v_w), BF16),
        scratch_shapes=[pltpu.VMEM((RET_HEADS, RET_DK, RET_DV), F32), pltpu.VMEM((RET_HEADS, ch, ch), F32)],
        compiler_params=_params("parallel", "arbitrary"), name="retention",
    )(proj, proj, proj, proj, cos_r, sin_r, log_g, gn.reshape(RET_HEADS, 1, RET_DV).astype(F32))


def _cross_kernel(h_ref, g_ref, wq_ref, kv_ref, wo_ref, o_ref):
    h = h_ref[...]
    hn = _rms_rows(h, g_ref[...]).astype(BF16)
    q = (_dot(hn, wq_ref[...]) * Q_SCALE).astype(BF16)
    kv_cols = X_HEADS * HEAD_DIM
    outs = []
    for hd in range(X_HEADS):
        cols = slice(hd * HEAD_DIM, (hd + 1) * HEAD_DIM)
        s = _dot_nt(q[:, cols], kv_ref[:, cols])
        mx = jnp.max(s, axis=-1, keepdims=True)
        p = jnp.exp2(s - mx)
        l = jnp.sum(p, axis=-1, keepdims=True)
        p = p / l
        outs.append(_dot(p.astype(BF16), kv_ref[:, kv_cols + hd * HEAD_DIM:kv_cols + (hd + 1) * HEAD_DIM]))
    o = jnp.concatenate(outs, axis=-1).astype(BF16)
    o_ref[...] = h + _dot(o, wo_ref[...])


def _cross(h, g, wq, kv, wo, layer, batch, seq, n_mem, *, tm):
    m, d = h.shape
    per_b = seq // tm
    kv_cols = 2 * X_HEADS * HEAD_DIM
    return pl.pallas_call(
        _cross_kernel, grid=(m // tm,),
        in_specs=[pl.BlockSpec((tm, d), lambda i: (i, 0)),
                  pl.BlockSpec((1, d), lambda i: (0, 0)),
                  pl.BlockSpec((d, X_HEADS * HEAD_DIM), lambda i: (0, 0)),
                  pl.BlockSpec((n_mem, kv_cols), lambda i: (i // per_b, layer)),
                  pl.BlockSpec((X_HEADS * HEAD_DIM, d), lambda i: (0, 0))],
        out_specs=pl.BlockSpec((tm, d), lambda i: (i, 0)),
        out_shape=jax.ShapeDtypeStruct((m, d), F32),
        compiler_params=_params("parallel"), name="cross_attn",
    )(h, g.reshape(1, d), wq, kv, wo)


def _ffn_kernel(h_ref, halo_ref, g_ref, wg_ref, wv_ref, cwg_ref, cwv_ref, cbg_ref, cbv_ref, wd_ref, gf_ref,
                o_ref, hn_ref, acc_ref, *, per_b, final_norm):
    i = pl.program_id(0)
    f = pl.program_id(1)
    tm = h_ref.shape[0]
    pad = SUBLANE * 2

    @pl.when(f == 0)
    def _():
        g = g_ref[...]
        hn_ref[pad:, :] = _rms_rows(h_ref[...], g).astype(BF16)
        halo = _rms_rows(halo_ref[...], g)
        halo = jnp.where(i % per_b == 0, 0.0, halo)
        hn_ref[:pad, :] = halo.astype(BF16)
        acc_ref[...] = jnp.zeros_like(acc_ref)

    hn = hn_ref[...]

    def conv(u, w, b):
        return (b + w[0:1, :] * u[pad - 2:pad - 2 + tm] + w[1:2, :] * u[pad - 1:pad - 1 + tm]
                + w[2:3, :] * u[pad:pad + tm])

    gate = conv(_dot(hn, wg_ref[...]), cwg_ref[...], cbg_ref[...])
    val = conv(_dot(hn, wv_ref[...]), cwv_ref[...], cbv_ref[...])
    act = (gate * _sigmoid(gate) * val).astype(BF16)
    acc_ref[...] += _dot(act, wd_ref[...])

    @pl.when(f == pl.num_programs(1) - 1)
    def _():
        if not final_norm:
            o_ref[...] = h_ref[...] + acc_ref[...]
        else:
            slab = SUBLANE * 16
            assert tm % slab == 0

            def finish(r, carry):
                rows = pl.ds(pl.multiple_of(r * slab, slab), slab)
                o_ref[rows, :] = _rms_rows(h_ref[rows, :] + acc_ref[rows, :], gf_ref[...])
                return carry

            lax.fori_loop(0, tm // slab, finish, 0)


def _ffn(h, g, w_up, conv_w, conv_b, w_down, g_final, batch, seq, *, tm, tf):
    m, d = h.shape
    dff = w_down.shape[0]
    assert seq % tm == 0 and dff % tf == 0
    per_b = seq // tm
    nf = dff // tf
    pad = SUBLANE * 2
    halo_blocks = tm // pad
    cw = jnp.zeros((SUBLANE, 2 * dff), F32).at[:CONV_WIDTH].set(conv_w)
    cb = conv_b.reshape(1, 2 * dff)
    return pl.pallas_call(
        functools.partial(_ffn_kernel, per_b=per_b, final_norm=g_final is not None),
        grid=(m // tm, nf),
        in_specs=[pl.BlockSpec((tm, d), lambda i, f: (i, 0)),
                  pl.BlockSpec((pad, d), lambda i, f: (jnp.maximum(i * halo_blocks - 1, 0), 0)),
                  pl.BlockSpec((1, d), lambda i, f: (0, 0)),
                  pl.BlockSpec((d, tf), lambda i, f: (0, f)),
                  pl.BlockSpec((d, tf), lambda i, f: (0, nf + f)),
                  pl.BlockSpec((SUBLANE, tf), lambda i, f: (0, f)),
                  pl.BlockSpec((SUBLANE, tf), lambda i, f: (0, nf + f)),
                  pl.BlockSpec((1, tf), lambda i, f: (0, f)),
                  pl.BlockSpec((1, tf), lambda i, f: (0, nf + f)),
                  pl.BlockSpec((tf, d), lambda i, f: (f, 0)),
                  pl.BlockSpec((1, d), lambda i, f: (0, 0))],
        out_specs=pl.BlockSpec((tm, d), lambda i, f: (i, 0), pipeline_mode=pl.Buffered(1)),
        out_shape=jax.ShapeDtypeStruct((m, d), F32),
        scratch_shapes=[pltpu.VMEM((tm + pad, d), BF16), pltpu.VMEM((tm, d), F32)],
        compiler_params=_params("parallel", "arbitrary"), name="conv_ffn",
    )(h, h, g.reshape(1, d), w_up, w_up, cw, cw, cb, cb, w_down,
      (g if g_final is None else g_final).reshape(1, d))


def _row_tile(m, want):
    return want if m % want == 0 else m


def _mixer_ab(h, g, w_in, pe, w1, w2, w_out, tabs, rot, batch, seq):
    m, d = h.shape
    cos, sin = tabs[0], tabs[1]
    w_pad = jnp.zeros((d, AB_BLOCKS * LANE), BF16).at[:, :AB_COLS].set(w_in.astype(BF16))
    proj = _norm_matmul(h, g, w_pad, tm=_row_tile(m, 1024), tn=2048, head_major=True)
    o_moba = _moba(proj, cos, sin, rot, batch, seq)
    kvc = _compress(proj, pe, w1, w2, batch, seq)
    o_cmp, sel = _nsa_cmp(proj, kvc, batch, seq)
    o_nsa = _nsa_sw(proj, sel, o_cmp, cos, sin, rot, batch, seq)
    return _matmul_res([o_moba, o_nsa], w_out.astype(BF16), h, tm=_row_tile(m, 1024), tn=1024)


def _mixer_c(h, g, w_in, gn, w_out, tabs, batch, seq):
    m, d = h.shape
    proj = _norm_matmul(h, g, w_in.astype(BF16), tm=_row_tile(m, 1024), tn=2048)
    y = _retention(proj, tabs[2], tabs[3], gn, batch, seq)
    return _matmul_res([y], w_out.astype(BF16), h, tm=_row_tile(m, 1024), tn=1024)


def kernel(x, mem, positions, norm_mix, norm_cross, norm_ffn, norm_mem, norm_final, w_in_ab, cmp_pe_k, cmp_w1_k,
           cmp_w2_k, cmp_pe_v, cmp_w1_v, cmp_w2_v, w_out_ab, w_in_c, ret_gn, w_out_c, w_q_x, w_kv_x, w_o_x, w_up,
           conv_w, conv_b, w_down):
    batch, seq, d = x.shape
    n_mem = mem.shape[1]
    depth = norm_mix.shape[0]
    m = batch * seq
    tabs = _rope_tables(positions)
    rot = _rot_matrix()

    kv = _norm_matmul(mem.reshape(batch * n_mem, d), norm_mem, w_kv_x.astype(BF16),
                      tm=_row_tile(batch * n_mem, 512), tn=512)

    h = x.reshape(m, d)
    for l in range(depth):
        if l % 2 == 0:
            e = l // 2
            h = _mixer_ab(h, norm_mix[l], w_in_ab[e],
                          jnp.stack([cmp_pe_k[e], cmp_pe_v[e]]), jnp.stack([cmp_w1_k[e], cmp_w1_v[e]]),
                          jnp.stack([cmp_w2_k[e], cmp_w2_v[e]]), w_out_ab[e], tabs, rot, batch, seq)
        else:
            o = l // 2
            h = _mixer_c(h, norm_mix[l], w_in_c[o], ret_gn[o], w_out_c[o], tabs, batch, seq)
        h = _cross(h, norm_cross[l], w_q_x[l].astype(BF16), kv, w_o_x[l].astype(BF16), l, batch, seq, n_mem,
                   tm=_row_tile(seq, 1024))
        h = _ffn(h, norm_ffn[l], w_up[l].astype(BF16), conv_w[l], conv_b[l], w_down[l].astype(BF16),
                 norm_final if l == depth - 1 else None, batch, seq, tm=_row_tile(seq, 1024), tf=512)
    return h.reshape(batch, seq, d)
```

```python
import functools

import numpy as np
import jax
import jax.numpy as jnp
from jax import lax
from jax.experimental import pallas as pl
from jax.experimental.pallas import tpu as pltpu

F32 = jnp.float32
BF16 = jnp.bfloat16

D_MODEL = 2048
DEPTH = 4
HEAD_DIM = 128
ROPE_THETA = 500000.0
ROPE_DIM = HEAD_DIM // 4
X_HEADS = 4
MOBA_HEADS = 8
MOBA_BLOCK = 256
MOBA_TOPK = 3
NSA_HEADS = 8
NSA_GROUPS = 2
NSA_REP = NSA_HEADS // NSA_GROUPS
NSA_CMP_LEN = 32
NSA_CMP_STRIDE = 16
NSA_SEL_LEN = 64
NSA_TOPK = 16
NSA_WINDOW = 512
RET_HEADS = 8
RET_DK = 256
RET_DV = 512
RET_THETA = 10000.0
D_FF = 5632
CONV_WIDTH = 3
RMS_EPS = 1e-6
NEG = -1e30
FORCE = 1e9
LOG2E = 1.4426950408889634
Q_SCALE = HEAD_DIM ** -0.5 * LOG2E

LANE = 128
SUBLANE = 8
VMEM_LIMIT = 56 * 2 ** 20

AB_SIZES = (MOBA_HEADS * HEAD_DIM,) * 3 + (NSA_HEADS * HEAD_DIM,) + (NSA_GROUPS * HEAD_DIM,) * 6 + (NSA_HEADS * 3,)
AB_COLS = sum(AB_SIZES)
AB_BLOCKS = 48
BLK_MQ, BLK_MK, BLK_MV, BLK_NQ = 0, 8, 16, 24
BLK_NKC, BLK_NVC, BLK_NKS, BLK_NVS, BLK_NKW, BLK_NVW, BLK_NG = 32, 34, 36, 38, 40, 42, 44

ATT_TQ = 256
ATT_TK = 256
CMP_TQ = 1024
RET_CHUNK = 256
MOBA_GROUP = 4
MOBA_HB = 4
SEL_GROUP = 4

_NT = (((1,), (1,)), ((), ()))
_TN = (((0,), (0,)), ((), ()))


def _params(*sem):
    return pltpu.CompilerParams(dimension_semantics=sem, vmem_limit_bytes=VMEM_LIMIT)


def _dot(a, b):
    return jnp.dot(a, b, preferred_element_type=F32)


def _dot_nt(a, b):
    return lax.dot_general(a, b, _NT, preferred_element_type=F32)


def _sigmoid(x):
    return 1.0 / (1.0 + jnp.exp(-x))


def _rms_rows(x, g):
    ms = jnp.mean(x * x, axis=-1, keepdims=True)
    return x * lax.rsqrt(ms + RMS_EPS) * g


def _tables_kernel(pos_ref, inv_rope_ref, inv_ret_ref, c_ref, s_ref, cr_ref, sr_ref):
    pos = pos_ref[...]
    lane = lax.broadcasted_iota(jnp.int32, pos.shape, 1)
    rot = lane < ROPE_DIM
    ang = pos * inv_rope_ref[...]
    c_ref[...] = jnp.where(rot, jnp.cos(ang), 1.0)
    s_ref[...] = jnp.where(rot, jnp.sin(ang), 0.0)
    ang_r = pos * inv_ret_ref[...]
    cr_ref[...] = jnp.cos(ang_r)
    sr_ref[...] = jnp.sin(ang_r)


def _rope_tables(positions):
    m = positions.size
    posb = jnp.broadcast_to(positions.reshape(m, 1).astype(F32), (m, LANE))
    half = ROPE_DIM // 2
    inv = jnp.float32(ROPE_THETA) ** (-jnp.arange(half, dtype=F32) / half)
    inv_rope = jnp.concatenate([inv, inv, jnp.zeros((LANE - ROPE_DIM,), F32)]).reshape(1, LANE)
    half_r = RET_DK // 2
    inv_ret = (jnp.float32(RET_THETA) ** (-jnp.arange(half_r, dtype=F32) / half_r)).reshape(1, LANE)
    tm = 1024 if m % 1024 == 0 else m
    row = pl.BlockSpec((tm, LANE), lambda i: (i, 0))
    vec = pl.BlockSpec((1, LANE), lambda i: (0, 0))
    sds = jax.ShapeDtypeStruct((m, LANE), F32)
    return pl.pallas_call(
        _tables_kernel, grid=(m // tm,), in_specs=[row, vec, vec], out_specs=[row] * 4,
        out_shape=[sds] * 4, compiler_params=_params("parallel"), name="rope_tables",
    )(posb, inv_rope, inv_ret)


def _rot_matrix():
    half = ROPE_DIM // 2
    p = np.zeros((HEAD_DIM, HEAD_DIM), np.float32)
    for l in range(half):
        p[l + half, l] = -1.0
        p[l, l + half] = 1.0
    return jnp.asarray(p, BF16)


def _rope_partial(x_bf16, c, s, p_ref):
    return x_bf16.astype(F32) * c + _dot(x_bf16, p_ref[...]) * s


def _norm_matmul_kernel(x_ref, g_ref, w_ref, o_ref, xn_ref, *, head_major):
    @pl.when(pl.program_id(1) == 0)
    def _():
        xn_ref[...] = _rms_rows(x_ref[...], g_ref[...]).astype(BF16)

    acc = _dot(xn_ref[...], w_ref[...])
    if head_major:
        for c in range(o_ref.shape[0]):
            o_ref[c] = acc[:, c * LANE:(c + 1) * LANE].astype(o_ref.dtype)
    else:
        o_ref[...] = acc.astype(o_ref.dtype)


def _norm_matmul(x, g, w, *, tm, tn, head_major=False):
    stack = w.shape[0] if w.ndim == 3 else 1
    m, k = x.shape
    per_w = w.shape[-1] // tn
    n = stack * w.shape[-1]
    assert m % tm == 0 and w.shape[-1] % tn == 0 and tn % LANE == 0
    if head_major:
        out_shape = jax.ShapeDtypeStruct((n // LANE, m, LANE), BF16)
        out_spec = pl.BlockSpec((tn // LANE, tm, LANE), lambda i, j: (j, i, 0))
    else:
        out_shape = jax.ShapeDtypeStruct((m, n), BF16)
        out_spec = pl.BlockSpec((tm, tn), lambda i, j: (i, j))
    return pl.pallas_call(
        functools.partial(_norm_matmul_kernel, head_major=head_major),
        grid=(m // tm, n // tn),
        in_specs=[pl.BlockSpec((tm, k), lambda i, j: (i, 0)),
                  pl.BlockSpec((1, k), lambda i, j: (0, 0)),
                  pl.BlockSpec((k, tn), lambda i, j: (0, j)) if w.ndim == 2 else
                  pl.BlockSpec((None, k, tn), lambda i, j: (j // per_w, 0, j % per_w))],
        out_specs=out_spec, out_shape=out_shape,
        scratch_shapes=[pltpu.VMEM((tm, k), BF16)],
        compiler_params=_params("parallel", "arbitrary"), name="norm_matmul",
    )(x, g.reshape(1, k), w)


def _matmul_res_kernel(*refs):
    r_ref, o_ref = refs[-2], refs[-1]
    acc = r_ref[...]
    for x_ref, w_ref in zip(refs[:-2:2], refs[1:-2:2]):
        acc = acc + _dot(x_ref[...], w_ref[...])
    o_ref[...] = acc


def _matmul_res(xs, w, res, *, tm, tn):
    m, n = res.shape
    k = xs[0].shape[1]
    assert m % tm == 0 and n % tn == 0 and all(x.shape[1] == k for x in xs) and w.shape[0] == k * len(xs)
    in_specs, args = [], []
    for idx, x in enumerate(xs):
        in_specs += [pl.BlockSpec((tm, k), lambda i, j: (i, 0)),
                     pl.BlockSpec((k, tn), lambda i, j, idx=idx: (idx, j))]
        args += [x, w]
    return pl.pallas_call(
        _matmul_res_kernel, grid=(m // tm, n // tn),
        in_specs=in_specs + [pl.BlockSpec((tm, tn), lambda i, j: (i, j))],
        out_specs=pl.BlockSpec((tm, tn), lambda i, j: (i, j)),
        out_shape=jax.ShapeDtypeStruct((m, n), F32),
        compiler_params=_params("parallel", "arbitrary"), name="matmul_res",
    )(*args, res)


def _topk_rows(x, n_valid, k, n_rows):
    rows, nq = x.shape
    ridx = lax.broadcasted_iota(jnp.int32, (rows, nq), 0)
    x = jnp.where(ridx < n_valid, x, NEG)
    x = jnp.where(ridx < n_rows, x, -jnp.inf)
    rank = jnp.zeros((rows, nq), F32)
    for i in range(n_rows):
        xi = x[i:i + 1, :]
        rank = rank + jnp.where(ridx > i, jnp.where(xi >= x, 1.0, 0.0), jnp.where(xi > x, 1.0, 0.0))
    marks = jnp.where(rank < float(k), jnp.where(x > NEG / 2, 1.0, 0.0), 0.0)
    if rows < LANE:
        marks = jnp.concatenate([marks, jnp.zeros((LANE - rows, nq), F32)], axis=0)
    return marks.T


def _moba_kernel(q_ref, k_ref, v_ref, c_ref, s_ref, p_ref, o_ref, krot_ref, kmean_ref, vext_ref, *, nb):
    qi = pl.program_id(2)
    blk = MOBA_BLOCK
    nb_pad = -(-nb // SUBLANE) * SUBLANE
    grp = MOBA_GROUP
    heads = range(MOBA_HB)

    @pl.when(qi == 0)
    def _():
        kmean_ref[...] = jnp.zeros_like(kmean_ref)
        for hh in heads:
            for j in range(nb):
                rows = slice(j * blk, (j + 1) * blk)
                kr = _rope_partial(k_ref[hh, rows, :], c_ref[rows, :], s_ref[rows, :], p_ref)
                krot_ref[hh, rows, :] = kr.astype(BF16)
                kmean_ref[hh, j:j + 1, :] = jnp.mean(kr, axis=0, keepdims=True)
            vext_ref[hh, :, :HEAD_DIM] = v_ref[hh]
            vext_ref[hh, :, HEAD_DIM:] = jnp.ones((v_ref.shape[1], HEAD_DIM), BF16)

    r0 = pl.multiple_of(qi * blk, blk)
    cq = c_ref[pl.ds(r0, blk), :]
    sq = s_ref[pl.ds(r0, blk), :]
    lane = lax.broadcasted_iota(jnp.int32, (blk, LANE), 1)
    row = lax.broadcasted_iota(jnp.int32, (blk, blk), 0)
    col = lax.broadcasted_iota(jnp.int32, (blk, blk), 1)

    qbs, sels, state = [], [], []
    for hh in heads:
        q = _rope_partial(q_ref[hh], cq, sq, p_ref)
        qb = (q * Q_SCALE).astype(BF16)
        gate_t = lax.dot_general(kmean_ref[hh], q, _NT, precision=lax.Precision.HIGHEST,
                                 preferred_element_type=F32)[:nb_pad, :]
        sels.append(_topk_rows(gate_t, qi, min(MOBA_TOPK, nb), nb))
        qbs.append(qb)
        s = jnp.where(col <= row, _dot_nt(qb, krot_ref[hh, pl.ds(r0, blk), :]), NEG)
        m = jnp.max(s, axis=-1, keepdims=True)
        p = jnp.exp2((s - m).astype(BF16))
        state += [m, _dot(p, vext_ref[hh, pl.ds(r0, blk), :])]

    def body(jg, carry):
        off = pl.multiple_of(jg * (grp * blk), grp * blk)
        out = []
        for hh in heads:
            m, acc = carry[2 * hh:2 * hh + 2]
            s = _dot_nt(qbs[hh], krot_ref[hh, pl.ds(off, grp * blk), :])
            parts = []
            for t in range(grp):
                chosen = jnp.sum(jnp.where(lane == jg * grp + t, sels[hh], 0.0), axis=-1, keepdims=True)
                parts.append(jnp.where(chosen > 0.5, s[:, t * blk:(t + 1) * blk], NEG))
            s = jnp.concatenate(parts, axis=-1)
            m_new = jnp.maximum(m, jnp.max(s, axis=-1, keepdims=True))
            p = jnp.exp2((s - m_new).astype(BF16))
            out += [m_new, jnp.exp2(m - m_new) * acc + _dot(p, vext_ref[hh, pl.ds(off, grp * blk), :])]
        return tuple(out)

    state = lax.fori_loop(0, (qi + grp - 1) // grp, body, tuple(state))
    for hh in heads:
        acc = state[2 * hh + 1]
        o_ref[:, hh * HEAD_DIM:(hh + 1) * HEAD_DIM] = (acc[:, :HEAD_DIM] / acc[:, HEAD_DIM:HEAD_DIM + 1]).astype(o_ref.dtype)


def _moba(proj, cos, sin, rot, batch, seq):
    nb = seq // MOBA_BLOCK
    hb = MOBA_HB
    assert seq % (MOBA_BLOCK * MOBA_GROUP) == 0 and nb <= LANE
    assert MOBA_HEADS % hb == 0 and BLK_MQ % hb == 0 and BLK_MK % hb == 0 and BLK_MV % hb == 0
    m = batch * seq
    head = lambda off: pl.BlockSpec((hb, seq, HEAD_DIM), lambda b, h, i: (off // hb + h, b, 0))
    table = pl.BlockSpec((seq, LANE), lambda b, h, i: (b, 0))
    return pl.pallas_call(
        functools.partial(_moba_kernel, nb=nb),
        grid=(batch, MOBA_HEADS // hb, nb),
        in_specs=[pl.BlockSpec((hb, MOBA_BLOCK, HEAD_DIM), lambda b, h, i: (BLK_MQ // hb + h, b * nb + i, 0)),
                  head(BLK_MK), head(BLK_MV), table, table,
                  pl.BlockSpec((HEAD_DIM, HEAD_DIM), lambda b, h, i: (0, 0))],
        out_specs=pl.BlockSpec((MOBA_BLOCK, hb * HEAD_DIM), lambda b, h, i: (b * nb + i, h)),
        out_shape=jax.ShapeDtypeStruct((m, MOBA_HEADS * HEAD_DIM), BF16),
        scratch_shapes=[pltpu.VMEM((hb, seq, HEAD_DIM), BF16), pltpu.VMEM((hb, LANE, HEAD_DIM), F32),
                        pltpu.VMEM((hb, seq, 2 * HEAD_DIM), BF16)],
        compiler_params=_params("parallel", "parallel", "arbitrary"), name="moba",
    )(proj, proj, proj, cos, sin, rot)


def _compress_kernel(x_ref, pe_ref, w1_ref, w2_ref, o_ref, xf_ref):
    seq = x_ref.shape[0]
    n_sub = seq // NSA_CMP_STRIDE
    xf_ref[:seq, :] = x_ref[...].astype(F32)
    xf_ref[seq:, :] = jnp.zeros((NSA_CMP_STRIDE, HEAD_DIM), F32)
    hid = _dot(pe_ref[...], w1_ref[...])[0:1, :]
    for r in range(NSA_CMP_LEN):
        rows = xf_ref[pl.ds(r, n_sub, stride=NSA_CMP_STRIDE), :]
        hid = hid + _dot(rows.astype(BF16), w1_ref[r * HEAD_DIM:(r + 1) * HEAD_DIM, :])
    hid = hid * _sigmoid(hid)
    o_ref[...] = _dot(hid.astype(BF16), w2_ref[...]).astype(o_ref.dtype)


def _compress(proj, pe, w1, w2, batch, seq):
    n_sub = seq // NSA_CMP_STRIDE
    flat = NSA_CMP_LEN * HEAD_DIM
    pe_flat = jnp.zeros((2, SUBLANE * 2, flat), BF16).at[:, 0].set(pe.reshape(2, flat).astype(BF16))
    return pl.pallas_call(
        _compress_kernel, grid=(2, batch, NSA_GROUPS),
        in_specs=[pl.BlockSpec((None, seq, HEAD_DIM), lambda t, b, g: (BLK_NKC + 2 * t + g, b, 0)),
                  pl.BlockSpec((None, SUBLANE * 2, flat), lambda t, b, g: (t, 0, 0)),
                  pl.BlockSpec((None, flat, HEAD_DIM), lambda t, b, g: (t, 0, 0)),
                  pl.BlockSpec((None, HEAD_DIM, HEAD_DIM), lambda t, b, g: (t, 0, 0))],
        out_specs=pl.BlockSpec((None, None, None, n_sub, HEAD_DIM), lambda t, b, g: (t, b, g, 0, 0)),
        out_shape=jax.ShapeDtypeStruct((2, batch, NSA_GROUPS, n_sub, HEAD_DIM), BF16),
        scratch_shapes=[pltpu.VMEM((seq + NSA_CMP_STRIDE, HEAD_DIM), F32)],
        compiler_params=_params("parallel", "parallel", "parallel"), name="nsa_compress",
    )(proj, pe_flat, w1.astype(BF16), w2.astype(BF16))


def _nsa_cmp_kernel(q_ref, kc_ref, vc_ref, cov_ref, oc_ref, sel_ref, *, n_sel):
    qi = pl.program_id(2)
    tq = CMP_TQ
    n_cmp = kc_ref.shape[0]
    scale = HEAD_DIM ** -0.5
    q_pos = qi * tq + lax.broadcasted_iota(jnp.int32, (tq, n_cmp), 0)
    n_idx = lax.broadcasted_iota(jnp.int32, (tq, n_cmp), 1)
    ok = n_idx * NSA_CMP_STRIDE + (NSA_CMP_LEN - 1) <= q_pos
    kc = kc_ref[...]
    vc = vc_ref[...]
    p_sum = jnp.zeros((tq, n_cmp), F32)
    for r in range(NSA_REP):
        s = jnp.where(ok, _dot_nt(q_ref[r], kc) * scale, NEG)
        mx = jnp.max(s, axis=-1, keepdims=True)
        e = jnp.where(ok, jnp.exp(s - mx), 0.0)
        l = jnp.sum(e, axis=-1, keepdims=True)
        p = e * jnp.where(l > 0.0, 1.0 / l, 0.0)
        oc_ref[r] = _dot(p.astype(BF16), vc).astype(oc_ref.dtype)
        p_sum = p_sum + p

    n_rows = -(-n_sel // SUBLANE) * SUBLANE
    imp = lax.dot_general(cov_ref[...], p_sum, _NT, precision=lax.Precision.HIGHEST,
                          preferred_element_type=F32)[:n_rows, :]
    ridx = lax.broadcasted_iota(jnp.int32, (n_rows, tq), 0)
    blk = (qi * tq + lax.broadcasted_iota(jnp.int32, (n_rows, tq), 1)) // NSA_SEL_LEN
    forced = (ridx == 0) | (ridx == blk) | (ridx == blk - 1)
    x = jnp.where(forced, FORCE, imp)
    sel_ref[...] = _topk_rows(x, blk[0:1, :] + 1, min(NSA_TOPK, n_sel), n_sel).astype(sel_ref.dtype)


def _cover_matrix(n_cmp_pad, n_cmp, n_sel):
    c_start = np.arange(n_cmp_pad) * NSA_CMP_STRIDE
    s_start = np.arange(LANE) * NSA_SEL_LEN
    cover = (c_start[:, None] < s_start[None, :] + NSA_SEL_LEN) & (c_start[:, None] + NSA_CMP_LEN > s_start[None, :])
    cover &= (np.arange(n_cmp_pad) < n_cmp)[:, None] & (np.arange(LANE) < n_sel)[None, :]
    return jnp.asarray(cover.T, F32)


def _nsa_cmp(proj, kvc, batch, seq):
    nq = seq // CMP_TQ
    n_sub = seq // NSA_CMP_STRIDE
    n_sel = seq // NSA_SEL_LEN
    assert n_sel <= LANE
    m = batch * seq
    cover = _cover_matrix(n_sub, n_sub - NSA_CMP_LEN // NSA_CMP_STRIDE + 1, n_sel)
    cmp_spec = lambda t: pl.BlockSpec((None, None, None, n_sub, HEAD_DIM), lambda b, g, i: (t, b, g, 0, 0))
    return pl.pallas_call(
        functools.partial(_nsa_cmp_kernel, n_sel=n_sel),
        grid=(batch, NSA_GROUPS, nq),
        in_specs=[pl.BlockSpec((NSA_REP, CMP_TQ, HEAD_DIM), lambda b, g, i: (BLK_NQ // NSA_REP + g, b * nq + i, 0)),
                  cmp_spec(0), cmp_spec(1),
                  pl.BlockSpec((LANE, n_sub), lambda b, g, i: (0, 0))],
        out_specs=[pl.BlockSpec((NSA_REP, CMP_TQ, HEAD_DIM), lambda b, g, i: (g, b * nq + i, 0)),
                   pl.BlockSpec((CMP_TQ, LANE), lambda b, g, i: (b * nq + i, g))],
        out_shape=[jax.ShapeDtypeStruct((NSA_HEADS, m, HEAD_DIM), BF16),
                   jax.ShapeDtypeStruct((m, NSA_GROUPS * LANE), BF16)],
        compiler_params=_params("parallel", "parallel", "parallel"), name="nsa_cmp",
    )(proj, kvc, kvc, cover)


def _nsa_sw_kernel(q_ref, ks_ref, vs_ref, kw_ref, vw_ref, sel_ref, oc_ref, gl_ref, c_ref, s_ref, p_ref,
                   o_ref, ksr_ref, kwr_ref, vwp_ref, vse_ref, qr_ref, m_ref, acc_ref, ow_ref, *, seq):
    grp = pl.program_id(1)
    qi = pl.program_id(2)
    tq, tk = ATT_TQ, ATT_TK
    rep = NSA_REP
    win = NSA_WINDOW
    gk = SEL_GROUP * tk

    @pl.when(qi == 0)
    def _():
        kwr_ref[:win, :] = jnp.zeros((win, HEAD_DIM), BF16)
        vwp_ref[:win, :] = jnp.zeros((win, 2 * HEAD_DIM), BF16)
        vwp_ref[win:, HEAD_DIM:] = jnp.ones((seq, HEAD_DIM), BF16)
        vse_ref[:, :HEAD_DIM] = vs_ref[...]
        vse_ref[:, HEAD_DIM:] = jnp.ones((seq, HEAD_DIM), BF16)
        for j in range(seq // tk):
            rows = slice(j * tk, (j + 1) * tk)
            shifted = slice(win + j * tk, win + (j + 1) * tk)
            c, s = c_ref[rows, :], s_ref[rows, :]
            ksr_ref[rows, :] = _rope_partial(ks_ref[rows, :], c, s, p_ref).astype(BF16)
            kwr_ref[shifted, :] = _rope_partial(kw_ref[rows, :], c, s, p_ref).astype(BF16)
            vwp_ref[shifted, :HEAD_DIM] = vw_ref[rows, :]

    r0 = pl.multiple_of(qi * tq, tq)
    cq = c_ref[pl.ds(r0, tq), :]
    sq = s_ref[pl.ds(r0, tq), :]
    for r in range(rep):
        qr_ref[r] = (_rope_partial(q_ref[r], cq, sq, p_ref) * Q_SCALE).astype(BF16)

    q_all = qr_ref[...].reshape(rep * tq, HEAD_DIM)

    row_w = lax.broadcasted_iota(jnp.int32, (tq, win + tq), 0)
    col_w = lax.broadcasted_iota(jnp.int32, (tq, win + tq), 1)
    in_win = (col_w > row_w) & (col_w <= row_w + win) & (col_w >= win - r0)
    s = _dot_nt(q_all, kwr_ref[pl.ds(r0, win + tq), :]).reshape(rep, tq, win + tq)
    s = jnp.where(in_win[None], s, NEG)
    p = jnp.exp2((s - jnp.max(s, axis=-1, keepdims=True)).astype(BF16))
    o_w = _dot(p.reshape(rep * tq, win + tq), vwp_ref[pl.ds(r0, win + tq), :]).reshape(rep, tq, 2 * HEAD_DIM)
    ow_ref[...] = o_w[..., :HEAD_DIM] / o_w[..., HEAD_DIM:HEAD_DIM + 1]

    sel = sel_ref[...]
    row = lax.broadcasted_iota(jnp.int32, (tq, gk), 0)
    col = lax.broadcasted_iota(jnp.int32, (tq, gk), 1)

    def chosen_keys(jg):
        blk_row = lax.broadcasted_iota(jnp.int32, (LANE, gk), 0)
        blk_col = jg * (gk // NSA_SEL_LEN) + lax.broadcasted_iota(jnp.int32, (LANE, gk), 1) // NSA_SEL_LEN
        return _dot(sel, jnp.where(blk_row == blk_col, 1.0, 0.0).astype(BF16))

    def attend(off, valid, first):
        s = _dot_nt(q_all, ksr_ref[pl.ds(off, gk), :]).reshape(rep, tq, gk)
        s = jnp.where(valid[None], s, NEG)
        m_new = jnp.max(s, axis=-1, keepdims=True)
        if not first:
            m_old = m_ref[...]
            m_new = jnp.maximum(m_old, m_new)
        p = jnp.exp2((s - m_new).astype(BF16))
        v_grp = vse_ref[pl.ds(off, gk), :]
        for r in range(rep):
            pv = _dot(p[r], v_grp)
            if first:
                acc_ref[r] = pv
            else:
                acc_ref[r] = jnp.exp2(m_old[r] - m_new[r]) * acc_ref[r] + pv
        m_ref[...] = m_new

    gd = qi // SEL_GROUP
    off_d = pl.multiple_of(gd * gk, gk)
    causal = col + off_d <= row + r0
    attend(off_d, jnp.where(causal, chosen_keys(gd), 0.0) > 0.5, True)

    def sel_body(jg, carry):
        attend(pl.multiple_of(jg * gk, gk), chosen_keys(jg) > 0.5, False)
        return carry

    lax.fori_loop(0, gd, sel_body, 0)

    gates = _sigmoid(gl_ref[...].astype(F32))
    lane = lax.broadcasted_iota(jnp.int32, (tq, LANE), 1)

    def gate_col(c):
        return jnp.sum(jnp.where(lane == c, gates, 0.0), axis=-1, keepdims=True)

    for r in range(rep):
        base = (grp * rep + r) * 3
        o_s = acc_ref[r, :, :HEAD_DIM] / acc_ref[r, :, HEAD_DIM:HEAD_DIM + 1]
        o = gate_col(base) * oc_ref[r].astype(F32) + gate_col(base + 1) * o_s + gate_col(base + 2) * ow_ref[r]
        o_ref[:, r * HEAD_DIM:(r + 1) * HEAD_DIM] = o.astype(o_ref.dtype)


def _nsa_sw(proj, sel, o_cmp, cos, sin, rot, batch, seq):
    nq = seq // ATT_TQ
    assert seq % (SEL_GROUP * ATT_TK) == 0 and NSA_WINDOW % ATT_TK == 0 and ATT_TQ == ATT_TK
    m = batch * seq
    rep = NSA_REP
    head = lambda off: pl.BlockSpec((None, seq, HEAD_DIM), lambda b, g, i: (off + g, b, 0))
    table = pl.BlockSpec((seq, LANE), lambda b, g, i: (b, 0))
    q_like = lambda off: pl.BlockSpec((rep, ATT_TQ, HEAD_DIM), lambda b, g, i: (off + g, b * nq + i, 0))
    return pl.pallas_call(
        functools.partial(_nsa_sw_kernel, seq=seq),
        grid=(batch, NSA_GROUPS, nq),
        in_specs=[q_like(BLK_NQ // rep), head(BLK_NKS), head(BLK_NVS), head(BLK_NKW), head(BLK_NVW),
                  pl.BlockSpec((ATT_TQ, LANE), lambda b, g, i: (b * nq + i, g)),
                  q_like(0),
                  pl.BlockSpec((None, ATT_TQ, LANE), lambda b, g, i: (BLK_NG, b * nq + i, 0)),
                  table, table,
                  pl.BlockSpec((HEAD_DIM, HEAD_DIM), lambda b, g, i: (0, 0))],
        out_specs=pl.BlockSpec((ATT_TQ, rep * HEAD_DIM), lambda b, g, i: (b * nq + i, g)),
        out_shape=jax.ShapeDtypeStruct((m, NSA_HEADS * HEAD_DIM), BF16),
        scratch_shapes=[pltpu.VMEM((seq, HEAD_DIM), BF16),
                        pltpu.VMEM((seq + NSA_WINDOW, HEAD_DIM), BF16),
                        pltpu.VMEM((seq + NSA_WINDOW, 2 * HEAD_DIM), BF16),
                        pltpu.VMEM((seq, 2 * HEAD_DIM), BF16),
                        pltpu.VMEM((rep, ATT_TQ, HEAD_DIM), BF16),
                        pltpu.VMEM((rep, ATT_TQ, 1), F32),
                        pltpu.VMEM((rep, ATT_TQ, 2 * HEAD_DIM), F32), pltpu.VMEM((rep, ATT_TQ, HEAD_DIM), F32)],
        compiler_params=_params("parallel", "parallel", "arbitrary"), name="nsa_sel_win",
    )(proj, proj, proj, proj, proj, sel, o_cmp, proj, cos, sin, rot)


def _rope_full(x_bf16, c, s):
    half = x_bf16.shape[-1] // 2
    x = x_bf16.astype(F32)
    x1, x2 = x[:, :half], x[:, half:]
    return jnp.concatenate([x1 * c - x2 * s, x1 * s + x2 * c], axis=-1)


def _retention_kernel(q_ref, k_ref, v_ref, g_ref, c_ref, s_ref, lg_ref, gn_ref, o_ref, state_ref, decay_ref):
    ci = pl.program_id(1)
    ch = q_ref.shape[0]

    @pl.when(ci == 0)
    def _():
        state_ref[...] = jnp.zeros_like(state_ref)
        n_row = lax.broadcasted_iota(jnp.int32, (ch, ch), 0)
        n_col = lax.broadcasted_iota(jnp.int32, (ch, ch), 1)
        diff = (n_row - n_col).astype(F32)
        for h in range(RET_HEADS):
            decay_ref[h] = jnp.where(diff >= 0.0, jnp.exp(jnp.maximum(diff, 0.0) * lg_ref[h][:, 0:1]), 0.0)

    c, s = c_ref[...], s_ref[...]
    n_vec = lax.broadcasted_iota(jnp.int32, (ch, 1), 0).astype(F32)

    for h in range(RET_HEADS):
        log_g = lg_ref[h][:, 0:1]
        qk_cols = slice(h * RET_DK, (h + 1) * RET_DK)
        v_cols = slice(h * RET_DV, (h + 1) * RET_DV)
        q = _rope_full(q_ref[:, qk_cols], c, s)
        k = _rope_full(k_ref[:, qk_cols], c, s) * (RET_DK ** -0.5)
        v = v_ref[:, v_cols]

        q_dec = jnp.exp((n_vec + 1.0) * log_g)
        k_dec = jnp.exp((ch - 1.0 - n_vec) * log_g)
        c_dec = jnp.exp(ch * log_g)

        qb = q.astype(BF16)
        state = state_ref[h]
        scores = _dot_nt(qb, k.astype(BF16)) * decay_ref[h]
        y = _dot(scores.astype(BF16), v) + _dot(qb, state.astype(BF16)) * q_dec
        state_ref[h] = state * c_dec + lax.dot_general((k * k_dec).astype(BF16), v, _TN,
                                                       preferred_element_type=F32)

        mu = jnp.mean(y, axis=-1, keepdims=True)
        yc = y - mu
        var = jnp.mean(yc * yc, axis=-1, keepdims=True)
        yn = yc * lax.rsqrt(var + RMS_EPS) * gn_ref[h]
        gate = g_ref[:, v_cols].astype(F32)
        o_ref[:, v_cols] = (gate * _sigmoid(gate) * yn).astype(o_ref.dtype)


def _retention(proj, cos_r, sin_r, gn, batch, seq):
    ch = RET_CHUNK
    nch = seq // ch
    assert seq % ch == 0
    m = batch * seq
    log_g = jnp.log(1.0 - jnp.exp2(-5.0 - jnp.arange(RET_HEADS, dtype=F32)))
    log_g = jnp.broadcast_to(log_g[:, None, None], (RET_HEADS, 1, LANE))
    qk_w = RET_HEADS * RET_DK
    v_w = RET_HEADS * RET_DV
    assert v_w == 2 * qk_w
    rows = lambda b, c: b * nch + c
    return pl.pallas_call(
        _retention_kernel, grid=(batch, nch),
        in_specs=[pl.BlockSpec((ch, qk_w), lambda b, c: (rows(b, c), 0)),
                  pl.BlockSpec((ch, qk_w), lambda b, c: (rows(b, c), 1)),
                  pl.BlockSpec((ch, v_w), lambda b, c: (rows(b, c), 1)),
                  pl.BlockSpec((ch, v_w), lambda b, c: (rows(b, c), 2)),
                  pl.BlockSpec((ch, LANE), lambda b, c: (rows(b, c), 0)),
                  pl.BlockSpec((ch, LANE), lambda b, c: (rows(b, c), 0)),
                  pl.BlockSpec((RET_HEADS, 1, LANE), lambda b, c: (0, 0, 0)),
                  pl.BlockSpec((RET_HEADS, 1, RET_DV), lambda b, c: (0, 0, 0))],
        out_specs=pl.BlockSpec((ch, v_w), lambda b, c: (rows(b, c), 0)),
        out_shape=jax.ShapeDtypeStruct((m, v_w), BF16),
        scratch_shapes=[pltpu.VMEM((RET_HEADS, RET_DK, RET_DV), F32), pltpu.VMEM((RET_HEADS, ch, ch), F32)],
        compiler_params=_params("parallel", "arbitrary"), name="retention",
    )(proj, proj, proj, proj, cos_r, sin_r, log_g, gn.reshape(RET_HEADS, 1, RET_DV).astype(F32))


def _cross_kernel(h_ref, g_ref, wq_ref, kv_ref, wo_ref, o_ref):
    h = h_ref[...]
    hn = _rms_rows(h, g_ref[...]).astype(BF16)
    q = (_dot(hn, wq_ref[...]) * Q_SCALE).astype(BF16)
    kv_cols = X_HEADS * HEAD_DIM
    ones = jnp.ones((kv_ref.shape[0], HEAD_DIM), BF16)
    outs = []
    for hd in range(X_HEADS):
        cols = slice(hd * HEAD_DIM, (hd + 1) * HEAD_DIM)
        s = _dot_nt(q[:, cols], kv_ref[:, cols])
        p = jnp.exp2((s - jnp.max(s, axis=-1, keepdims=True)).astype(BF16))
        v_ext = jnp.concatenate([kv_ref[:, kv_cols + hd * HEAD_DIM:kv_cols + (hd + 1) * HEAD_DIM], ones], axis=-1)
        pv = _dot(p, v_ext)
        outs.append(pv[:, :HEAD_DIM] / pv[:, HEAD_DIM:HEAD_DIM + 1])
    o = jnp.concatenate(outs, axis=-1).astype(BF16)
    o_ref[...] = h + _dot(o, wo_ref[...])


def _cross(h, g, wq, kv, wo, layer, batch, seq, n_mem, *, tm):
    m, d = h.shape
    per_b = seq // tm
    kv_cols = 2 * X_HEADS * HEAD_DIM
    return pl.pallas_call(
        _cross_kernel, grid=(m // tm,),
        in_specs=[pl.BlockSpec((tm, d), lambda i: (i, 0)),
                  pl.BlockSpec((1, d), lambda i: (0, 0)),
                  pl.BlockSpec((d, X_HEADS * HEAD_DIM), lambda i: (0, 0)),
                  pl.BlockSpec((n_mem, kv_cols), lambda i: (i // per_b, layer)),
                  pl.BlockSpec((X_HEADS * HEAD_DIM, d), lambda i: (0, 0))],
        out_specs=pl.BlockSpec((tm, d), lambda i: (i, 0)),
        out_shape=jax.ShapeDtypeStruct((m, d), F32),
        compiler_params=_params("parallel"), name="cross_attn",
    )(h, g.reshape(1, d), wq, kv, wo)


def _ffn_kernel(h_ref, halo_ref, g_ref, wg_ref, wv_ref, cwg_ref, cwv_ref, cbg_ref, cbv_ref, wd_ref, gf_ref,
                o_ref, hn_ref, acc_ref, *, per_b, final_norm):
    i = pl.program_id(0)
    f = pl.program_id(1)
    tm = h_ref.shape[0]
    pad = SUBLANE * 2

    @pl.when(f == 0)
    def _():
        g = g_ref[...]
        hn_ref[pad:, :] = _rms_rows(h_ref[...], g).astype(BF16)
        halo = _rms_rows(halo_ref[...], g)
        halo = jnp.where(i % per_b == 0, 0.0, halo)
        hn_ref[:pad, :] = halo.astype(BF16)
        acc_ref[...] = jnp.zeros_like(acc_ref)

    hn = hn_ref[...]

    def conv(u, w, b):
        return (b + w[0:1, :] * u[pad - 2:pad - 2 + tm] + w[1:2, :] * u[pad - 1:pad - 1 + tm]
                + w[2:3, :] * u[pad:pad + tm])

    gate = conv(_dot(hn, wg_ref[...]), cwg_ref[...], cbg_ref[...])
    val = conv(_dot(hn, wv_ref[...]), cwv_ref[...], cbv_ref[...])
    act = (gate * _sigmoid(gate) * val).astype(BF16)
    acc_ref[...] += _dot(act, wd_ref[...])

    @pl.when(f == pl.num_programs(1) - 1)
    def _():
        if not final_norm:
            o_ref[...] = h_ref[...] + acc_ref[...]
        else:
            slab = SUBLANE * 16
            assert tm % slab == 0

            def finish(r, carry):
                rows = pl.ds(pl.multiple_of(r * slab, slab), slab)
                o_ref[rows, :] = _rms_rows(h_ref[rows, :] + acc_ref[rows, :], gf_ref[...])
                return carry

            lax.fori_loop(0, tm // slab, finish, 0)


def _ffn(h, g, w_up, conv_w, conv_b, w_down, g_final, batch, seq, *, tm, tf):
    m, d = h.shape
    dff = w_down.shape[0]
    assert seq % tm == 0 and dff % tf == 0
    per_b = seq // tm
    nf = dff // tf
    pad = SUBLANE * 2
    halo_blocks = tm // pad
    cw = jnp.zeros((SUBLANE, 2 * dff), F32).at[:CONV_WIDTH].set(conv_w)
    cb = conv_b.reshape(1, 2 * dff)
    return pl.pallas_call(
        functools.partial(_ffn_kernel, per_b=per_b, final_norm=g_final is not None),
        grid=(m // tm, nf),
        in_specs=[pl.BlockSpec((tm, d), lambda i, f: (i, 0)),
                  pl.BlockSpec((pad, d), lambda i, f: (jnp.maximum(i * halo_blocks - 1, 0), 0)),
                  pl.BlockSpec((1, d), lambda i, f: (0, 0)),
                  pl.BlockSpec((d, tf), lambda i, f: (0, f)),
                  pl.BlockSpec((d, tf), lambda i, f: (0, nf + f)),
                  pl.BlockSpec((SUBLANE, tf), lambda i, f: (0, f)),
                  pl.BlockSpec((SUBLANE, tf), lambda i, f: (0, nf + f)),
                  pl.BlockSpec((1, tf), lambda i, f: (0, f)),
                  pl.BlockSpec((1, tf), lambda i, f: (0, nf + f)),
                  pl.BlockSpec((tf, d), lambda i, f: (f, 0)),
                  pl.BlockSpec((1, d), lambda i, f: (0, 0))],
        out_specs=pl.BlockSpec((tm, d), lambda i, f: (i, 0), pipeline_mode=pl.Buffered(1)),
        out_shape=jax.ShapeDtypeStruct((m, d), F32),
        scratch_shapes=[pltpu.VMEM((tm + pad, d), BF16), pltpu.VMEM((tm, d), F32)],
        compiler_params=_params("parallel", "arbitrary"), name="conv_ffn",
    )(h, h, g.reshape(1, d), w_up, w_up, cw, cw, cb, cb, w_down,
      (g if g_final is None else g_final).reshape(1, d))


def _row_tile(m, want):
    return want if m % want == 0 else m


def _mixer_ab(h, g, w_in, pe, w1, w2, w_out, tabs, rot, batch, seq):
    m, d = h.shape
    cos, sin = tabs[0], tabs[1]
    w_pad = jnp.zeros((d, AB_BLOCKS * LANE), BF16).at[:, :AB_COLS].set(w_in.astype(BF16))
    proj = _norm_matmul(h, g, w_pad, tm=_row_tile(m, 1024), tn=2048, head_major=True)
    o_moba = _moba(proj, cos, sin, rot, batch, seq)
    kvc = _compress(proj, pe, w1, w2, batch, seq)
    o_cmp, sel = _nsa_cmp(proj, kvc, batch, seq)
    o_nsa = _nsa_sw(proj, sel, o_cmp, cos, sin, rot, batch, seq)
    return _matmul_res([o_moba, o_nsa], w_out.astype(BF16), h, tm=_row_tile(m, 1024), tn=1024)


def _mixer_c(h, g, w_in, gn, w_out, tabs, batch, seq):
    m, d = h.shape
    proj = _norm_matmul(h, g, w_in.astype(BF16), tm=_row_tile(m, 1024), tn=2048)
    y = _retention(proj, tabs[2], tabs[3], gn, batch, seq)
    return _matmul_res([y], w_out.astype(BF16), h, tm=_row_tile(m, 1024), tn=1024)


def kernel(x, mem, positions, norm_mix, norm_cross, norm_ffn, norm_mem, norm_final, w_in_ab, cmp_pe_k, cmp_w1_k,
           cmp_w2_k, cmp_pe_v, cmp_w1_v, cmp_w2_v, w_out_ab, w_in_c, ret_gn, w_out_c, w_q_x, w_kv_x, w_o_x, w_up,
           conv_w, conv_b, w_down):
    batch, seq, d = x.shape
    n_mem = mem.shape[1]
    depth = norm_mix.shape[0]
    m = batch * seq
    tabs = _rope_tables(positions)
    rot = _rot_matrix()

    kv = _norm_matmul(mem.reshape(batch * n_mem, d), norm_mem, w_kv_x.astype(BF16),
                      tm=_row_tile(batch * n_mem, 512), tn=512)

    h = x.reshape(m, d)
    for l in range(depth):
        if l % 2 == 0:
            e = l // 2
            h = _mixer_ab(h, norm_mix[l], w_in_ab[e],
                          jnp.stack([cmp_pe_k[e], cmp_pe_v[e]]), jnp.stack([cmp_w1_k[e], cmp_w1_v[e]]),
                          jnp.stack([cmp_w2_k[e], cmp_w2_v[e]]), w_out_ab[e], tabs, rot, batch, seq)
        else:
            o = l // 2
            h = _mixer_c(h, norm_mix[l], w_in_c[o], ret_gn[o], w_out_c[o], tabs, batch, seq)
        h = _cross(h, norm_cross[l], w_q_x[l].astype(BF16), kv, w_o_x[l].astype(BF16), l, batch, seq, n_mem,
                   tm=_row_tile(seq, 1024))
        h = _ffn(h, norm_ffn[l], w_up[l].astype(BF16), conv_w[l], conv_b[l], w_down[l].astype(BF16),
                 norm_final if l == depth - 1 else None, batch, seq, tm=_row_tile(seq, 1024), tf=512)
    return h.reshape(batch, seq, d)
```
